```python
import jax, jax.numpy as jnp
from jax import lax
import numpy as np

D_MODEL = 2048
BATCH = 2
SEQ = 8192
DEPTH = 2

PLE_DIM = 256
GM_GROUPS = 8
GM_GROUP_W = 128
GM_W = GM_GROUPS * GM_GROUP_W
CHUNK = 128
MLA_HEADS = 8
Q_LORA = 512
KV_LORA = 256
QK_NOPE = 128
QK_ROPE = 64
V_HEAD = 128
ROPE_THETA = 10000.0
Q_BLOCK = 128
IN_SPLITS = (GM_W, GM_W, Q_LORA, KV_LORA, QK_ROPE, D_MODEL, D_MODEL)
IN_COLS = GM_W + GM_W + Q_LORA + KV_LORA + QK_ROPE + D_MODEL + D_MODEL
D_FF = 7 * D_MODEL // 2
N_EXPERTS = 8
TOP_K = 2
MOE_BLOCK = 512
N_DENSE = (DEPTH + 1) // 2
N_MOE = DEPTH // 2
EPS = 1e-6

kernel_name = "hybrid_gmlp_mla_moe_encoder"


def rms_norm(x, g):
    xf = x.astype(jnp.float32)
    y = xf * lax.rsqrt(jnp.mean(xf * xf, axis=-1, keepdims=True) + EPS)
    return (y * g.astype(jnp.float32)).astype(x.dtype)


def group_layer_norm(x, g, b):
    xf = x.astype(jnp.float32)
    mu = jnp.mean(xf, axis=-1, keepdims=True)
    var = jnp.mean(jnp.square(xf - mu), axis=-1, keepdims=True)
    y = (xf - mu) * lax.rsqrt(var + EPS)
    return (y * g.astype(jnp.float32) + b.astype(jnp.float32)).astype(x.dtype)


def rope_tables(positions):
    inv = ROPE_THETA ** (-jnp.arange(0, QK_ROPE, 2, dtype=jnp.float32) / QK_ROPE)
    ang = positions.astype(jnp.float32)[..., None] * inv
    return jnp.cos(ang), jnp.sin(ang)


def apply_rope(x, cos, sin):
    xf = x.astype(jnp.float32)
    x1, x2 = jnp.split(xf, 2, axis=-1)
    return jnp.concatenate([x1 * cos - x2 * sin, x2 * cos + x1 * sin], axis=-1).astype(x.dtype)


def split_in(z):
    outs, off = [], 0
    for w in IN_SPLITS:
        outs.append(z[..., off:off + w])
        off += w
    return outs


def gmlp_branch(u, v, ln_g, ln_b, w_s, b_s):
    b, s, _ = u.shape
    u = jax.nn.gelu(u)
    v = jax.nn.gelu(v).reshape(b, s // CHUNK, CHUNK, GM_GROUPS, GM_GROUP_W)
    v = group_layer_norm(v, ln_g.reshape(GM_GROUPS, GM_GROUP_W), ln_b.reshape(GM_GROUPS, GM_GROUP_W))
    mixed = jnp.einsum('gts,bnsgc->bntgc', w_s, v) + b_s.T[:, :, None]
    return u * mixed.reshape(b, s, GM_W)


def mla_branch(c_q, c_kv, k_rope, q_norm, w_uq, kv_norm, w_ukv, cos, sin):
    b, s, _ = c_q.shape
    q = (rms_norm(c_q, q_norm) @ w_uq).reshape(b, s, MLA_HEADS, QK_NOPE + QK_ROPE)
    q_nope = q[..., :QK_NOPE]
    q_rope = apply_rope(q[..., QK_NOPE:], cos[:, :, None], sin[:, :, None])
    kv = (rms_norm(c_kv, kv_norm) @ w_ukv).reshape(b, s, MLA_HEADS, QK_NOPE + V_HEAD)
    k_nope, v = kv[..., :QK_NOPE], kv[..., QK_NOPE:]
    k_rope = apply_rope(k_rope, cos, sin)
    scale = (QK_NOPE + QK_ROPE) ** -0.5
    nb = s // Q_BLOCK

    def to_blocks(t):
        return jnp.moveaxis(t.reshape(b, nb, Q_BLOCK, *t.shape[2:]), 1, 0)

    def attend(qb):
        qn, qr = qb
        sc = (jnp.einsum('bqhd,bkhd->bhqk', qn, k_nope)
              + jnp.einsum('bqhr,bkr->bhqk', qr, k_rope))
        pr = jax.nn.softmax(sc.astype(jnp.float32) * scale, axis=-1).astype(v.dtype)
        return jnp.einsum('bhqk,bkhd->bqhd', pr, v)

    o = lax.map(attend, (to_blocks(q_nope), to_blocks(q_rope)))
    return jnp.moveaxis(o, 0, 1).reshape(b, s, MLA_HEADS * V_HEAD)


def swiglu(x, w_gu, w_down):
    g, u = jnp.split(x @ w_gu, 2, axis=-1)
    return (jax.nn.silu(g) * u) @ w_down


def moe_swiglu(x, w_router, w_gu, w_down):
    b, s, d = x.shape
    xt = x.reshape(-1, d)
    n = xt.shape[0]
    a = n * TOP_K
    logits = (xt @ w_router).astype(jnp.float32)
    top_val, top_idx = lax.top_k(logits, TOP_K)
    gate = jax.nn.softmax(top_val, axis=-1)
    flat_e = top_idx.reshape(a).astype(jnp.int32)
    flat_w = gate.reshape(a)
    flat_tok = jnp.arange(a, dtype=jnp.int32) // TOP_K
    order = jnp.argsort(flat_e)
    e_sorted = flat_e[order]
    counts = jnp.bincount(flat_e, length=N_EXPERTS).astype(jnp.int32)
    padded = (counts + MOE_BLOCK - 1) // MOE_BLOCK * MOE_BLOCK
    start = jnp.cumsum(counts) - counts
    pad_end = jnp.cumsum(padded)
    pad_start = pad_end - padded
    dest = pad_start[e_sorted] + jnp.arange(a, dtype=jnp.int32) - start[e_sorted]
    n_blocks = -(-a // MOE_BLOCK) + N_EXPERTS
    n_slots = n_blocks * MOE_BLOCK
    slot_tok = jnp.zeros((n_slots,), jnp.int32).at[dest].set(flat_tok[order])
    slot_w = jnp.zeros((n_slots,), jnp.float32).at[dest].set(flat_w[order])
    block_e = jnp.minimum(
        jnp.searchsorted(pad_end, jnp.arange(n_blocks, dtype=jnp.int32) * MOE_BLOCK, side='right'),
        N_EXPERTS - 1)
    xs = xt[slot_tok].reshape(n_blocks, MOE_BLOCK, d)

    def expert_block(args):
        xb, e = args
        return swiglu(xb, w_gu[e], w_down[e])

    ys = lax.map(expert_block, (xs, block_e)).reshape(n_slots, d)
    ys = ys * slot_w[:, None].astype(ys.dtype)
    y = jnp.zeros_like(xt).at[slot_tok].add(ys)
    return y.reshape(b, s, d)


def setup_inputs(seed: int = 0) -> dict:
    key = jax.random.key(seed)
    ks = jax.random.split(key, 32)
    f32 = jnp.float32

    def nrm(k, shape, fan_in):
        return jax.random.normal(k, shape, f32) * (fan_in ** -0.5)

    def gain(k, shape):
        return 1.0 + 0.02 * jax.random.normal(k, shape, f32)

    return {
        "x": jax.random.normal(ks[0], (BATCH, SEQ, D_MODEL), f32),
        "p": jax.random.normal(ks[1], (DEPTH, BATCH, SEQ, PLE_DIM), f32),
        "positions": jnp.broadcast_to(jnp.arange(SEQ, dtype=jnp.int32), (BATCH, SEQ)),
        "norm_pre_mix": gain(ks[2], (DEPTH, D_MODEL)),
        "norm_post_mix": gain(ks[3], (DEPTH, D_MODEL)),
        "norm_pre_ffn": gain(ks[4], (DEPTH, D_MODEL)),
        "norm_post_ffn": gain(ks[5], (DEPTH, D_MODEL)),
        "w_in": nrm(ks[6], (DEPTH, D_MODEL, IN_COLS), D_MODEL),
        "gm_ln_g": gain(ks[7], (DEPTH, GM_W)),
        "gm_ln_b": 0.02 * jax.random.normal(ks[8], (DEPTH, GM_W), f32),
        "gm_ws": nrm(ks[9], (DEPTH, GM_GROUPS, CHUNK, CHUNK), CHUNK),
        "gm_bs": 1.0 + 0.02 * jax.random.normal(ks[10], (DEPTH, GM_GROUPS, CHUNK), f32),
        "mla_q_norm": gain(ks[11], (DEPTH, Q_LORA)),
        "w_uq": nrm(ks[12], (DEPTH, Q_LORA, MLA_HEADS * (QK_NOPE + QK_ROPE)), Q_LORA),
        "mla_kv_norm": gain(ks[13], (DEPTH, KV_LORA)),
        "w_ukv": nrm(ks[14], (DEPTH, KV_LORA, MLA_HEADS * (QK_NOPE + V_HEAD)), KV_LORA),
        "w_branch_a": nrm(ks[15], (DEPTH, GM_W, D_MODEL), GM_W),
        "w_branch_b": nrm(ks[16], (DEPTH, MLA_HEADS * V_HEAD, D_MODEL), MLA_HEADS * V_HEAD),
        "w_out": nrm(ks[17], (DEPTH, D_MODEL, D_MODEL), D_MODEL),
        "w_dense_gu": nrm(ks[18], (N_DENSE, D_MODEL, 2 * D_FF), D_MODEL),
        "w_dense_down": nrm(ks[19], (N_DENSE, D_FF, D_MODEL), D_FF),
        "w_router": nrm(ks[20], (N_MOE, D_MODEL, N_EXPERTS), D_MODEL),
        "w_exp_gu": nrm(ks[21], (N_MOE, N_EXPERTS, D_MODEL, 2 * D_FF), D_MODEL),
        "w_exp_down": nrm(ks[22], (N_MOE, N_EXPERTS, D_FF, D_MODEL), D_FF),
        "w_ple_in": nrm(ks[23], (DEPTH, PLE_DIM, D_MODEL), PLE_DIM),
        "w_ple_gate": nrm(ks[24], (DEPTH, D_MODEL, D_MODEL), D_MODEL),
        "ple_norm": gain(ks[25], (DEPTH, D_MODEL)),
    }


def reference(x, p, positions, norm_pre_mix, norm_post_mix, norm_pre_ffn, norm_post_ffn,
              w_in, gm_ln_g, gm_ln_b, gm_ws, gm_bs, mla_q_norm, w_uq, mla_kv_norm, w_ukv,
              w_branch_a, w_branch_b, w_out, w_dense_gu, w_dense_down, w_router, w_exp_gu,
              w_exp_down, w_ple_in, w_ple_gate, ple_norm):
    cos, sin = rope_tables(positions)
    h = x
    for i in range(DEPTH):
        hn = rms_norm(h, norm_pre_mix[i])
        u, v, c_q, c_kv, k_r, g_a, g_b = split_in(hn @ w_in[i])
        y_a = gmlp_branch(u, v, gm_ln_g[i], gm_ln_b[i], gm_ws[i], gm_bs[i]) @ w_branch_a[i]
        y_b = mla_branch(c_q, c_kv, k_r, mla_q_norm[i], w_uq[i], mla_kv_norm[i], w_ukv[i],
                         cos, sin) @ w_branch_b[i]
        mixed = (jax.nn.sigmoid(g_a) * y_a + jax.nn.sigmoid(g_b) * y_b) @ w_out[i]
        h = h + rms_norm(mixed, norm_post_mix[i])
        hn = rms_norm(h, norm_pre_ffn[i])
        if i % 2 == 0:
            f = swiglu(hn, w_dense_gu[i // 2], w_dense_down[i // 2])
        else:
            f = moe_swiglu(hn, w_router[i // 2], w_exp_gu[i // 2], w_exp_down[i // 2])
        h = h + rms_norm(f, norm_post_ffn[i])
        e = p[i] @ w_ple_in[i]
        gate = jax.nn.sigmoid(h @ w_ple_gate[i])
        h = h + rms_norm(gate * e, ple_norm[i])
    return h
```

```python
import functools

import jax
import jax.numpy as jnp
from jax import lax
from jax.experimental import pallas as pl
from jax.experimental.pallas import tpu as pltpu

F32 = jnp.float32
BF16 = jnp.bfloat16

GM_GROUPS = 8
GM_GROUP_W = 128
GM_W = GM_GROUPS * GM_GROUP_W
CHUNK = 128
MLA_HEADS = 8
Q_LORA = 512
KV_LORA = 256
QK_NOPE = 128
QK_ROPE = 64
V_HEAD = 128
ROPE_THETA = 10000.0
N_EXPERTS = 8
TOP_K = 2
MOE_BLOCK = 512
EPS = 1e-6

LANES = 128
HEAD_PAD = 2 * LANES
V7X_VMEM_BYTES = 64 * 1024 * 1024
VMEM_LIMIT = V7X_VMEM_BYTES - 8 * 1024 * 1024


def _params(*sem):
    return pltpu.CompilerParams(dimension_semantics=sem, vmem_limit_bytes=VMEM_LIMIT)


def _tile(n, want):
    t = min(n, want)
    while n % t:
        t //= 2
    return t


def _rms(x, g):
    return x * lax.rsqrt(jnp.mean(x * x, axis=-1, keepdims=True) + EPS) * g


def _rope_table_kernel(pos_ref, inv_ref, sgn_ref, cos_ref, sin_ref):
    ang = pos_ref[...].astype(F32) * inv_ref[...]
    cos_ref[...] = jnp.cos(ang) * jnp.abs(sgn_ref[...])
    sin_ref[...] = jnp.sin(ang) * sgn_ref[...]


def _rope_tables(positions):
    n = positions.size
    tm = _tile(n, 1024)
    half = QK_ROPE // 2
    inv = ROPE_THETA ** (-jnp.arange(0, QK_ROPE, 2, dtype=F32) / QK_ROPE)
    z = jnp.zeros((half,), F32)
    inv_row = jnp.concatenate([inv, z, inv, z]).reshape(1, LANES)
    one = jnp.ones((half,), F32)
    sgn_row = jnp.concatenate([-one, z, one, z]).reshape(1, LANES)
    row = pl.BlockSpec((1, LANES), lambda i: (0, 0))
    tab = pl.BlockSpec((tm, LANES), lambda i: (i, 0))
    return pl.pallas_call(
        _rope_table_kernel,
        grid=(n // tm,),
        in_specs=[pl.BlockSpec((tm, 1), lambda i: (i, 0)), row, row],
        out_specs=[tab, tab],
        out_shape=[jax.ShapeDtypeStruct((n, LANES), F32)] * 2,
        compiler_params=_params("parallel"),
        name="rope_tables",
    )(positions.reshape(n, 1), inv_row, sgn_row)


def _rope_slot(x, cos_t, sin_t):
    return x * cos_t + pltpu.roll(x, LANES // 2, axis=1) * sin_t


def _norm_kernel(h_ref, g_ref, o_ref):
    o_ref[...] = _rms(h_ref[...], g_ref[...]).astype(o_ref.dtype)


def _norm_bf16(h, g):
    n, d = h.shape
    tm = _tile(n, 512)
    return pl.pallas_call(
        _norm_kernel,
        grid=(n // tm,),
        in_specs=[pl.BlockSpec((tm, d), lambda i: (i, 0)), pl.BlockSpec((1, d), lambda i: (0, 0))],
        out_specs=pl.BlockSpec((tm, d), lambda i: (i, 0)),
        out_shape=jax.ShapeDtypeStruct((n, d), BF16),
        compiler_params=_params("parallel"),
        name="rmsnorm",
    )(h, g.reshape(1, d))


def _norm_router_kernel(h_ref, g_ref, wr_ref, o_ref, lg_ref):
    y = _rms(h_ref[...], g_ref[...])
    o_ref[...] = y.astype(o_ref.dtype)
    lg_ref[...] = jnp.dot(y, wr_ref[...], preferred_element_type=F32,
                          precision=lax.Precision.HIGHEST)


def _norm_router(h, g, w_router):
    n, d = h.shape
    tm = _tile(n, 512)
    wr = jnp.zeros((d, LANES), F32).at[:, :N_EXPERTS].set(w_router)
    return pl.pallas_call(
        _norm_router_kernel,
        grid=(n // tm,),
        in_specs=[pl.BlockSpec((tm, d), lambda i: (i, 0)), pl.BlockSpec((1, d), lambda i: (0, 0)),
                  pl.BlockSpec((d, LANES), lambda i: (0, 0))],
        out_specs=[pl.BlockSpec((tm, d), lambda i: (i, 0)), pl.BlockSpec((tm, LANES), lambda i: (i, 0))],
        out_shape=[jax.ShapeDtypeStruct((n, d), BF16), jax.ShapeDtypeStruct((n, LANES), F32)],
        compiler_params=_params("parallel"),
        name="rmsnorm_router",
    )(h, g.reshape(1, d), wr)


def _mm_kernel(epi, n_extra, x_ref, w_ref, *rest):
    extras = rest[:n_extra]
    outs = rest[n_extra:]
    acc = jnp.dot(x_ref[...], w_ref[...], preferred_element_type=F32)
    res = epi(acc, *extras)
    if not isinstance(res, (tuple, list)):
        res = (res,)
    for o, r in zip(outs, res):
        o[...] = r.astype(o.dtype)


def _mm(x, w, epi, *, tm, tn, extras=(), outs, name):
    m, k = x.shape
    n = w.shape[1]
    tm = _tile(m, tm)
    tn = _tile(n, tn)

    def spec(ncols, kind):
        if kind == "tile":
            return pl.BlockSpec((tm, tn), lambda j, i: (i, j))
        if kind == "rows":
            return pl.BlockSpec((tm, ncols), lambda j, i: (i, 0))
        assert kind == "vec", kind
        return pl.BlockSpec((1, ncols), lambda j, i: (0, 0))

    in_specs = [pl.BlockSpec((tm, k), lambda j, i: (i, 0)), pl.BlockSpec((k, tn), lambda j, i: (0, j))]
    in_specs += [spec(a.shape[1], kind) for a, kind in extras]
    return pl.pallas_call(
        functools.partial(_mm_kernel, epi, len(extras)),
        grid=(n // tn, m // tm),
        in_specs=in_specs,
        out_specs=[spec(nc, kind) for nc, _, kind in outs],
        out_shape=[jax.ShapeDtypeStruct((m, nc), dt) for nc, dt, _ in outs],
        compiler_params=_params("parallel", "parallel"),
        name=name,
    )(x, w, *[a for a, _ in extras])


def _epi_gelu(acc):
    return jax.nn.gelu(acc)


def _epi_gelu_group_ln(acc, g_ref, b_ref):
    y = jax.nn.gelu(acc)
    g = g_ref[...]
    b = b_ref[...]
    cols = []
    for gi in range(acc.shape[1] // GM_GROUP_W):
        sl = slice(gi * GM_GROUP_W, (gi + 1) * GM_GROUP_W)
        yg = y[:, sl]
        mu = jnp.mean(yg, axis=-1, keepdims=True)
        dlt = yg - mu
        var = jnp.mean(dlt * dlt, axis=-1, keepdims=True)
        cols.append(dlt * lax.rsqrt(var + EPS) * g[:, sl] + b[:, sl])
    return jnp.concatenate(cols, axis=1)


def _epi_sigmoid(acc):
    return jax.nn.sigmoid(acc)


def _epi_latent(acc, qn_ref, kvn_ref, cos_ref, sin_ref):
    c_q = _rms(acc[:, :Q_LORA], qn_ref[...])
    c_kv = _rms(acc[:, Q_LORA:Q_LORA + KV_LORA], kvn_ref[...])
    k_r = _rope_slot(acc[:, Q_LORA + KV_LORA:], cos_ref[...], sin_ref[...])
    return c_q, c_kv, k_r


def _epi_q(scale, acc, cos_ref, sin_ref):
    cos_t = cos_ref[...]
    sin_t = sin_ref[...]
    cols = []
    for hd in range(MLA_HEADS):
        base = hd * HEAD_PAD
        cols.append(acc[:, base:base + QK_NOPE] * scale)
        cols.append(_rope_slot(acc[:, base + QK_NOPE:base + HEAD_PAD], cos_t, sin_t) * scale)
    return jnp.concatenate(cols, axis=1)


def _kv_kernel(c_ref, wk_ref, wv_ref, kr_ref, k_ref, v_ref):
    c = c_ref[...]
    kn = jnp.dot(c, wk_ref[...], preferred_element_type=F32).astype(k_ref.dtype)
    kr = kr_ref[...]
    for hd in range(MLA_HEADS):
        k_ref[:, hd * HEAD_PAD:hd * HEAD_PAD + QK_NOPE] = kn[:, hd * QK_NOPE:(hd + 1) * QK_NOPE]
        k_ref[:, hd * HEAD_PAD + QK_NOPE:(hd + 1) * HEAD_PAD] = kr
    v_ref[...] = jnp.dot(c, wv_ref[...], preferred_element_type=F32).astype(v_ref.dtype)


def _kv_proj(c_kv, w_uk, w_uv, k_rope):
    n, r = c_kv.shape
    tm = _tile(n, 512)
    kw = MLA_HEADS * HEAD_PAD
    vw = MLA_HEADS * V_HEAD
    return pl.pallas_call(
        _kv_kernel,
        grid=(n // tm,),
        in_specs=[pl.BlockSpec((tm, r), lambda i: (i, 0)),
                  pl.BlockSpec(w_uk.shape, lambda i: (0, 0)),
                  pl.BlockSpec(w_uv.shape, lambda i: (0, 0)),
                  pl.BlockSpec((tm, LANES), lambda i: (i, 0))],
        out_specs=[pl.BlockSpec((tm, kw), lambda i: (i, 0)), pl.BlockSpec((tm, vw), lambda i: (i, 0))],
        out_shape=[jax.ShapeDtypeStruct((n, kw), BF16), jax.ShapeDtypeStruct((n, vw), BF16)],
        compiler_params=_params("parallel"),
        name="mla_kv_proj",
    )(c_kv, w_uk, w_uv, k_rope)


def _attn_kernel(tk, q_ref, k_ref, v_ref, o_ref):
    q = q_ref[...]
    tq = q.shape[0]
    s_len = k_ref.shape[0]

    def body(i, carry):
        m, l, acc = carry
        off = pl.multiple_of(i * tk, tk)
        k = k_ref[pl.ds(off, tk), :]
        v = v_ref[pl.ds(off, tk), :]
        s = lax.dot_general(q, k, (((1,), (1,)), ((), ())), preferred_element_type=F32)
        m_new = jnp.maximum(m, jnp.max(s, axis=-1, keepdims=True))
        alpha = jnp.exp(m - m_new)
        p = jnp.exp(s - m_new)
        l_new = alpha * l + jnp.sum(p, axis=-1, keepdims=True)
        acc_new = alpha * acc + jnp.dot(p.astype(v.dtype), v, preferred_element_type=F32)
        return m_new, l_new, acc_new

    init = (jnp.full((tq, 1), -jnp.inf, F32), jnp.zeros((tq, 1), F32),
            jnp.zeros((tq, v_ref.shape[1]), F32))
    _, l, acc = lax.fori_loop(0, s_len // tk, body, init)
    o_ref[...] = (acc / l).astype(o_ref.dtype)


def _attention(q, k, v, batch, seq):
    n = q.shape[0]
    tq = _tile(seq, 512)
    tk = _tile(seq, 512)
    nq = seq // tq
    return pl.pallas_call(
        functools.partial(_attn_kernel, tk),
        grid=(batch, MLA_HEADS, nq),
        in_specs=[pl.BlockSpec((tq, HEAD_PAD), lambda b, h, i: (b * nq + i, h)),
                  pl.BlockSpec((seq, HEAD_PAD), lambda b, h, i: (b, h)),
                  pl.BlockSpec((seq, V_HEAD), lambda b, h, i: (b, h))],
        out_specs=pl.BlockSpec((tq, V_HEAD), lambda b, h, i: (b * nq + i, h)),
        out_shape=jax.ShapeDtypeStruct((n, MLA_HEADS * V_HEAD), BF16),
        compiler_params=_params("parallel", "parallel", "arbitrary"),
        name="mla_attention",
    )(q, k, v)


def _gmlp_kernel(u_ref, v_ref, ws_ref, bs_ref, o_ref):
    n_chunks = u_ref.shape[0] // CHUNK
    for c in range(n_chunks):
        rows = slice(c * CHUNK, (c + 1) * CHUNK)
        for g in range(GM_GROUPS):
            cols = slice(g * GM_GROUP_W, (g + 1) * GM_GROUP_W)
            mixed = jnp.dot(ws_ref[g], v_ref[rows, cols], preferred_element_type=F32)
            o_ref[rows, cols] = (u_ref[rows, cols].astype(F32) * (mixed + bs_ref[:, cols])).astype(o_ref.dtype)


def _gmlp_mix(u, v, ws, bs_tile):
    n, w = u.shape
    tm = _tile(n, 4 * CHUNK)
    blk = pl.BlockSpec((tm, w), lambda i: (i, 0))
    return pl.pallas_call(
        _gmlp_kernel,
        grid=(n // tm,),
        in_specs=[blk, blk, pl.BlockSpec(ws.shape, lambda i: (0, 0, 0)),
                  pl.BlockSpec(bs_tile.shape, lambda i: (0, 0))],
        out_specs=blk,
        out_shape=jax.ShapeDtypeStruct((n, w), BF16),
        compiler_params=_params("parallel"),
        name="gmlp_spatial",
    )(u, v, ws, bs_tile)


def _merge_kernel(a_ref, wa_ref, b_ref, wb_ref, ga_ref, gb_ref, o_ref):
    ya = jnp.dot(a_ref[...], wa_ref[...], preferred_element_type=F32)
    yb = jnp.dot(b_ref[...], wb_ref[...], preferred_element_type=F32)
    o_ref[...] = (ga_ref[...].astype(F32) * ya + gb_ref[...].astype(F32) * yb).astype(o_ref.dtype)


def _branch_merge(a, wa, b, wb, gates):
    n, ka = a.shape
    kb = b.shape[1]
    d = wa.shape[1]
    tm = _tile(n, 1024)
    tn = _tile(d, 1024)
    nj = d // tn
    return pl.pallas_call(
        _merge_kernel,
        grid=(nj, n // tm),
        in_specs=[pl.BlockSpec((tm, ka), lambda j, i: (i, 0)), pl.BlockSpec((ka, tn), lambda j, i: (0, j)),
                  pl.BlockSpec((tm, kb), lambda j, i: (i, 0)), pl.BlockSpec((kb, tn), lambda j, i: (0, j)),
                  pl.BlockSpec((tm, tn), lambda j, i: (i, j)),
                  pl.BlockSpec((tm, tn), lambda j, i: (i, j + nj))],
        out_specs=pl.BlockSpec((tm, tn), lambda j, i: (i, j)),
        out_shape=jax.ShapeDtypeStruct((n, d), BF16),
        compiler_params=_params("parallel", "parallel"),
        name="branch_merge",
    )(a, wa, b, wb, gates, gates)


def _epi_norm_residual(acc, h_ref, g_ref):
    return h_ref[...] + _rms(acc, g_ref[...])


def _swiglu_step(x, wg_ref, wu_ref, wd_ref):
    g = jnp.dot(x, wg_ref[...], preferred_element_type=F32)
    u = jnp.dot(x, wu_ref[...], preferred_element_type=F32)
    a = (jax.nn.silu(g) * u).astype(x.dtype)
    return jnp.dot(a, wd_ref[...], preferred_element_type=F32)


def _dense_ffn_kernel(x_ref, wg_ref, wu_ref, wd_ref, o_ref):
    f = pl.program_id(1)
    y = _swiglu_step(x_ref[...], wg_ref, wu_ref, wd_ref)

    @pl.when(f == 0)
    def _():
        o_ref[...] = y

    @pl.when(f > 0)
    def _():
        o_ref[...] += y


def _dense_ffn(x, w_gu, w_down):
    n, d = x.shape
    dff = w_down.shape[0]
    tm = _tile(n, 1024)
    tf = _tile(dff, 512)
    nf = dff // tf
    return pl.pallas_call(
        _dense_ffn_kernel,
        grid=(n // tm, nf),
        in_specs=[pl.BlockSpec((tm, d), lambda i, f: (i, 0)),
                  pl.BlockSpec((d, tf), lambda i, f: (0, f)),
                  pl.BlockSpec((d, tf), lambda i, f: (0, f + nf)),
                  pl.BlockSpec((tf, d), lambda i, f: (f, 0))],
        out_specs=pl.BlockSpec((tm, d), lambda i, f: (i, 0)),
        out_shape=jax.ShapeDtypeStruct((n, d), F32),
        compiler_params=_params("parallel", "arbitrary"),
        name="dense_swiglu",
    )(x, w_gu, w_gu, w_down)


def _moe_ffn_kernel(be_ref, nu_ref, x_ref, sw_ref, wg_ref, wu_ref, wd_ref, o_ref):
    b = pl.program_id(0)
    f = pl.program_id(1)
    last = pl.num_programs(1) - 1
    active = b < nu_ref[0]

    @pl.when(active)
    def _():
        y = _swiglu_step(x_ref[...], wg_ref, wu_ref, wd_ref)

        @pl.when(f == 0)
        def _():
            o_ref[...] = y

        @pl.when(f > 0)
        def _():
            o_ref[...] += y

        @pl.when(f == last)
        def _():
            o_ref[...] = o_ref[...] * sw_ref[...]

    @pl.when(jnp.logical_and(jnp.logical_not(active), f == 0))
    def _():
        o_ref[...] = jnp.zeros_like(o_ref)


def _moe_ffn(xs, slot_w, block_e, n_used, w_gu, w_down):
    n_slots, d = xs.shape
    dff = w_down.shape[1]
    tm = MOE_BLOCK
    tf = _tile(dff, 512)
    nf = dff // tf
    nb = n_slots // tm

    def _blk(b, nu):
        return jnp.minimum(b, nu[0] - 1)

    def _ff(b, f, nu):
        return jnp.where(b < nu[0], f, nf - 1)

    grid_spec = pltpu.PrefetchScalarGridSpec(
        num_scalar_prefetch=2,
        grid=(nb, nf),
        in_specs=[
            pl.BlockSpec((tm, d), lambda b, f, be, nu: (_blk(b, nu), 0)),
            pl.BlockSpec((tm, 1), lambda b, f, be, nu: (b, 0)),
            pl.BlockSpec((None, d, tf), lambda b, f, be, nu: (be[_blk(b, nu)], 0, _ff(b, f, nu))),
            pl.BlockSpec((None, d, tf), lambda b, f, be, nu: (be[_blk(b, nu)], 0, _ff(b, f, nu) + nf)),
            pl.BlockSpec((None, tf, d), lambda b, f, be, nu: (be[_blk(b, nu)], _ff(b, f, nu), 0)),
        ],
        out_specs=pl.BlockSpec((tm, d), lambda b, f, be, nu: (b, 0)),
    )
    return pl.pallas_call(
        _moe_ffn_kernel,
        grid_spec=grid_spec,
        out_shape=jax.ShapeDtypeStruct((n_slots, d), F32),
        compiler_params=_params("arbitrary", "arbitrary"),
        name="moe_swiglu",
    )(block_e, n_used, xs, slot_w, w_gu, w_gu, w_down)


def _ple_kernel(n_f, *refs):
    h_ref = refs[0]
    f_refs = refs[1:1 + n_f]
    gf_ref, p_ref, wp_ref, wg_ref, gp_ref, o_ref = refs[1 + n_f:]
    f = f_refs[0][...]
    for r in f_refs[1:]:
        f = f + r[...]
    h = h_ref[...] + _rms(f, gf_ref[...])
    e = jnp.dot(p_ref[...].astype(BF16), wp_ref[...], preferred_element_type=F32)
    gate = jax.nn.sigmoid(jnp.dot(h.astype(BF16), wg_ref[...], preferred_element_type=F32))
    o_ref[...] = h + _rms(gate * e, gp_ref[...])


def _ffn_post_ple(h, fs, g_post, p, w_ple_in, w_ple_gate, g_ple):
    n, d = h.shape
    pd = p.shape[1]
    tm = _tile(n, 256)
    blk = pl.BlockSpec((tm, d), lambda i: (i, 0))
    row = pl.BlockSpec((1, d), lambda i: (0, 0))
    return pl.pallas_call(
        functools.partial(_ple_kernel, len(fs)),
        grid=(n // tm,),
        in_specs=[blk] + [blk] * len(fs) + [row, pl.BlockSpec((tm, pd), lambda i: (i, 0)),
                                            pl.BlockSpec((pd, d), lambda i: (0, 0)),
                                            pl.BlockSpec((d, d), lambda i: (0, 0)), row],
        out_specs=blk,
        out_shape=jax.ShapeDtypeStruct((n, d), F32),
        compiler_params=_params("parallel"),
        name="ffn_post_ple",
    )(h, *fs, g_post.reshape(1, d), p, w_ple_in, w_ple_gate, g_ple.reshape(1, d))


def _route(logits, n_tok):
    a = n_tok * TOP_K
    top_val, top_idx = lax.top_k(logits, TOP_K)
    gate = jax.nn.softmax(top_val, axis=-1)
    flat_e = top_idx.reshape(a).astype(jnp.int32)
    flat_w = gate.reshape(a)
    onehot = (flat_e[:, None] == jnp.arange(N_EXPERTS, dtype=jnp.int32)[None, :]).astype(jnp.int32)
    csum = jnp.cumsum(onehot, axis=0)
    counts = csum[-1]
    rank = jnp.take_along_axis(csum, flat_e[:, None], axis=1)[:, 0] - 1
    padded = (counts + MOE_BLOCK - 1) // MOE_BLOCK * MOE_BLOCK
    pad_end = jnp.cumsum(padded)
    pad_start = pad_end - padded
    dest = pad_start[flat_e] + rank
    n_blocks = -(-a // MOE_BLOCK) + N_EXPERTS
    n_slots = n_blocks * MOE_BLOCK
    flat_tok = jnp.arange(a, dtype=jnp.int32) // TOP_K
    slot_tok = jnp.zeros((n_slots,), jnp.int32).at[dest].set(flat_tok)
    slot_w = jnp.zeros((n_slots,), F32).at[dest].set(flat_w)
    block_e = jnp.minimum(
        jnp.searchsorted(pad_end, jnp.arange(n_blocks, dtype=jnp.int32) * MOE_BLOCK, side='right'),
        N_EXPERTS - 1).astype(jnp.int32)
    n_used = (pad_end[-1] // MOE_BLOCK).astype(jnp.int32).reshape(1)
    return slot_tok, slot_w, block_e, n_used, dest.reshape(n_tok, TOP_K)


def _rope_slot_cols(w):
    half = QK_ROPE // 2
    z = jnp.zeros(w.shape[:-1] + (half,), w.dtype)
    return jnp.concatenate([w[..., :half], z, w[..., half:], z], axis=-1)


def _prep_layer(i, w_in, gm_ws, gm_bs, w_uq, w_ukv, w_branch_a, w_branch_b, w_out, w_ple_in, w_ple_gate):
    d = w_in.shape[1]
    wi = w_in[i]
    o = 0
    w_u = wi[:, o:o + GM_W]; o += GM_W
    w_v = wi[:, o:o + GM_W]; o += GM_W
    w_cq = wi[:, o:o + Q_LORA]; o += Q_LORA
    w_ckv = wi[:, o:o + KV_LORA]; o += KV_LORA
    w_kr = wi[:, o:o + QK_ROPE]; o += QK_ROPE
    w_g = wi[:, o:o + 2 * d]
    w_lat = jnp.concatenate([w_cq, w_ckv, _rope_slot_cols(w_kr)], axis=1)
    wq = w_uq[i].reshape(Q_LORA, MLA_HEADS, QK_NOPE + QK_ROPE)
    wq = jnp.concatenate([wq[..., :QK_NOPE], _rope_slot_cols(wq[..., QK_NOPE:])], axis=-1)
    wkv = w_ukv[i].reshape(KV_LORA, MLA_HEADS, QK_NOPE + V_HEAD)
    bs_tile = jnp.repeat(gm_bs[i].T, GM_GROUP_W, axis=1)
    return dict(
        w_u=w_u.astype(BF16), w_v=w_v.astype(BF16), w_lat=w_lat.astype(BF16), w_g=w_g.astype(BF16),
        w_q=wq.reshape(Q_LORA, MLA_HEADS * HEAD_PAD).astype(BF16),
        w_uk=wkv[..., :QK_NOPE].reshape(KV_LORA, MLA_HEADS * QK_NOPE).astype(BF16),
        w_uv=wkv[..., QK_NOPE:].reshape(KV_LORA, MLA_HEADS * V_HEAD).astype(BF16),
        ws=gm_ws[i].astype(BF16), bs_tile=bs_tile,
        w_a=w_branch_a[i].astype(BF16), w_b=w_branch_b[i].astype(BF16), w_o=w_out[i].astype(BF16),
        w_pi=w_ple_in[i].astype(BF16), w_pg=w_ple_gate[i].astype(BF16),
    )


def kernel(x, p, positions, norm_pre_mix, norm_post_mix, norm_pre_ffn, norm_post_ffn, w_in, gm_ln_g, gm_ln_b, gm_ws, gm_bs, mla_q_norm, w_uq, mla_kv_norm, w_ukv, w_branch_a, w_branch_b, w_out, w_dense_gu, w_dense_down, w_router, w_exp_gu, w_exp_down, w_ple_in, w_ple_gate, ple_norm):
    batch, seq, d = x.shape
    depth = p.shape[0]
    n = batch * seq
    scale = (QK_NOPE + QK_ROPE) ** -0.5
    cos_t, sin_t = _rope_tables(positions)
    h = x.reshape(n, d)

    for li in range(depth):
        wts = _prep_layer(li, w_in, gm_ws, gm_bs, w_uq, w_ukv, w_branch_a, w_branch_b, w_out,
                          w_ple_in, w_ple_gate)
        hn = _norm_bf16(h, norm_pre_mix[li])
        (u,) = _mm(hn, wts["w_u"], _epi_gelu, tm=1024, tn=GM_W, outs=[(GM_W, BF16, "tile")],
                   name="in_proj_u")
        (v,) = _mm(hn, wts["w_v"], _epi_gelu_group_ln, tm=512, tn=GM_W,
                   extras=[(gm_ln_g[li].reshape(1, GM_W), "vec"), (gm_ln_b[li].reshape(1, GM_W), "vec")],
                   outs=[(GM_W, BF16, "tile")], name="in_proj_v")
        c_q, c_kv, k_r = _mm(
            hn, wts["w_lat"], _epi_latent, tm=512, tn=wts["w_lat"].shape[1],
            extras=[(mla_q_norm[li].reshape(1, Q_LORA), "vec"), (mla_kv_norm[li].reshape(1, KV_LORA), "vec"),
                    (cos_t, "rows"), (sin_t, "rows")],
            outs=[(Q_LORA, BF16, "rows"), (KV_LORA, BF16, "rows"), (LANES, BF16, "rows")],
            name="in_proj_latent")
        (gates,) = _mm(hn, wts["w_g"], _epi_sigmoid, tm=1024, tn=1024, outs=[(2 * d, BF16, "tile")],
                       name="in_proj_gates")
        qw = MLA_HEADS * HEAD_PAD
        (q,) = _mm(c_q, wts["w_q"], functools.partial(_epi_q, scale), tm=512, tn=qw,
                   extras=[(cos_t, "rows"), (sin_t, "rows")], outs=[(qw, BF16, "tile")],
                   name="mla_q_proj")
        k, vv = _kv_proj(c_kv, wts["w_uk"], wts["w_uv"], k_r)
        attn = _attention(q, k, vv, batch, seq)
        gm = _gmlp_mix(u, v, wts["ws"], wts["bs_tile"])
        merged = _branch_merge(gm, wts["w_a"], attn, wts["w_b"], gates)
        (h,) = _mm(merged, wts["w_o"], _epi_norm_residual, tm=512, tn=d,
                   extras=[(h, "rows"), (norm_post_mix[li].reshape(1, d), "vec")],
                   outs=[(d, F32, "tile")], name="out_proj")
        if li % 2 == 0:
            hn = _norm_bf16(h, norm_pre_ffn[li])
            fs = [_dense_ffn(hn, w_dense_gu[li // 2].astype(BF16), w_dense_down[li // 2].astype(BF16))]
        else:
            hn, logits = _norm_router(h, norm_pre_ffn[li], w_router[li // 2])
            slot_tok, slot_w, block_e, n_used, dest = _route(logits[:, :N_EXPERTS], n)
            xs = jnp.take(hn, slot_tok, axis=0)
            ys = _moe_ffn(xs, slot_w.reshape(-1, 1), block_e, n_used,
                          w_exp_gu[li // 2].astype(BF16), w_exp_down[li // 2].astype(BF16))
            fs = [jnp.take(ys, dest[:, kk], axis=0) for kk in range(TOP_K)]
        h = _ffn_post_ple(h, fs, norm_post_ffn[li], p[li].reshape(n, -1), wts["w_pi"], wts["w_pg"],
                          ple_norm[li])
    return h.reshape(batch, seq, d)
```

```python
import functools

import jax
import jax.numpy as jnp
from jax import lax
from jax.experimental import pallas as pl
from jax.experimental.pallas import tpu as pltpu

F32 = jnp.float32
BF16 = jnp.bfloat16

GM_GROUPS = 8
GM_GROUP_W = 128
GM_W = GM_GROUPS * GM_GROUP_W
CHUNK = 128
MLA_HEADS = 8
Q_LORA = 512
KV_LORA = 256
QK_NOPE = 128
QK_ROPE = 64
V_HEAD = 128
ROPE_THETA = 10000.0
N_EXPERTS = 8
TOP_K = 2
MOE_BLOCK = 512
MOE_TILE = 2 * MOE_BLOCK
EPS = 1e-6
LOG2_E = 1.4426950408889634

LANES = 128
HEAD_PAD = 2 * LANES
V7X_VMEM_BYTES = 64 * 1024 * 1024
VMEM_LIMIT = V7X_VMEM_BYTES - 8 * 1024 * 1024


def _params(*sem):
    return pltpu.CompilerParams(dimension_semantics=sem, vmem_limit_bytes=VMEM_LIMIT)


def _tile(n, want):
    t = min(n, want)
    while n % t:
        t //= 2
    return t


def _rms(x, g):
    return x * lax.rsqrt(jnp.mean(x * x, axis=-1, keepdims=True) + EPS) * g


def _rope_table_kernel(pos_c_ref, pos_r_ref, inv_r_ref, sgn_r_ref, inv_c_ref, sgn_c_ref,
                       cos_ref, sin_ref, cos_t_ref, sin_t_ref):
    ang = pos_c_ref[...].astype(F32) * inv_r_ref[...]
    cos_ref[...] = jnp.cos(ang) * jnp.abs(sgn_r_ref[...])
    sin_ref[...] = jnp.sin(ang) * sgn_r_ref[...]
    ang_t = inv_c_ref[...] * pos_r_ref[...].astype(F32)
    cos_t_ref[...] = jnp.cos(ang_t) * jnp.abs(sgn_c_ref[...])
    sin_t_ref[...] = jnp.sin(ang_t) * sgn_c_ref[...]


def _rope_tables(positions):
    n = positions.size
    tm = _tile(n, 1024)
    half = QK_ROPE // 2
    inv = ROPE_THETA ** (-jnp.arange(0, QK_ROPE, 2, dtype=F32) / QK_ROPE)
    z = jnp.zeros((half,), F32)
    inv_slot = jnp.concatenate([inv, z, inv, z])
    one = jnp.ones((half,), F32)
    sgn_slot = jnp.concatenate([-one, z, one, z])
    row = pl.BlockSpec((1, LANES), lambda i: (0, 0))
    col = pl.BlockSpec((LANES, 1), lambda i: (0, 0))
    tab = pl.BlockSpec((tm, LANES), lambda i: (i, 0))
    tab_t = pl.BlockSpec((LANES, tm), lambda i: (0, i))
    return pl.pallas_call(
        _rope_table_kernel,
        grid=(n // tm,),
        in_specs=[pl.BlockSpec((tm, 1), lambda i: (i, 0)), pl.BlockSpec((1, tm), lambda i: (0, i)),
                  row, row, col, col],
        out_specs=[tab, tab, tab_t, tab_t],
        out_shape=[jax.ShapeDtypeStruct((n, LANES), F32)] * 2 + [jax.ShapeDtypeStruct((LANES, n), F32)] * 2,
        compiler_params=_params("parallel"),
        name="rope_tables",
    )(positions.reshape(n, 1), positions.reshape(1, n), inv_slot.reshape(1, LANES),
      sgn_slot.reshape(1, LANES), inv_slot.reshape(LANES, 1), sgn_slot.reshape(LANES, 1))


def _rope_slot(x, cos_t, sin_t):
    return x * cos_t + pltpu.roll(x, LANES // 2, axis=1) * sin_t


def _norm_kernel(h_ref, g_ref, o_ref):
    o_ref[...] = _rms(h_ref[...], g_ref[...]).astype(o_ref.dtype)


def _norm_bf16(h, g):
    n, d = h.shape
    tm = _tile(n, 512)
    return pl.pallas_call(
        _norm_kernel,
        grid=(n // tm,),
        in_specs=[pl.BlockSpec((tm, d), lambda i: (i, 0)), pl.BlockSpec((1, d), lambda i: (0, 0))],
        out_specs=pl.BlockSpec((tm, d), lambda i: (i, 0)),
        out_shape=jax.ShapeDtypeStruct((n, d), BF16),
        compiler_params=_params("parallel"),
        name="rmsnorm",
    )(h, g.reshape(1, d))


def _norm_router_kernel(h_ref, g_ref, wr_ref, o_ref, lg_ref):
    y = _rms(h_ref[...], g_ref[...])
    o_ref[...] = y.astype(o_ref.dtype)
    lg_ref[...] = jnp.dot(y, wr_ref[...], preferred_element_type=F32,
                          precision=lax.Precision.HIGHEST)


def _norm_router(h, g, w_router):
    n, d = h.shape
    tm = _tile(n, 512)
    wr = jnp.zeros((d, LANES), F32).at[:, :N_EXPERTS].set(w_router)
    return pl.pallas_call(
        _norm_router_kernel,
        grid=(n // tm,),
        in_specs=[pl.BlockSpec((tm, d), lambda i: (i, 0)), pl.BlockSpec((1, d), lambda i: (0, 0)),
                  pl.BlockSpec((d, LANES), lambda i: (0, 0))],
        out_specs=[pl.BlockSpec((tm, d), lambda i: (i, 0)), pl.BlockSpec((tm, LANES), lambda i: (i, 0))],
        out_shape=[jax.ShapeDtypeStruct((n, d), BF16), jax.ShapeDtypeStruct((n, LANES), F32)],
        compiler_params=_params("parallel"),
        name="rmsnorm_router",
    )(h, g.reshape(1, d), wr)


def _mm_kernel(epi, n_extra, x_ref, w_ref, *rest):
    extras = rest[:n_extra]
    outs = rest[n_extra:]
    acc = jnp.dot(x_ref[...], w_ref[...], preferred_element_type=F32)
    res = epi(acc, *extras)
    if not isinstance(res, (tuple, list)):
        res = (res,)
    for o, r in zip(outs, res):
        o[...] = r.astype(o.dtype)


def _mm(x, w, epi, *, tm, tn, extras=(), outs, name):
    m, k = x.shape
    n = w.shape[1]
    tm = _tile(m, tm)
    tn = _tile(n, tn)

    def spec(ncols, kind):
        if kind == "tile":
            return pl.BlockSpec((tm, tn), lambda j, i: (i, j))
        if kind == "rows":
            return pl.BlockSpec((tm, ncols), lambda j, i: (i, 0))
        assert kind == "vec", kind
        return pl.BlockSpec((1, ncols), lambda j, i: (0, 0))

    in_specs = [pl.BlockSpec((tm, k), lambda j, i: (i, 0)), pl.BlockSpec((k, tn), lambda j, i: (0, j))]
    in_specs += [spec(a.shape[1], kind) for a, kind in extras]
    return pl.pallas_call(
        functools.partial(_mm_kernel, epi, len(extras)),
        grid=(n // tn, m // tm),
        in_specs=in_specs,
        out_specs=[spec(nc, kind) for nc, _, kind in outs],
        out_shape=[jax.ShapeDtypeStruct((m, nc), dt) for nc, dt, _ in outs],
        compiler_params=_params("parallel", "parallel"),
        name=name,
    )(x, w, *[a for a, _ in extras])


def _epi_gelu(acc):
    return jax.nn.gelu(acc)


def _epi_gelu_group_ln(acc, g_ref, b_ref):
    y = jax.nn.gelu(acc)
    g = g_ref[...]
    b = b_ref[...]
    cols = []
    for gi in range(acc.shape[1] // GM_GROUP_W):
        sl = slice(gi * GM_GROUP_W, (gi + 1) * GM_GROUP_W)
        yg = y[:, sl]
        mu = jnp.mean(yg, axis=-1, keepdims=True)
        dlt = yg - mu
        var = jnp.mean(dlt * dlt, axis=-1, keepdims=True)
        cols.append(dlt * lax.rsqrt(var + EPS) * g[:, sl] + b[:, sl])
    return jnp.concatenate(cols, axis=1)


def _epi_sigmoid(acc):
    return jax.nn.sigmoid(acc)


def _epi_latent(acc, qn_ref, kvn_ref, cos_ref, sin_ref):
    c_q = _rms(acc[:, :Q_LORA], qn_ref[...])
    c_kv = _rms(acc[:, Q_LORA:Q_LORA + KV_LORA], kvn_ref[...])
    k_r = _rope_slot(acc[:, Q_LORA + KV_LORA:], cos_ref[...], sin_ref[...])
    return c_q, c_kv, k_r


def _q_proj_kernel(scale, c_ref, w_ref, cos_ref, sin_ref, o_ref):
    acc = lax.dot_general(w_ref[...], c_ref[...], (((1,), (1,)), ((), ())),
                          preferred_element_type=F32)
    cos_t = cos_ref[...] * scale
    sin_t = sin_ref[...] * scale
    half = LANES // 2
    for hd in range(MLA_HEADS):
        base = hd * HEAD_PAD
        o_ref[base:base + QK_NOPE, :] = (acc[base:base + QK_NOPE, :] * scale).astype(o_ref.dtype)
        r = acc[base + QK_NOPE:base + HEAD_PAD, :]
        rot = jnp.concatenate([r[half:, :], r[:half, :]], axis=0)
        o_ref[base + QK_NOPE:base + HEAD_PAD, :] = (r * cos_t + rot * sin_t).astype(o_ref.dtype)


def _q_proj(c_q, w_qt, cos_tt, sin_tt, scale):
    n, r = c_q.shape
    qw = w_qt.shape[0]
    tm = _tile(n, 512)
    tab = pl.BlockSpec((LANES, tm), lambda i: (0, i))
    return pl.pallas_call(
        functools.partial(_q_proj_kernel, scale),
        grid=(n // tm,),
        in_specs=[pl.BlockSpec((tm, r), lambda i: (i, 0)), pl.BlockSpec((qw, r), lambda i: (0, 0)), tab, tab],
        out_specs=pl.BlockSpec((qw, tm), lambda i: (0, i)),
        out_shape=jax.ShapeDtypeStruct((qw, n), BF16),
        compiler_params=_params("parallel"),
        name="mla_q_proj",
    )(c_q, w_qt, cos_tt, sin_tt)


def _kv_kernel(c_ref, wk_ref, wvt_ref, kr_ref, k_ref, vt_ref):
    c = c_ref[...]
    kn = jnp.dot(c, wk_ref[...], preferred_element_type=F32).astype(k_ref.dtype)
    kr = kr_ref[...]
    for hd in range(MLA_HEADS):
        k_ref[:, hd * HEAD_PAD:hd * HEAD_PAD + QK_NOPE] = kn[:, hd * QK_NOPE:(hd + 1) * QK_NOPE]
        k_ref[:, hd * HEAD_PAD + QK_NOPE:(hd + 1) * HEAD_PAD] = kr
    vt_ref[...] = lax.dot_general(wvt_ref[...], c, (((1,), (1,)), ((), ())),
                                  preferred_element_type=F32).astype(vt_ref.dtype)


def _kv_proj(c_kv, w_uk, w_uvt, k_rope):
    n, r = c_kv.shape
    tm = _tile(n, 512)
    kw = MLA_HEADS * HEAD_PAD
    vw = MLA_HEADS * V_HEAD
    return pl.pallas_call(
        _kv_kernel,
        grid=(n // tm,),
        in_specs=[pl.BlockSpec((tm, r), lambda i: (i, 0)),
                  pl.BlockSpec(w_uk.shape, lambda i: (0, 0)),
                  pl.BlockSpec(w_uvt.shape, lambda i: (0, 0)),
                  pl.BlockSpec((tm, LANES), lambda i: (i, 0))],
        out_specs=[pl.BlockSpec((tm, kw), lambda i: (i, 0)), pl.BlockSpec((vw, tm), lambda i: (0, i))],
        out_shape=[jax.ShapeDtypeStruct((n, kw), BF16), jax.ShapeDtypeStruct((vw, n), BF16)],
        compiler_params=_params("parallel"),
        name="mla_kv_proj",
    )(c_kv, w_uk, w_uvt, k_rope)


def _attn_kernel(tk, qt_ref, k_ref, vt_ref, o_ref, s_a, s_b, p_a, p_b):
    tq = qt_ref.shape[1]
    n_chunks = k_ref.shape[0] // tk
    qt = qt_ref[...]

    def scores(i, s_out):
        off = pl.multiple_of(i * tk, tk)
        s = jnp.dot(k_ref[pl.ds(off, tk), :], qt, preferred_element_type=F32)
        s_out[...] = s
        return jnp.max(s, axis=0, keepdims=True)

    def softmax(s_in, p_out, mx, m, l):
        m_new = jnp.maximum(m, mx)
        alpha = jnp.exp2(m - m_new)
        p = jnp.exp2(s_in[...] - m_new)
        p_out[...] = p.astype(p_out.dtype)
        return m_new, alpha * l + jnp.sum(p, axis=0, keepdims=True), alpha

    def values(i, p_in, alpha, acc):
        off = pl.multiple_of(i * tk, tk)
        return alpha * acc + jnp.dot(vt_ref[:, pl.ds(off, tk)], p_in[...], preferred_element_type=F32)

    def step(i, odd, carry, first=False, last=False):
        s_cur, s_nxt, p_cur, p_prv = (s_b, s_a, p_b, p_a) if odd else (s_a, s_b, p_a, p_b)
        mx, m, l, alpha, acc = carry
        if not first:
            acc = values(i - 1, p_prv, alpha, acc)
        mx_nxt = mx if last else scores(i + 1, s_nxt)
        m, l, alpha = softmax(s_cur, p_cur, mx, m, l)
        return mx_nxt, m, l, alpha, acc

    carry = (scores(0, s_a), jnp.full((1, tq), -jnp.inf, F32), jnp.zeros((1, tq), F32),
             jnp.ones((1, tq), F32), jnp.zeros((vt_ref.shape[0], tq), F32))
    carry = step(0, False, carry, first=True)

    def pair(j, c):
        c = step(2 * j + 1, True, c)
        return step(2 * j + 2, False, c)

    carry = lax.fori_loop(0, (n_chunks - 2) // 2, pair, carry)
    _, _, l, alpha, acc = step(n_chunks - 1, True, carry, last=True)
    acc = values(n_chunks - 1, p_b, alpha, acc)
    o_ref[...] = jnp.transpose(acc / l).astype(o_ref.dtype)


def _attention(qt, k, vt, batch, seq):
    n = k.shape[0]
    tq = _tile(seq, 512)
    tk = _tile(seq // 2, 512)
    assert (seq // tk) % 2 == 0
    nq = seq // tq
    return pl.pallas_call(
        functools.partial(_attn_kernel, tk),
        grid=(batch, MLA_HEADS, nq),
        scratch_shapes=[pltpu.VMEM((tk, tq), F32), pltpu.VMEM((tk, tq), F32),
                        pltpu.VMEM((tk, tq), BF16), pltpu.VMEM((tk, tq), BF16)],
        in_specs=[pl.BlockSpec((HEAD_PAD, tq), lambda b, h, i: (h, b * nq + i)),
                  pl.BlockSpec((seq, HEAD_PAD), lambda b, h, i: (b, h)),
                  pl.BlockSpec((V_HEAD, seq), lambda b, h, i: (h, b))],
        out_specs=pl.BlockSpec((tq, V_HEAD), lambda b, h, i: (b * nq + i, h)),
        out_shape=jax.ShapeDtypeStruct((n, MLA_HEADS * V_HEAD), BF16),
        compiler_params=_params("parallel", "parallel", "arbitrary"),
        name="mla_attention",
    )(qt, k, vt)


def _gmlp_kernel(u_ref, v_ref, ws_ref, bs_ref, o_ref):
    n_chunks = u_ref.shape[0] // CHUNK
    for c in range(n_chunks):
        rows = slice(c * CHUNK, (c + 1) * CHUNK)
        for g in range(GM_GROUPS):
            cols = slice(g * GM_GROUP_W, (g + 1) * GM_GROUP_W)
            mixed = jnp.dot(ws_ref[g], v_ref[rows, cols], preferred_element_type=F32)
            o_ref[rows, cols] = (u_ref[rows, cols].astype(F32) * (mixed + bs_ref[:, cols])).astype(o_ref.dtype)


def _gmlp_mix(u, v, ws, bs_tile):
    n, w = u.shape
    tm = _tile(n, 4 * CHUNK)
    blk = pl.BlockSpec((tm, w), lambda i: (i, 0))
    return pl.pallas_call(
        _gmlp_kernel,
        grid=(n // tm,),
        in_specs=[blk, blk, pl.BlockSpec(ws.shape, lambda i: (0, 0, 0)),
                  pl.BlockSpec(bs_tile.shape, lambda i: (0, 0))],
        out_specs=blk,
        out_shape=jax.ShapeDtypeStruct((n, w), BF16),
        compiler_params=_params("parallel"),
        name="gmlp_spatial",
    )(u, v, ws, bs_tile)


def _merge_kernel(a_ref, wa_ref, b_ref, wb_ref, ga_ref, gb_ref, o_ref):
    ya = jnp.dot(a_ref[...], wa_ref[...], preferred_element_type=F32)
    yb = jnp.dot(b_ref[...], wb_ref[...], preferred_element_type=F32)
    o_ref[...] = (ga_ref[...].astype(F32) * ya + gb_ref[...].astype(F32) * yb).astype(o_ref.dtype)


def _branch_merge(a, wa, b, wb, gates):
    n, ka = a.shape
    kb = b.shape[1]
    d = wa.shape[1]
    tm = _tile(n, 1024)
    tn = _tile(d, 1024)
    nj = d // tn
    return pl.pallas_call(
        _merge_kernel,
        grid=(nj, n // tm),
        in_specs=[pl.BlockSpec((tm, ka), lambda j, i: (i, 0)), pl.BlockSpec((ka, tn), lambda j, i: (0, j)),
                  pl.BlockSpec((tm, kb), lambda j, i: (i, 0)), pl.BlockSpec((kb, tn), lambda j, i: (0, j)),
                  pl.BlockSpec((tm, tn), lambda j, i: (i, j)),
                  pl.BlockSpec((tm, tn), lambda j, i: (i, j + nj))],
        out_specs=pl.BlockSpec((tm, tn), lambda j, i: (i, j)),
        out_shape=jax.ShapeDtypeStruct((n, d), BF16),
        compiler_params=_params("parallel", "parallel"),
        name="branch_merge",
    )(a, wa, b, wb, gates, gates)


def _epi_norm_residual(acc, h_ref, g_ref):
    return h_ref[...] + _rms(acc, g_ref[...])


def _swiglu_step(x, wg_ref, wu_ref, wd_ref):
    g = jnp.dot(x, wg_ref[...], preferred_element_type=F32)
    u = jnp.dot(x, wu_ref[...], preferred_element_type=F32)
    a = (jax.nn.silu(g) * u).astype(x.dtype)
    return jnp.dot(a, wd_ref[...], preferred_element_type=F32)


def _dense_ffn_kernel(x_ref, wg_ref, wu_ref, wd_ref, o_ref):
    f = pl.program_id(1)
    @pl.when(f == 0)
    def _():
        o_ref[...] = jnp.zeros_like(o_ref)

    o_ref[...] += _swiglu_step(x_ref[...], wg_ref, wu_ref, wd_ref)


def _dense_ffn(x, w_gu, w_down):
    n, d = x.shape
    dff = w_down.shape[0]
    tm = _tile(n, 1024)
    tf = _tile(dff, 512)
    nf = dff // tf
    return pl.pallas_call(
        _dense_ffn_kernel,
        grid=(n // tm, nf),
        in_specs=[pl.BlockSpec((tm, d), lambda i, f: (i, 0)),
                  pl.BlockSpec((d, tf), lambda i, f: (0, f)),
                  pl.BlockSpec((d, tf), lambda i, f: (0, f + nf)),
                  pl.BlockSpec((tf, d), lambda i, f: (f, 0))],
        out_specs=pl.BlockSpec((tm, d), lambda i, f: (i, 0)),
        out_shape=jax.ShapeDtypeStruct((n, d), F32),
        compiler_params=_params("parallel", "arbitrary"),
        name="dense_swiglu",
    )(x, w_gu, w_gu, w_down)


def _moe_ffn_kernel(te_ref, th_ref, nu_ref, x_ref, sw_ref, wg_ref, wu_ref, wd_ref, o_ref):
    t = pl.program_id(0)
    f = pl.program_id(1)
    last = pl.num_programs(1) - 1
    n_half = th_ref[t]

    @pl.when(f == 0)
    def _():
        o_ref[...] = jnp.zeros_like(o_ref)

    @pl.when(n_half > 0)
    def _():
        wg = wg_ref[...].astype(BF16)
        wu = wu_ref[...].astype(BF16)
        wd = wd_ref[...].astype(BF16)

        def half(rows):
            x = x_ref[rows, :]
            g = jnp.dot(x, wg, preferred_element_type=F32)
            u = jnp.dot(x, wu, preferred_element_type=F32)
            y = jnp.dot((jax.nn.silu(g) * u).astype(BF16), wd, preferred_element_type=F32)
            o_ref[rows, :] += y

            @pl.when(f == last)
            def _():
                o_ref[rows, :] = o_ref[rows, :] * sw_ref[rows, :]

        half(slice(0, MOE_BLOCK))

        @pl.when(n_half > 1)
        def _():
            half(slice(MOE_BLOCK, MOE_TILE))


def _moe_ffn(xs, slot_w, tile_e, tile_halves, n_used, w_gu, w_down):
    n_slots, d = xs.shape
    dff = w_down.shape[1]
    tm = MOE_TILE
    tf = _tile(dff, 512)
    nf = dff // tf

    def _e(t, te, nu):
        return te[jnp.minimum(t, nu[0] - 1)]

    def _ff(t, f, nu):
        return jnp.where(t < nu[0], f, nf - 1)

    grid_spec = pltpu.PrefetchScalarGridSpec(
        num_scalar_prefetch=3,
        grid=(n_slots // tm, nf),
        in_specs=[
            pl.BlockSpec((tm, d), lambda t, f, te, th, nu: (jnp.minimum(t, nu[0] - 1), 0),
                         pipeline_mode=pl.Buffered(1)),
            pl.BlockSpec((tm, 1), lambda t, f, te, th, nu: (t, 0)),
            pl.BlockSpec((None, d, tf), lambda t, f, te, th, nu: (_e(t, te, nu), 0, _ff(t, f, nu))),
            pl.BlockSpec((None, d, tf), lambda t, f, te, th, nu: (_e(t, te, nu), 0, _ff(t, f, nu) + nf)),
            pl.BlockSpec((None, tf, d), lambda t, f, te, th, nu: (_e(t, te, nu), _ff(t, f, nu), 0)),
        ],
        out_specs=pl.BlockSpec((tm, d), lambda t, f, te, th, nu: (t, 0), pipeline_mode=pl.Buffered(1)),
    )
    return pl.pallas_call(
        _moe_ffn_kernel,
        grid_spec=grid_spec,
        out_shape=jax.ShapeDtypeStruct((n_slots, d), F32),
        compiler_params=_params("arbitrary", "arbitrary"),
        name="moe_swiglu",
    )(tile_e, tile_halves, n_used, xs, slot_w, w_gu, w_gu, w_down)


def _ple_kernel(n_f, *refs):
    h_ref = refs[0]
    f_refs = refs[1:1 + n_f]
    gf_ref, p_ref, wp_ref, wg_ref, gp_ref, o_ref = refs[1 + n_f:]
    f = f_refs[0][...]
    for r in f_refs[1:]:
        f = f + r[...]
    h = h_ref[...] + _rms(f, gf_ref[...])
    e = jnp.dot(p_ref[...].astype(BF16), wp_ref[...], preferred_element_type=F32)
    gate = jax.nn.sigmoid(jnp.dot(h.astype(BF16), wg_ref[...], preferred_element_type=F32))
    o_ref[...] = h + _rms(gate * e, gp_ref[...])


def _ffn_post_ple(h, fs, g_post, p, w_ple_in, w_ple_gate, g_ple):
    n, d = h.shape
    pd = p.shape[1]
    tm = _tile(n, 256)
    blk = pl.BlockSpec((tm, d), lambda i: (i, 0))
    row = pl.BlockSpec((1, d), lambda i: (0, 0))
    return pl.pallas_call(
        functools.partial(_ple_kernel, len(fs)),
        grid=(n // tm,),
        in_specs=[blk] + [blk] * len(fs) + [row, pl.BlockSpec((tm, pd), lambda i: (i, 0)),
                                            pl.BlockSpec((pd, d), lambda i: (0, 0)),
                                            pl.BlockSpec((d, d), lambda i: (0, 0)), row],
        out_specs=blk,
        out_shape=jax.ShapeDtypeStruct((n, d), F32),
        compiler_params=_params("parallel"),
        name="ffn_post_ple",
    )(h, *fs, g_post.reshape(1, d), p, w_ple_in, w_ple_gate, g_ple.reshape(1, d))


def _route(logits, n_tok):
    a = n_tok * TOP_K
    top_val, top_idx = lax.top_k(logits, TOP_K)
    gate = jax.nn.softmax(top_val, axis=-1)
    flat_e = top_idx.reshape(a).astype(jnp.int32)
    flat_w = gate.reshape(a)
    onehot = (flat_e[:, None] == jnp.arange(N_EXPERTS, dtype=jnp.int32)[None, :]).astype(jnp.int32)
    csum = jnp.cumsum(onehot, axis=0)
    counts = csum[-1]
    rank = jnp.take_along_axis(csum, flat_e[:, None], axis=1)[:, 0] - 1
    padded = (counts + MOE_TILE - 1) // MOE_TILE * MOE_TILE
    pad_end = jnp.cumsum(padded)
    pad_start = pad_end - padded
    dest = pad_start[flat_e] + rank
    n_tiles = -(-a // MOE_TILE) + N_EXPERTS
    n_slots = n_tiles * MOE_TILE
    flat_tok = jnp.arange(a, dtype=jnp.int32) // TOP_K
    slot_tok = jnp.zeros((n_slots,), jnp.int32).at[dest].set(flat_tok)
    slot_w = jnp.zeros((n_slots,), F32).at[dest].set(flat_w)
    tile_start = jnp.arange(n_tiles, dtype=jnp.int32) * MOE_TILE
    tile_e = jnp.minimum(jnp.searchsorted(pad_end, tile_start, side='right'), N_EXPERTS - 1).astype(jnp.int32)
    rows = jnp.clip(pad_start[tile_e] + counts[tile_e] - tile_start, 0, MOE_TILE)
    tile_halves = ((rows + MOE_BLOCK - 1) // MOE_BLOCK).astype(jnp.int32)
    n_used = (pad_end[-1] // MOE_TILE).astype(jnp.int32).reshape(1)
    return slot_tok, slot_w, tile_e, tile_halves, n_used, dest.reshape(n_tok, TOP_K)


def _rope_slot_cols(w):
    half = QK_ROPE // 2
    z = jnp.zeros(w.shape[:-1] + (half,), w.dtype)
    return jnp.concatenate([w[..., :half], z, w[..., half:], z], axis=-1)


def _prep_layer(i, w_in, gm_ws, gm_bs, w_uq, w_ukv, w_branch_a, w_branch_b, w_out, w_ple_in, w_ple_gate):
    d = w_in.shape[1]
    wi = w_in[i]
    o = 0
    w_u = wi[:, o:o + GM_W]; o += GM_W
    w_v = wi[:, o:o + GM_W]; o += GM_W
    w_cq = wi[:, o:o + Q_LORA]; o += Q_LORA
    w_ckv = wi[:, o:o + KV_LORA]; o += KV_LORA
    w_kr = wi[:, o:o + QK_ROPE]; o += QK_ROPE
    w_g = wi[:, o:o + 2 * d]
    w_lat = jnp.concatenate([w_cq, w_ckv, _rope_slot_cols(w_kr)], axis=1)
    wq = w_uq[i].reshape(Q_LORA, MLA_HEADS, QK_NOPE + QK_ROPE)
    wq = jnp.concatenate([wq[..., :QK_NOPE], _rope_slot_cols(wq[..., QK_NOPE:])], axis=-1)
    wkv = w_ukv[i].reshape(KV_LORA, MLA_HEADS, QK_NOPE + V_HEAD)
    bs_tile = jnp.repeat(gm_bs[i].T, GM_GROUP_W, axis=1)
    return dict(
        w_u=w_u.astype(BF16), w_v=w_v.astype(BF16), w_lat=w_lat.astype(BF16), w_g=w_g.astype(BF16),
        w_qt=wq.reshape(Q_LORA, MLA_HEADS * HEAD_PAD).T.astype(BF16),
        w_uk=wkv[..., :QK_NOPE].reshape(KV_LORA, MLA_HEADS * QK_NOPE).astype(BF16),
        w_uvt=wkv[..., QK_NOPE:].reshape(KV_LORA, MLA_HEADS * V_HEAD).T.astype(BF16),
        ws=gm_ws[i].astype(BF16), bs_tile=bs_tile,
        w_a=w_branch_a[i].astype(BF16), w_b=w_branch_b[i].astype(BF16), w_o=w_out[i].astype(BF16),
        w_pi=w_ple_in[i].astype(BF16), w_pg=w_ple_gate[i].astype(BF16),
    )


def kernel(x, p, positions, norm_pre_mix, norm_post_mix, norm_pre_ffn, norm_post_ffn, w_in, gm_ln_g, gm_ln_b, gm_ws, gm_bs, mla_q_norm, w_uq, mla_kv_norm, w_ukv, w_branch_a, w_branch_b, w_out, w_dense_gu, w_dense_down, w_router, w_exp_gu, w_exp_down, w_ple_in, w_ple_gate, ple_norm):
    batch, seq, d = x.shape
    depth = p.shape[0]
    n = batch * seq
    scale = (QK_NOPE + QK_ROPE) ** -0.5 * LOG2_E
    cos_t, sin_t, cos_tt, sin_tt = _rope_tables(positions)
    h = x.reshape(n, d)

    for li in range(depth):
        wts = _prep_layer(li, w_in, gm_ws, gm_bs, w_uq, w_ukv, w_branch_a, w_branch_b, w_out,
                          w_ple_in, w_ple_gate)
        hn = _norm_bf16(h, norm_pre_mix[li])
        (u,) = _mm(hn, wts["w_u"], _epi_gelu, tm=1024, tn=GM_W, outs=[(GM_W, BF16, "tile")],
                   name="in_proj_u")
        (v,) = _mm(hn, wts["w_v"], _epi_gelu_group_ln, tm=512, tn=GM_W,
                   extras=[(gm_ln_g[li].reshape(1, GM_W), "vec"), (gm_ln_b[li].reshape(1, GM_W), "vec")],
                   outs=[(GM_W, BF16, "tile")], name="in_proj_v")
        c_q, c_kv, k_r = _mm(
            hn, wts["w_lat"], _epi_latent, tm=512, tn=wts["w_lat"].shape[1],
            extras=[(mla_q_norm[li].reshape(1, Q_LORA), "vec"), (mla_kv_norm[li].reshape(1, KV_LORA), "vec"),
                    (cos_t, "rows"), (sin_t, "rows")],
            outs=[(Q_LORA, BF16, "rows"), (KV_LORA, BF16, "rows"), (LANES, BF16, "rows")],
            name="in_proj_latent")
        (gates,) = _mm(hn, wts["w_g"], _epi_sigmoid, tm=1024, tn=1024, outs=[(2 * d, BF16, "tile")],
                       name="in_proj_gates")
        qt = _q_proj(c_q, wts["w_qt"], cos_tt, sin_tt, scale)
        k, vt = _kv_proj(c_kv, wts["w_uk"], wts["w_uvt"], k_r)
        attn = _attention(qt, k, vt, batch, seq)
        gm = _gmlp_mix(u, v, wts["ws"], wts["bs_tile"])
        merged = _branch_merge(gm, wts["w_a"], attn, wts["w_b"], gates)
        (h,) = _mm(merged, wts["w_o"], _epi_norm_residual, tm=512, tn=d,
                   extras=[(h, "rows"), (norm_post_mix[li].reshape(1, d), "vec")],
                   outs=[(d, F32, "tile")], name="out_proj")
        if li % 2 == 0:
            hn = _norm_bf16(h, norm_pre_ffn[li])
            fs = [_dense_ffn(hn, w_dense_gu[li // 2].astype(BF16), w_dense_down[li // 2].astype(BF16))]
        else:
            hn, logits = _norm_router(h, norm_pre_ffn[li], w_router[li // 2])
            slot_tok, slot_w, tile_e, tile_halves, n_used, dest = _route(logits[:, :N_EXPERTS], n)
            xs = jnp.take(hn, slot_tok, axis=0)
            ys = _moe_ffn(xs, slot_w.reshape(-1, 1), tile_e, tile_halves, n_used,
                          w_exp_gu[li // 2], w_exp_down[li // 2])
            fs = [jnp.take(ys, dest[:, kk], axis=0) for kk in range(TOP_K)]
        h = _ffn_post_ple(h, fs, norm_post_ffn[li], p[li].reshape(n, -1), wts["w_pi"], wts["w_pg"],
                          ple_norm[li])
    return h.reshape(batch, seq, d)
```

```python
import functools

import jax
import jax.numpy as jnp
from jax import lax
from jax.experimental import pallas as pl
from jax.experimental.pallas import tpu as pltpu

F32 = jnp.float32
BF16 = jnp.bfloat16

GM_GROUPS = 8
GM_GROUP_W = 128
GM_W = GM_GROUPS * GM_GROUP_W
CHUNK = 128
MLA_HEADS = 8
Q_LORA = 512
KV_LORA = 256
QK_NOPE = 128
QK_ROPE = 64
V_HEAD = 128
ROPE_THETA = 10000.0
N_EXPERTS = 8
TOP_K = 2
MOE_BLOCK = 512
MOE_TILE = 2 * MOE_BLOCK
EPS = 1e-6
LOG2_E = 1.4426950408889634

LANES = 128
HEAD_PAD = 2 * LANES
V7X_VMEM_BYTES = 64 * 1024 * 1024
VMEM_LIMIT = V7X_VMEM_BYTES - 8 * 1024 * 1024


def _params(*sem):
    return pltpu.CompilerParams(dimension_semantics=sem, vmem_limit_bytes=VMEM_LIMIT)


def _tile(n, want):
    t = min(n, want)
    while n % t:
        t //= 2
    return t


def _rms(x, g):
    return x * lax.rsqrt(jnp.mean(x * x, axis=-1, keepdims=True) + EPS) * g


def _rope_table_kernel(pos_c_ref, pos_r_ref, inv_r_ref, sgn_r_ref, inv_c_ref, sgn_c_ref,
                       cos_ref, sin_ref, cos_t_ref, sin_t_ref):
    ang = pos_c_ref[...].astype(F32) * inv_r_ref[...]
    cos_ref[...] = jnp.cos(ang) * jnp.abs(sgn_r_ref[...])
    sin_ref[...] = jnp.sin(ang) * sgn_r_ref[...]
    ang_t = inv_c_ref[...] * pos_r_ref[...].astype(F32)
    cos_t_ref[...] = jnp.cos(ang_t) * jnp.abs(sgn_c_ref[...])
    sin_t_ref[...] = jnp.sin(ang_t) * sgn_c_ref[...]


def _rope_tables(positions):
    n = positions.size
    tm = _tile(n, 1024)
    half = QK_ROPE // 2
    inv = ROPE_THETA ** (-jnp.arange(0, QK_ROPE, 2, dtype=F32) / QK_ROPE)
    z = jnp.zeros((half,), F32)
    inv_slot = jnp.concatenate([inv, z, inv, z])
    one = jnp.ones((half,), F32)
    sgn_slot = jnp.concatenate([-one, z, one, z])
    row = pl.BlockSpec((1, LANES), lambda i: (0, 0))
    col = pl.BlockSpec((LANES, 1), lambda i: (0, 0))
    tab = pl.BlockSpec((tm, LANES), lambda i: (i, 0))
    tab_t = pl.BlockSpec((LANES, tm), lambda i: (0, i))
    return pl.pallas_call(
        _rope_table_kernel,
        grid=(n // tm,),
        in_specs=[pl.BlockSpec((tm, 1), lambda i: (i, 0)), pl.BlockSpec((1, tm), lambda i: (0, i)),
                  row, row, col, col],
        out_specs=[tab, tab, tab_t, tab_t],
        out_shape=[jax.ShapeDtypeStruct((n, LANES), F32)] * 2 + [jax.ShapeDtypeStruct((LANES, n), F32)] * 2,
        compiler_params=_params("parallel"),
        name="rope_tables",
    )(positions.reshape(n, 1), positions.reshape(1, n), inv_slot.reshape(1, LANES),
      sgn_slot.reshape(1, LANES), inv_slot.reshape(LANES, 1), sgn_slot.reshape(LANES, 1))


def _rope_slot(x, cos_t, sin_t):
    return x * cos_t + pltpu.roll(x, LANES // 2, axis=1) * sin_t


def _norm_kernel(h_ref, g_ref, o_ref):
    o_ref[...] = _rms(h_ref[...], g_ref[...]).astype(o_ref.dtype)


def _norm_bf16(h, g):
    n, d = h.shape
    tm = _tile(n, 512)
    return pl.pallas_call(
        _norm_kernel,
        grid=(n // tm,),
        in_specs=[pl.BlockSpec((tm, d), lambda i: (i, 0)), pl.BlockSpec((1, d), lambda i: (0, 0))],
        out_specs=pl.BlockSpec((tm, d), lambda i: (i, 0)),
        out_shape=jax.ShapeDtypeStruct((n, d), BF16),
        compiler_params=_params("parallel"),
        name="rmsnorm",
    )(h, g.reshape(1, d))


ROUTE_E, ROUTE_RANK = 0, TOP_K


def _router_kernel(h_ref, g_ref, wr_ref, ri_ref, rw_ref, cnt_ref, carry):
    @pl.when(pl.program_id(0) == 0)
    def _():
        carry[...] = jnp.zeros_like(carry)

    y = _rms(h_ref[...], g_ref[...])
    logits = jnp.dot(y, wr_ref[...], preferred_element_type=F32, precision=lax.Precision.HIGHEST)
    tm = logits.shape[0]
    lane = lax.broadcasted_iota(jnp.int32, logits.shape, 1)
    lane_f = lane.astype(F32)
    lg = jnp.where(lane < N_EXPERTS, logits, -jnp.inf)
    tops, hots = [], []
    for _ in range(TOP_K):
        mx = jnp.max(lg, axis=1, keepdims=True)
        idx = jnp.min(jnp.where(lg == mx, lane_f, float(LANES)), axis=1, keepdims=True)
        hot = lane_f == idx
        tops.append((mx, idx))
        hots.append(hot)
        lg = jnp.where(hot, -jnp.inf, lg)
    e1 = jnp.exp(tops[1][0] - tops[0][0])
    gates = (1.0 / (1.0 + e1), e1 / (1.0 + e1))
    hot_any = hots[0].astype(F32) + hots[1].astype(F32)
    tri = (lax.broadcasted_iota(jnp.int32, (tm, tm), 0) > lax.broadcasted_iota(jnp.int32, (tm, tm), 1))
    before = jnp.dot(tri.astype(BF16), hot_any.astype(BF16), preferred_element_type=F32) + carry[...]
    ranks = [jnp.sum(jnp.where(hot, before, 0.0), axis=1, keepdims=True) for hot in hots]
    carry[...] += jnp.sum(hot_any, axis=0, keepdims=True)
    cnt_ref[...] = carry[...]
    rec = jnp.zeros(logits.shape, jnp.int32)
    wrec = jnp.zeros(logits.shape, F32)
    for k in range(TOP_K):
        rec = jnp.where(lane == ROUTE_E + k, tops[k][1].astype(jnp.int32), rec)
        rec = jnp.where(lane == ROUTE_RANK + k, ranks[k].astype(jnp.int32), rec)
        wrec = jnp.where(lane == k, gates[k], wrec)
    ri_ref[...] = rec
    rw_ref[...] = wrec


def _router(h, g, w_router):
    n, d = h.shape
    tm = _tile(n, 512)
    wr = jnp.zeros((d, LANES), F32).at[:, :N_EXPERTS].set(w_router)
    rec = pl.BlockSpec((tm, LANES), lambda i: (i, 0))
    return pl.pallas_call(
        _router_kernel,
        grid=(n // tm,),
        in_specs=[pl.BlockSpec((tm, d), lambda i: (i, 0)), pl.BlockSpec((1, d), lambda i: (0, 0)),
                  pl.BlockSpec((d, LANES), lambda i: (0, 0))],
        out_specs=[rec, rec, pl.BlockSpec((1, LANES), lambda i: (0, 0))],
        out_shape=[jax.ShapeDtypeStruct((n, LANES), jnp.int32), jax.ShapeDtypeStruct((n, LANES), F32),
                   jax.ShapeDtypeStruct((1, LANES), F32)],
        scratch_shapes=[pltpu.VMEM((1, LANES), F32)],
        compiler_params=_params("arbitrary"),
        name="moe_router",
    )(h, g.reshape(1, d), wr)


def _pack_bf16_pairs(y):
    half = y.shape[1] // 2
    lo = lax.bitcast_convert_type(y[:, :half].astype(BF16).astype(F32), jnp.uint32)
    hi = lax.bitcast_convert_type(y[:, half:].astype(BF16).astype(F32), jnp.uint32)
    return hi | (lo >> 16)


def _unpack_bf16_pairs(w):
    lo = lax.bitcast_convert_type(w << 16, F32).astype(BF16)
    hi = lax.bitcast_convert_type(w & jnp.uint32(0xFFFF0000), F32).astype(BF16)
    return jnp.concatenate([lo, hi], axis=1)


def _row_copy_wait(src_row, dst_row, sem, count):
    def body(_, c):
        pltpu.make_async_copy(src_row, dst_row, sem).wait()
        return c
    lax.fori_loop(0, count, body, 0, unroll=16)


def _dispatch_kernel(dest_ref, h_ref, g_ref, xs_zero_ref, xs_ref, pbuf, sem):
    del xs_zero_ref
    i = pl.program_id(0)
    last = pl.num_programs(0) - 1
    slot = i % 2
    tm = h_ref.shape[0]

    def wait_slot(s):
        _row_copy_wait(pbuf.at[s, pl.ds(0, 1)], xs_ref.at[pl.ds(0, 1)], sem.at[s], TOP_K * tm)

    @pl.when(i >= 2)
    def _():
        wait_slot(slot)

    pbuf[slot] = _pack_bf16_pairs(_rms(h_ref[...], g_ref[...]))

    def send(r, c):
        for k in range(TOP_K):
            dst = dest_ref[0, 0, TOP_K * r + k]
            pltpu.make_async_copy(pbuf.at[slot, pl.ds(r, 1)], xs_ref.at[pl.ds(dst, 1)], sem.at[slot]).start()
        return c
    lax.fori_loop(0, tm, send, 0, unroll=8)

    @pl.when(i == last)
    def _():
        @pl.when(i >= 1)
        def _():
            wait_slot(1 - slot)
        wait_slot(slot)


def _dispatch(h, g, dest, n_slots):
    n, d = h.shape
    tm = _tile(n, 512)
    nt = n // tm
    xs0 = jnp.zeros((n_slots, d // 2), jnp.uint32)
    return pl.pallas_call(
        _dispatch_kernel,
        grid=(nt,),
        in_specs=[pl.BlockSpec((1, 1, TOP_K * tm), lambda i: (i, 0, 0), memory_space=pltpu.SMEM),
                  pl.BlockSpec((tm, d), lambda i: (i, 0)), pl.BlockSpec((1, d), lambda i: (0, 0)),
                  pl.BlockSpec(memory_space=pl.ANY)],
        out_specs=pl.BlockSpec(memory_space=pl.ANY),
        out_shape=jax.ShapeDtypeStruct((n_slots, d // 2), jnp.uint32),
        scratch_shapes=[pltpu.VMEM((2, tm, d // 2), jnp.uint32), pltpu.SemaphoreType.DMA((2,))],
        input_output_aliases={3: 0},
        compiler_params=_params("arbitrary"),
        name="moe_dispatch",
    )(dest.reshape(nt, 1, TOP_K * tm), h, g.reshape(1, d), xs0)


def _mm_kernel(epi, n_extra, x_ref, w_ref, *rest):
    extras = rest[:n_extra]
    outs = rest[n_extra:]
    acc = jnp.dot(x_ref[...], w_ref[...], preferred_element_type=F32)
    res = epi(acc, *extras)
    if not isinstance(res, (tuple, list)):
        res = (res,)
    for o, r in zip(outs, res):
        o[...] = r.astype(o.dtype)


def _mm(x, w, epi, *, tm, tn, extras=(), outs, name):
    m, k = x.shape
    n = w.shape[1]
    tm = _tile(m, tm)
    tn = _tile(n, tn)

    def spec(ncols, kind):
        if kind == "tile":
            return pl.BlockSpec((tm, tn), lambda j, i: (i, j))
        if kind == "rows":
            return pl.BlockSpec((tm, ncols), lambda j, i: (i, 0))
        assert kind == "vec", kind
        return pl.BlockSpec((1, ncols), lambda j, i: (0, 0))

    in_specs = [pl.BlockSpec((tm, k), lambda j, i: (i, 0)), pl.BlockSpec((k, tn), lambda j, i: (0, j))]
    in_specs += [spec(a.shape[1], kind) for a, kind in extras]
    return pl.pallas_call(
        functools.partial(_mm_kernel, epi, len(extras)),
        grid=(n // tn, m // tm),
        in_specs=in_specs,
        out_specs=[spec(nc, kind) for nc, _, kind in outs],
        out_shape=[jax.ShapeDtypeStruct((m, nc), dt) for nc, dt, _ in outs],
        compiler_params=_params("parallel", "parallel"),
        name=name,
    )(x, w, *[a for a, _ in extras])


def _epi_gelu(acc):
    return jax.nn.gelu(acc)


def _epi_gelu_group_ln(acc, g_ref, b_ref):
    y = jax.nn.gelu(acc)
    g = g_ref[...]
    b = b_ref[...]
    cols = []
    for gi in range(acc.shape[1] // GM_GROUP_W):
        sl = slice(gi * GM_GROUP_W, (gi + 1) * GM_GROUP_W)
        yg = y[:, sl]
        mu = jnp.mean(yg, axis=-1, keepdims=True)
        dlt = yg - mu
        var = jnp.mean(dlt * dlt, axis=-1, keepdims=True)
        cols.append(dlt * lax.rsqrt(var + EPS) * g[:, sl] + b[:, sl])
    return jnp.concatenate(cols, axis=1)


def _epi_sigmoid(acc):
    return jax.nn.sigmoid(acc)


def _epi_latent(acc, qn_ref, kvn_ref, cos_ref, sin_ref):
    c_q = _rms(acc[:, :Q_LORA], qn_ref[...])
    c_kv = _rms(acc[:, Q_LORA:Q_LORA + KV_LORA], kvn_ref[...])
    k_r = _rope_slot(acc[:, Q_LORA + KV_LORA:], cos_ref[...], sin_ref[...])
    return c_q, c_kv, k_r


def _q_proj_kernel(scale, c_ref, w_ref, cos_ref, sin_ref, o_ref):
    acc = lax.dot_general(w_ref[...], c_ref[...], (((1,), (1,)), ((), ())),
                          preferred_element_type=F32)
    cos_t = cos_ref[...] * scale
    sin_t = sin_ref[...] * scale
    half = LANES // 2
    for hd in range(MLA_HEADS):
        base = hd * HEAD_PAD
        o_ref[base:base + QK_NOPE, :] = (acc[base:base + QK_NOPE, :] * scale).astype(o_ref.dtype)
        r = acc[base + QK_NOPE:base + HEAD_PAD, :]
        rot = jnp.concatenate([r[half:, :], r[:half, :]], axis=0)
        o_ref[base + QK_NOPE:base + HEAD_PAD, :] = (r * cos_t + rot * sin_t).astype(o_ref.dtype)


def _q_proj(c_q, w_qt, cos_tt, sin_tt, scale):
    n, r = c_q.shape
    qw = w_qt.shape[0]
    tm = _tile(n, 512)
    tab = pl.BlockSpec((LANES, tm), lambda i: (0, i))
    return pl.pallas_call(
        functools.partial(_q_proj_kernel, scale),
        grid=(n // tm,),
        in_specs=[pl.BlockSpec((tm, r), lambda i: (i, 0)), pl.BlockSpec((qw, r), lambda i: (0, 0)), tab, tab],
        out_specs=pl.BlockSpec((qw, tm), lambda i: (0, i)),
        out_shape=jax.ShapeDtypeStruct((qw, n), BF16),
        compiler_params=_params("parallel"),
        name="mla_q_proj",
    )(c_q, w_qt, cos_tt, sin_tt)


def _kv_kernel(c_ref, wk_ref, wvt_ref, kr_ref, k_ref, vt_ref):
    c = c_ref[...]
    kn = jnp.dot(c, wk_ref[...], preferred_element_type=F32).astype(k_ref.dtype)
    kr = kr_ref[...]
    for hd in range(MLA_HEADS):
        k_ref[:, hd * HEAD_PAD:hd * HEAD_PAD + QK_NOPE] = kn[:, hd * QK_NOPE:(hd + 1) * QK_NOPE]
        k_ref[:, hd * HEAD_PAD + QK_NOPE:(hd + 1) * HEAD_PAD] = kr
    vt_ref[...] = lax.dot_general(wvt_ref[...], c, (((1,), (1,)), ((), ())),
                                  preferred_element_type=F32).astype(vt_ref.dtype)


def _kv_proj(c_kv, w_uk, w_uvt, k_rope):
    n, r = c_kv.shape
    tm = _tile(n, 512)
    kw = MLA_HEADS * HEAD_PAD
    vw = MLA_HEADS * V_HEAD
    return pl.pallas_call(
        _kv_kernel,
        grid=(n // tm,),
        in_specs=[pl.BlockSpec((tm, r), lambda i: (i, 0)),
                  pl.BlockSpec(w_uk.shape, lambda i: (0, 0)),
                  pl.BlockSpec(w_uvt.shape, lambda i: (0, 0)),
                  pl.BlockSpec((tm, LANES), lambda i: (i, 0))],
        out_specs=[pl.BlockSpec((tm, kw), lambda i: (i, 0)), pl.BlockSpec((vw, tm), lambda i: (0, i))],
        out_shape=[jax.ShapeDtypeStruct((n, kw), BF16), jax.ShapeDtypeStruct((vw, n), BF16)],
        compiler_params=_params("parallel"),
        name="mla_kv_proj",
    )(c_kv, w_uk, w_uvt, k_rope)


def _attn_kernel(tk, qt_ref, k_ref, vt_ref, o_ref, s_a, s_b, p_a, p_b):
    tq = qt_ref.shape[1]
    n_chunks = k_ref.shape[0] // tk
    qt = qt_ref[...]

    def scores(i, s_out):
        off = pl.multiple_of(i * tk, tk)
        s = jnp.dot(k_ref[pl.ds(off, tk), :], qt, preferred_element_type=F32)
        s_out[...] = s
        return jnp.max(s, axis=0, keepdims=True)

    def softmax(s_in, p_out, mx, m, l):
        m_new = jnp.maximum(m, mx)
        alpha = jnp.exp2(m - m_new)
        p = jnp.exp2(s_in[...] - m_new)
        p_out[...] = p.astype(p_out.dtype)
        return m_new, alpha * l + jnp.sum(p, axis=0, keepdims=True), alpha

    def values(i, p_in, alpha, acc):
        off = pl.multiple_of(i * tk, tk)
        return alpha * acc + jnp.dot(vt_ref[:, pl.ds(off, tk)], p_in[...], preferred_element_type=F32)

    def step(i, odd, carry, first=False, last=False):
        s_cur, s_nxt, p_cur, p_prv = (s_b, s_a, p_b, p_a) if odd else (s_a, s_b, p_a, p_b)
        mx, m, l, alpha, acc = carry
        if not first:
            acc = values(i - 1, p_prv, alpha, acc)
        mx_nxt = mx if last else scores(i + 1, s_nxt)
        m, l, alpha = softmax(s_cur, p_cur, mx, m, l)
        return mx_nxt, m, l, alpha, acc

    carry = (scores(0, s_a), jnp.full((1, tq), -jnp.inf, F32), jnp.zeros((1, tq), F32),
             jnp.ones((1, tq), F32), jnp.zeros((vt_ref.shape[0], tq), F32))
    carry = step(0, False, carry, first=True)

    def pair(j, c):
        c = step(2 * j + 1, True, c)
        return step(2 * j + 2, False, c)

    carry = lax.fori_loop(0, (n_chunks - 2) // 2, pair, carry)
    _, _, l, alpha, acc = step(n_chunks - 1, True, carry, last=True)
    acc = values(n_chunks - 1, p_b, alpha, acc)
    o_ref[...] = jnp.transpose(acc / l).astype(o_ref.dtype)


def _attention(qt, k, vt, batch, seq):
    n = k.shape[0]
    tq = _tile(seq, 512)
    tk = _tile(seq // 2, 512)
    assert (seq // tk) % 2 == 0
    nq = seq // tq
    return pl.pallas_call(
        functools.partial(_attn_kernel, tk),
        grid=(batch, MLA_HEADS, nq),
        scratch_shapes=[pltpu.VMEM((tk, tq), F32), pltpu.VMEM((tk, tq), F32),
                        pltpu.VMEM((tk, tq), BF16), pltpu.VMEM((tk, tq), BF16)],
        in_specs=[pl.BlockSpec((HEAD_PAD, tq), lambda b, h, i: (h, b * nq + i)),
                  pl.BlockSpec((seq, HEAD_PAD), lambda b, h, i: (b, h)),
                  pl.BlockSpec((V_HEAD, seq), lambda b, h, i: (h, b))],
        out_specs=pl.BlockSpec((tq, V_HEAD), lambda b, h, i: (b * nq + i, h)),
        out_shape=jax.ShapeDtypeStruct((n, MLA_HEADS * V_HEAD), BF16),
        compiler_params=_params("parallel", "parallel", "arbitrary"),
        name="mla_attention",
    )(qt, k, vt)


def _gmlp_kernel(u_ref, v_ref, ws_ref, bs_ref, o_ref):
    n_chunks = u_ref.shape[0] // CHUNK
    for c in range(n_chunks):
        rows = slice(c * CHUNK, (c + 1) * CHUNK)
        for g in range(GM_GROUPS):
            cols = slice(g * GM_GROUP_W, (g + 1) * GM_GROUP_W)
            mixed = jnp.dot(ws_ref[g], v_ref[rows, cols], preferred_element_type=F32)
            o_ref[rows, cols] = (u_ref[rows, cols].astype(F32) * (mixed + bs_ref[:, cols])).astype(o_ref.dtype)


def _gmlp_mix(u, v, ws, bs_tile):
    n, w = u.shape
    tm = _tile(n, 4 * CHUNK)
    blk = pl.BlockSpec((tm, w), lambda i: (i, 0))
    return pl.pallas_call(
        _gmlp_kernel,
        grid=(n // tm,),
        in_specs=[blk, blk, pl.BlockSpec(ws.shape, lambda i: (0, 0, 0)),
                  pl.BlockSpec(bs_tile.shape, lambda i: (0, 0))],
        out_specs=blk,
        out_shape=jax.ShapeDtypeStruct((n, w), BF16),
        compiler_params=_params("parallel"),
        name="gmlp_spatial",
    )(u, v, ws, bs_tile)


def _merge_kernel(a_ref, wa_ref, b_ref, wb_ref, ga_ref, gb_ref, o_ref):
    ya = jnp.dot(a_ref[...], wa_ref[...], preferred_element_type=F32)
    yb = jnp.dot(b_ref[...], wb_ref[...], preferred_element_type=F32)
    o_ref[...] = (ga_ref[...].astype(F32) * ya + gb_ref[...].astype(F32) * yb).astype(o_ref.dtype)


def _branch_merge(a, wa, b, wb, gates):
    n, ka = a.shape
    kb = b.shape[1]
    d = wa.shape[1]
    tm = _tile(n, 1024)
    tn = _tile(d, 1024)
    nj = d // tn
    return pl.pallas_call(
        _merge_kernel,
        grid=(nj, n // tm),
        in_specs=[pl.BlockSpec((tm, ka), lambda j, i: (i, 0)), pl.BlockSpec((ka, tn), lambda j, i: (0, j)),
                  pl.BlockSpec((tm, kb), lambda j, i: (i, 0)), pl.BlockSpec((kb, tn), lambda j, i: (0, j)),
                  pl.BlockSpec((tm, tn), lambda j, i: (i, j)),
                  pl.BlockSpec((tm, tn), lambda j, i: (i, j + nj))],
        out_specs=pl.BlockSpec((tm, tn), lambda j, i: (i, j)),
        out_shape=jax.ShapeDtypeStruct((n, d), BF16),
        compiler_params=_params("parallel", "parallel"),
        name="branch_merge",
    )(a, wa, b, wb, gates, gates)


def _epi_norm_residual(acc, h_ref, g_ref):
    return h_ref[...] + _rms(acc, g_ref[...])


def _swiglu_step(x, wg_ref, wu_ref, wd_ref):
    g = jnp.dot(x, wg_ref[...], preferred_element_type=F32)
    u = jnp.dot(x, wu_ref[...], preferred_element_type=F32)
    a = (jax.nn.silu(g) * u).astype(x.dtype)
    return jnp.dot(a, wd_ref[...], preferred_element_type=F32)


def _dense_ffn_kernel(x_ref, wg_ref, wu_ref, wd_ref, o_ref):
    f = pl.program_id(1)
    @pl.when(f == 0)
    def _():
        o_ref[...] = jnp.zeros_like(o_ref)

    o_ref[...] += _swiglu_step(x_ref[...], wg_ref, wu_ref, wd_ref)


def _dense_ffn(x, w_gu, w_down):
    n, d = x.shape
    dff = w_down.shape[0]
    tm = _tile(n, 1024)
    tf = _tile(dff, 512)
    nf = dff // tf
    return pl.pallas_call(
        _dense_ffn_kernel,
        grid=(n // tm, nf),
        in_specs=[pl.BlockSpec((tm, d), lambda i, f: (i, 0)),
                  pl.BlockSpec((d, tf), lambda i, f: (0, f)),
                  pl.BlockSpec((d, tf), lambda i, f: (0, f + nf)),
                  pl.BlockSpec((tf, d), lambda i, f: (f, 0))],
        out_specs=pl.BlockSpec((tm, d), lambda i, f: (i, 0)),
        out_shape=jax.ShapeDtypeStruct((n, d), F32),
        compiler_params=_params("parallel", "arbitrary"),
        name="dense_swiglu",
    )(x, w_gu, w_gu, w_down)


def _moe_ffn_kernel(te_ref, th_ref, nu_ref, x_ref, wg_ref, wu_ref, wd_ref, o_ref):
    t = pl.program_id(0)
    f = pl.program_id(1)
    n_half = th_ref[t]

    @pl.when(f == 0)
    def _():
        o_ref[...] = jnp.zeros_like(o_ref)

    @pl.when(n_half > 0)
    def _():
        wg = wg_ref[...].astype(BF16)
        wu = wu_ref[...].astype(BF16)
        wd = wd_ref[...].astype(BF16)

        def half(rows):
            x = _unpack_bf16_pairs(x_ref[rows, :])
            g = jnp.dot(x, wg, preferred_element_type=F32)
            u = jnp.dot(x, wu, preferred_element_type=F32)
            y = jnp.dot((jax.nn.silu(g) * u).astype(BF16), wd, preferred_element_type=F32)
            o_ref[rows, :] += y

        half(slice(0, MOE_BLOCK))

        @pl.when(n_half > 1)
        def _():
            half(slice(MOE_BLOCK, MOE_TILE))


def _moe_ffn(xs, tile_e, tile_halves, n_used, w_gu, w_down):
    n_slots = xs.shape[0]
    d = w_down.shape[2]
    dff = w_down.shape[1]
    tm = MOE_TILE
    tf = _tile(dff, 512)
    nf = dff // tf

    def _e(t, te, nu):
        return te[jnp.minimum(t, nu[0] - 1)]

    def _ff(t, f, nu):
        return jnp.where(t < nu[0], f, nf - 1)

    grid_spec = pltpu.PrefetchScalarGridSpec(
        num_scalar_prefetch=3,
        grid=(n_slots // tm, nf),
        in_specs=[
            pl.BlockSpec((tm, d // 2), lambda t, f, te, th, nu: (jnp.minimum(t, nu[0] - 1), 0),
                         pipeline_mode=pl.Buffered(1)),
            pl.BlockSpec((None, d, tf), lambda t, f, te, th, nu: (_e(t, te, nu), 0, _ff(t, f, nu))),
            pl.BlockSpec((None, d, tf), lambda t, f, te, th, nu: (_e(t, te, nu), 0, _ff(t, f, nu) + nf)),
            pl.BlockSpec((None, tf, d), lambda t, f, te, th, nu: (_e(t, te, nu), _ff(t, f, nu), 0)),
        ],
        out_specs=pl.BlockSpec((tm, d), lambda t, f, te, th, nu: (t, 0), pipeline_mode=pl.Buffered(1)),
    )
    return pl.pallas_call(
        _moe_ffn_kernel,
        grid_spec=grid_spec,
        out_shape=jax.ShapeDtypeStruct((n_slots, d), F32),
        compiler_params=_params("arbitrary", "arbitrary"),
        name="moe_swiglu",
    )(tile_e, tile_halves, n_used, xs, w_gu, w_gu, w_down)


def _post_ple_math(h, f, gf_ref, p_ref, wp_ref, wg_ref, gp_ref, o_ref):
    h = h + _rms(f, gf_ref[...])
    e = jnp.dot(p_ref[...].astype(BF16), wp_ref[...], preferred_element_type=F32)
    gate = jax.nn.sigmoid(jnp.dot(h.astype(BF16), wg_ref[...], preferred_element_type=F32))
    o_ref[...] = h + _rms(gate * e, gp_ref[...])


def _ple_kernel(h_ref, f_ref, gf_ref, p_ref, wp_ref, wg_ref, gp_ref, o_ref):
    _post_ple_math(h_ref[...], f_ref[...], gf_ref, p_ref, wp_ref, wg_ref, gp_ref, o_ref)


def _ple_specs(n, d, pd, tm):
    blk = pl.BlockSpec((tm, d), lambda i: (i, 0))
    row = pl.BlockSpec((1, d), lambda i: (0, 0))
    tail = [row, pl.BlockSpec((tm, pd), lambda i: (i, 0)), pl.BlockSpec((pd, d), lambda i: (0, 0)),
            pl.BlockSpec((d, d), lambda i: (0, 0)), row]
    return blk, tail


def _ffn_post_ple(h, f, g_post, p, w_ple_in, w_ple_gate, g_ple):
    n, d = h.shape
    tm = _tile(n, 256)
    blk, tail = _ple_specs(n, d, p.shape[1], tm)
    return pl.pallas_call(
        _ple_kernel,
        grid=(n // tm,),
        in_specs=[blk, blk] + tail,
        out_specs=blk,
        out_shape=jax.ShapeDtypeStruct((n, d), F32),
        compiler_params=_params("parallel"),
        name="ffn_post_ple",
    )(h, f, g_post.reshape(1, d), p, w_ple_in, w_ple_gate, g_ple.reshape(1, d))


def _ple_combine_kernel(d_cur_ref, d_nxt_ref, h_ref, rw_ref, gf_ref, p_ref, wp_ref, wg_ref, gp_ref,
                        ys_ref, o_ref, gbuf, sem):
    i = pl.program_id(0)
    last = pl.num_programs(0) - 1
    slot = i % 2
    tm = h_ref.shape[0]

    def fetch(d_ref, s):
        def body(r, c):
            for k in range(TOP_K):
                src = d_ref[0, 0, TOP_K * r + k]
                pltpu.make_async_copy(ys_ref.at[pl.ds(src, 1)], gbuf.at[s, k, pl.ds(r, 1)], sem.at[s]).start()
            return c
        lax.fori_loop(0, tm, body, 0, unroll=8)

    @pl.when(i == 0)
    def _():
        fetch(d_cur_ref, 0)

    @pl.when(i < last)
    def _():
        fetch(d_nxt_ref, 1 - slot)

    _row_copy_wait(ys_ref.at[pl.ds(0, 1)], gbuf.at[slot, 0, pl.ds(0, 1)], sem.at[slot], TOP_K * tm)
    f = rw_ref[:, 0:1] * gbuf[slot, 0]
    for k in range(1, TOP_K):
        f = f + rw_ref[:, k:k + 1] * gbuf[slot, k]
    _post_ple_math(h_ref[...], f, gf_ref, p_ref, wp_ref, wg_ref, gp_ref, o_ref)


def _moe_combine_post_ple(h, ys, dest, route_w, g_post, p, w_ple_in, w_ple_gate, g_ple):
    n, d = h.shape
    tm = _tile(n, 256)
    nt = n // tm
    blk, tail = _ple_specs(n, d, p.shape[1], tm)
    dest3 = dest.reshape(nt, 1, TOP_K * tm)
    idx = functools.partial(pl.BlockSpec, (1, 1, TOP_K * tm), memory_space=pltpu.SMEM)
    return pl.pallas_call(
        _ple_combine_kernel,
        grid=(nt,),
        in_specs=[idx(lambda i: (i, 0, 0)), idx(lambda i: (jnp.minimum(i + 1, nt - 1), 0, 0)),
                  blk, pl.BlockSpec((tm, LANES), lambda i: (i, 0))] + tail
                 + [pl.BlockSpec(memory_space=pl.ANY)],
        out_specs=blk,
        out_shape=jax.ShapeDtypeStruct((n, d), F32),
        scratch_shapes=[pltpu.VMEM((2, TOP_K, tm, d), F32), pltpu.SemaphoreType.DMA((2,))],
        compiler_params=_params("arbitrary"),
        name="moe_combine_post_ple",
    )(dest3, dest3, h, route_w, g_post.reshape(1, d), p, w_ple_in, w_ple_gate, g_ple.reshape(1, d), ys)


def _slot_tables(route_i, counts_f, n_tok):
    a = n_tok * TOP_K
    counts = counts_f[0, :N_EXPERTS].astype(jnp.int32)
    padded = (counts + MOE_TILE - 1) // MOE_TILE * MOE_TILE
    pad_end = jnp.cumsum(padded)
    pad_start = pad_end - padded
    experts = route_i[:, ROUTE_E:ROUTE_E + TOP_K]
    dest = pad_start[experts] + route_i[:, ROUTE_RANK:ROUTE_RANK + TOP_K]
    n_tiles = -(-a // MOE_TILE) + N_EXPERTS
    tile_start = jnp.arange(n_tiles, dtype=jnp.int32) * MOE_TILE
    tile_e = jnp.minimum(jnp.searchsorted(pad_end, tile_start, side='right'), N_EXPERTS - 1).astype(jnp.int32)
    rows = jnp.clip(pad_start[tile_e] + counts[tile_e] - tile_start, 0, MOE_TILE)
    tile_halves = ((rows + MOE_BLOCK - 1) // MOE_BLOCK).astype(jnp.int32)
    n_used = (pad_end[-1] // MOE_TILE).astype(jnp.int32).reshape(1)
    return dest.astype(jnp.int32), tile_e, tile_halves, n_used, n_tiles * MOE_TILE


def _rope_slot_cols(w):
    half = QK_ROPE // 2
    z = jnp.zeros(w.shape[:-1] + (half,), w.dtype)
    return jnp.concatenate([w[..., :half], z, w[..., half:], z], axis=-1)


def _prep_layer(i, w_in, gm_ws, gm_bs, w_uq, w_ukv, w_branch_a, w_branch_b, w_out, w_ple_in, w_ple_gate):
    d = w_in.shape[1]
    wi = w_in[i]
    o = 0
    w_u = wi[:, o:o + GM_W]; o += GM_W
    w_v = wi[:, o:o + GM_W]; o += GM_W
    w_cq = wi[:, o:o + Q_LORA]; o += Q_LORA
    w_ckv = wi[:, o:o + KV_LORA]; o += KV_LORA
    w_kr = wi[:, o:o + QK_ROPE]; o += QK_ROPE
    w_g = wi[:, o:o + 2 * d]
    w_lat = jnp.concatenate([w_cq, w_ckv, _rope_slot_cols(w_kr)], axis=1)
    wq = w_uq[i].reshape(Q_LORA, MLA_HEADS, QK_NOPE + QK_ROPE)
    wq = jnp.concatenate([wq[..., :QK_NOPE], _rope_slot_cols(wq[..., QK_NOPE:])], axis=-1)
    wkv = w_ukv[i].reshape(KV_LORA, MLA_HEADS, QK_NOPE + V_HEAD)
    bs_tile = jnp.repeat(gm_bs[i].T, GM_GROUP_W, axis=1)
    return dict(
        w_u=w_u.astype(BF16), w_v=w_v.astype(BF16), w_lat=w_lat.astype(BF16), w_g=w_g.astype(BF16),
        w_qt=wq.reshape(Q_LORA, MLA_HEADS * HEAD_PAD).T.astype(BF16),
        w_uk=wkv[..., :QK_NOPE].reshape(KV_LORA, MLA_HEADS * QK_NOPE).astype(BF16),
        w_uvt=wkv[..., QK_NOPE:].reshape(KV_LORA, MLA_HEADS * V_HEAD).T.astype(BF16),
        ws=gm_ws[i].astype(BF16), bs_tile=bs_tile,
        w_a=w_branch_a[i].astype(BF16), w_b=w_branch_b[i].astype(BF16), w_o=w_out[i].astype(BF16),
        w_pi=w_ple_in[i].astype(BF16), w_pg=w_ple_gate[i].astype(BF16),
    )


def kernel(x, p, positions, norm_pre_mix, norm_post_mix, norm_pre_ffn, norm_post_ffn, w_in, gm_ln_g, gm_ln_b, gm_ws, gm_bs, mla_q_norm, w_uq, mla_kv_norm, w_ukv, w_branch_a, w_branch_b, w_out, w_dense_gu, w_dense_down, w_router, w_exp_gu, w_exp_down, w_ple_in, w_ple_gate, ple_norm):
    batch, seq, d = x.shape
    depth = p.shape[0]
    n = batch * seq
    scale = (QK_NOPE + QK_ROPE) ** -0.5 * LOG2_E
    cos_t, sin_t, cos_tt, sin_tt = _rope_tables(positions)
    h = x.reshape(n, d)

    for li in range(depth):
        wts = _prep_layer(li, w_in, gm_ws, gm_bs, w_uq, w_ukv, w_branch_a, w_branch_b, w_out,
                          w_ple_in, w_ple_gate)
        hn = _norm_bf16(h, norm_pre_mix[li])
        (u,) = _mm(hn, wts["w_u"], _epi_gelu, tm=1024, tn=GM_W, outs=[(GM_W, BF16, "tile")],
                   name="in_proj_u")
        (v,) = _mm(hn, wts["w_v"], _epi_gelu_group_ln, tm=512, tn=GM_W,
                   extras=[(gm_ln_g[li].reshape(1, GM_W), "vec"), (gm_ln_b[li].reshape(1, GM_W), "vec")],
                   outs=[(GM_W, BF16, "tile")], name="in_proj_v")
        c_q, c_kv, k_r = _mm(
            hn, wts["w_lat"], _epi_latent, tm=512, tn=wts["w_lat"].shape[1],
            extras=[(mla_q_norm[li].reshape(1, Q_LORA), "vec"), (mla_kv_norm[li].reshape(1, KV_LORA), "vec"),
                    (cos_t, "rows"), (sin_t, "rows")],
            outs=[(Q_LORA, BF16, "rows"), (KV_LORA, BF16, "rows"), (LANES, BF16, "rows")],
            name="in_proj_latent")
        (gates,) = _mm(hn, wts["w_g"], _epi_sigmoid, tm=1024, tn=1024, outs=[(2 * d, BF16, "tile")],
                       name="in_proj_gates")
        qt = _q_proj(c_q, wts["w_qt"], cos_tt, sin_tt, scale)
        k, vt = _kv_proj(c_kv, wts["w_uk"], wts["w_uvt"], k_r)
        attn = _attention(qt, k, vt, batch, seq)
        gm = _gmlp_mix(u, v, wts["ws"], wts["bs_tile"])
        merged = _branch_merge(gm, wts["w_a"], attn, wts["w_b"], gates)
        (h,) = _mm(merged, wts["w_o"], _epi_norm_residual, tm=512, tn=d,
                   extras=[(h, "rows"), (norm_post_mix[li].reshape(1, d), "vec")],
                   outs=[(d, F32, "tile")], name="out_proj")
        p_li = p[li].reshape(n, -1)
        if li % 2 == 0:
            hn = _norm_bf16(h, norm_pre_ffn[li])
            f = _dense_ffn(hn, w_dense_gu[li // 2].astype(BF16), w_dense_down[li // 2].astype(BF16))
            h = _ffn_post_ple(h, f, norm_post_ffn[li], p_li, wts["w_pi"], wts["w_pg"], ple_norm[li])
        else:
            route_i, route_w, counts = _router(h, norm_pre_ffn[li], w_router[li // 2])
            dest, tile_e, tile_halves, n_used, n_slots = _slot_tables(route_i, counts, n)
            xs = _dispatch(h, norm_pre_ffn[li], dest, n_slots)
            ys = _moe_ffn(xs, tile_e, tile_halves, n_used, w_exp_gu[li // 2], w_exp_down[li // 2])
            h = _moe_combine_post_ple(h, ys, dest, route_w, norm_post_ffn[li], p_li, wts["w_pi"],
                                      wts["w_pg"], ple_norm[li])
    return h.reshape(batch, seq, d)
```

```python
import functools

import jax
import jax.numpy as jnp
from jax import lax
from jax.experimental import pallas as pl
from jax.experimental.pallas import tpu as pltpu

F32 = jnp.float32
BF16 = jnp.bfloat16

GM_GROUPS = 8
GM_GROUP_W = 128
GM_W = GM_GROUPS * GM_GROUP_W
CHUNK = 128
MLA_HEADS = 8
Q_LORA = 512
KV_LORA = 256
QK_NOPE = 128
QK_ROPE = 64
V_HEAD = 128
ROPE_THETA = 10000.0
N_EXPERTS = 8
TOP_K = 2
MOE_BLOCK = 512
MOE_TILE = 2 * MOE_BLOCK
EPS = 1e-6
LOG2_E = 1.4426950408889634

LANES = 128
HEAD_PAD = 2 * LANES
BF16_SUBLANES = 16
V_ROWS = V_HEAD + BF16_SUBLANES
V7X_VMEM_BYTES = 64 * 1024 * 1024
VMEM_LIMIT = V7X_VMEM_BYTES - 8 * 1024 * 1024


def _params(*sem):
    return pltpu.CompilerParams(dimension_semantics=sem, vmem_limit_bytes=VMEM_LIMIT)


def _tile(n, want):
    t = min(n, want)
    while n % t:
        t //= 2
    return t


def _rms(x, g):
    return x * lax.rsqrt(jnp.mean(x * x, axis=-1, keepdims=True) + EPS) * g


def _rope_table_kernel(pos_c_ref, pos_r_ref, inv_r_ref, sgn_r_ref, inv_c_ref, sgn_c_ref,
                       cos_ref, sin_ref, cos_t_ref, sin_t_ref):
    ang = pos_c_ref[...].astype(F32) * inv_r_ref[...]
    cos_ref[...] = jnp.cos(ang) * jnp.abs(sgn_r_ref[...])
    sin_ref[...] = jnp.sin(ang) * sgn_r_ref[...]
    ang_t = inv_c_ref[...] * pos_r_ref[...].astype(F32)
    cos_t_ref[...] = jnp.cos(ang_t) * jnp.abs(sgn_c_ref[...])
    sin_t_ref[...] = jnp.sin(ang_t) * sgn_c_ref[...]


def _rope_tables(positions):
    n = positions.size
    tm = _tile(n, 1024)
    half = QK_ROPE // 2
    inv = ROPE_THETA ** (-jnp.arange(0, QK_ROPE, 2, dtype=F32) / QK_ROPE)
    z = jnp.zeros((half,), F32)
    inv_slot = jnp.concatenate([inv, z, inv, z])
    one = jnp.ones((half,), F32)
    sgn_slot = jnp.concatenate([-one, z, one, z])
    row = pl.BlockSpec((1, LANES), lambda i: (0, 0))
    col = pl.BlockSpec((LANES, 1), lambda i: (0, 0))
    tab = pl.BlockSpec((tm, LANES), lambda i: (i, 0))
    tab_t = pl.BlockSpec((LANES, tm), lambda i: (0, i))
    return pl.pallas_call(
        _rope_table_kernel,
        grid=(n // tm,),
        in_specs=[pl.BlockSpec((tm, 1), lambda i: (i, 0)), pl.BlockSpec((1, tm), lambda i: (0, i)),
                  row, row, col, col],
        out_specs=[tab, tab, tab_t, tab_t],
        out_shape=[jax.ShapeDtypeStruct((n, LANES), F32)] * 2 + [jax.ShapeDtypeStruct((LANES, n), F32)] * 2,
        compiler_params=_params("parallel"),
        name="rope_tables",
    )(positions.reshape(n, 1), positions.reshape(1, n), inv_slot.reshape(1, LANES),
      sgn_slot.reshape(1, LANES), inv_slot.reshape(LANES, 1), sgn_slot.reshape(LANES, 1))


def _rope_slot(x, cos_t, sin_t):
    return x * cos_t + pltpu.roll(x, LANES // 2, axis=1) * sin_t


def _norm_kernel(h_ref, g_ref, o_ref):
    o_ref[...] = _rms(h_ref[...], g_ref[...]).astype(o_ref.dtype)


def _norm_bf16(h, g):
    n, d = h.shape
    tm = _tile(n, 512)
    return pl.pallas_call(
        _norm_kernel,
        grid=(n // tm,),
        in_specs=[pl.BlockSpec((tm, d), lambda i: (i, 0)), pl.BlockSpec((1, d), lambda i: (0, 0))],
        out_specs=pl.BlockSpec((tm, d), lambda i: (i, 0)),
        out_shape=jax.ShapeDtypeStruct((n, d), BF16),
        compiler_params=_params("parallel"),
        name="rmsnorm",
    )(h, g.reshape(1, d))


ROUTE_E, ROUTE_RANK = 0, TOP_K


def _router_kernel(h_ref, g_ref, wr_ref, ri_ref, rw_ref, cnt_ref, carry):
    @pl.when(pl.program_id(0) == 0)
    def _():
        carry[...] = jnp.zeros_like(carry)

    y = _rms(h_ref[...], g_ref[...])
    logits = jnp.dot(y, wr_ref[...], preferred_element_type=F32, precision=lax.Precision.HIGHEST)
    tm = logits.shape[0]
    lane = lax.broadcasted_iota(jnp.int32, logits.shape, 1)
    lane_f = lane.astype(F32)
    lg = jnp.where(lane < N_EXPERTS, logits, -jnp.inf)
    tops, hots = [], []
    for _ in range(TOP_K):
        mx = jnp.max(lg, axis=1, keepdims=True)
        idx = jnp.min(jnp.where(lg == mx, lane_f, float(LANES)), axis=1, keepdims=True)
        hot = lane_f == idx
        tops.append((mx, idx))
        hots.append(hot)
        lg = jnp.where(hot, -jnp.inf, lg)
    e1 = jnp.exp(tops[1][0] - tops[0][0])
    gates = (1.0 / (1.0 + e1), e1 / (1.0 + e1))
    hot_any = hots[0].astype(F32) + hots[1].astype(F32)
    tri = (lax.broadcasted_iota(jnp.int32, (tm, tm), 0) > lax.broadcasted_iota(jnp.int32, (tm, tm), 1))
    before = jnp.dot(tri.astype(BF16), hot_any.astype(BF16), preferred_element_type=F32) + carry[...]
    ranks = [jnp.sum(jnp.where(hot, before, 0.0), axis=1, keepdims=True) for hot in hots]
    carry[...] += jnp.sum(hot_any, axis=0, keepdims=True)
    cnt_ref[...] = carry[...]
    rec = jnp.zeros(logits.shape, jnp.int32)
    wrec = jnp.zeros(logits.shape, F32)
    for k in range(TOP_K):
        rec = jnp.where(lane == ROUTE_E + k, tops[k][1].astype(jnp.int32), rec)
        rec = jnp.where(lane == ROUTE_RANK + k, ranks[k].astype(jnp.int32), rec)
        wrec = jnp.where(lane == k, gates[k], wrec)
    ri_ref[...] = rec
    rw_ref[...] = wrec


def _router(h, g, w_router):
    n, d = h.shape
    tm = _tile(n, 512)
    wr = jnp.zeros((d, LANES), F32).at[:, :N_EXPERTS].set(w_router)
    rec = pl.BlockSpec((tm, LANES), lambda i: (i, 0))
    return pl.pallas_call(
        _router_kernel,
        grid=(n // tm,),
        in_specs=[pl.BlockSpec((tm, d), lambda i: (i, 0)), pl.BlockSpec((1, d), lambda i: (0, 0)),
                  pl.BlockSpec((d, LANES), lambda i: (0, 0))],
        out_specs=[rec, rec, pl.BlockSpec((1, LANES), lambda i: (0, 0))],
        out_shape=[jax.ShapeDtypeStruct((n, LANES), jnp.int32), jax.ShapeDtypeStruct((n, LANES), F32),
                   jax.ShapeDtypeStruct((1, LANES), F32)],
        scratch_shapes=[pltpu.VMEM((1, LANES), F32)],
        compiler_params=_params("arbitrary"),
        name="moe_router",
    )(h, g.reshape(1, d), wr)


def _pack_bf16_pairs(y):
    half = y.shape[1] // 2
    lo = lax.bitcast_convert_type(y[:, :half].astype(BF16).astype(F32), jnp.uint32)
    hi = lax.bitcast_convert_type(y[:, half:].astype(BF16).astype(F32), jnp.uint32)
    return hi | (lo >> 16)


def _unpack_bf16_pairs(w):
    lo = lax.bitcast_convert_type(w << 16, F32).astype(BF16)
    hi = lax.bitcast_convert_type(w & jnp.uint32(0xFFFF0000), F32).astype(BF16)
    return jnp.concatenate([lo, hi], axis=1)


def _row_copy_wait(src_row, dst_row, sem, count):
    def body(_, c):
        pltpu.make_async_copy(src_row, dst_row, sem).wait()
        return c
    lax.fori_loop(0, count, body, 0, unroll=16)


def _dispatch_kernel(dest_ref, h_ref, g_ref, xs_zero_ref, xs_ref, pbuf, sem):
    del xs_zero_ref
    i = pl.program_id(0)
    last = pl.num_programs(0) - 1
    slot = i % 2
    tm = h_ref.shape[0]

    def wait_slot(s):
        _row_copy_wait(pbuf.at[s, pl.ds(0, 1)], xs_ref.at[pl.ds(0, 1)], sem.at[s], TOP_K * tm)

    @pl.when(i >= 2)
    def _():
        wait_slot(slot)

    pbuf[slot] = _pack_bf16_pairs(_rms(h_ref[...], g_ref[...]))

    def send(r, c):
        for k in range(TOP_K):
            dst = dest_ref[0, 0, TOP_K * r + k]
            pltpu.make_async_copy(pbuf.at[slot, pl.ds(r, 1)], xs_ref.at[pl.ds(dst, 1)], sem.at[slot]).start()
        return c
    lax.fori_loop(0, tm, send, 0, unroll=8)

    @pl.when(i == last)
    def _():
        @pl.when(i >= 1)
        def _():
            wait_slot(1 - slot)
        wait_slot(slot)


def _dispatch(h, g, dest, n_slots):
    n, d = h.shape
    tm = _tile(n, 512)
    nt = n // tm
    xs0 = jnp.zeros((n_slots, d // 2), jnp.uint32)
    return pl.pallas_call(
        _dispatch_kernel,
        grid=(nt,),
        in_specs=[pl.BlockSpec((1, 1, TOP_K * tm), lambda i: (i, 0, 0), memory_space=pltpu.SMEM),
                  pl.BlockSpec((tm, d), lambda i: (i, 0)), pl.BlockSpec((1, d), lambda i: (0, 0)),
                  pl.BlockSpec(memory_space=pl.ANY)],
        out_specs=pl.BlockSpec(memory_space=pl.ANY),
        out_shape=jax.ShapeDtypeStruct((n_slots, d // 2), jnp.uint32),
        scratch_shapes=[pltpu.VMEM((2, tm, d // 2), jnp.uint32), pltpu.SemaphoreType.DMA((2,))],
        input_output_aliases={3: 0},
        compiler_params=_params("arbitrary"),
        name="moe_dispatch",
    )(dest.reshape(nt, 1, TOP_K * tm), h, g.reshape(1, d), xs0)


def _mm_kernel(epi, n_extra, x_ref, w_ref, *rest):
    extras = rest[:n_extra]
    outs = rest[n_extra:]
    acc = jnp.dot(x_ref[...], w_ref[...], preferred_element_type=F32)
    res = epi(acc, *extras)
    if not isinstance(res, (tuple, list)):
        res = (res,)
    for o, r in zip(outs, res):
        o[...] = r.astype(o.dtype)


def _mm(x, w, epi, *, tm, tn, extras=(), outs, name):
    m, k = x.shape
    n = w.shape[1]
    tm = _tile(m, tm)
    tn = _tile(n, tn)

    def spec(ncols, kind):
        if kind == "tile":
            return pl.BlockSpec((tm, tn), lambda j, i: (i, j))
        if kind == "rows":
            return pl.BlockSpec((tm, ncols), lambda j, i: (i, 0))
        assert kind == "vec", kind
        return pl.BlockSpec((1, ncols), lambda j, i: (0, 0))

    in_specs = [pl.BlockSpec((tm, k), lambda j, i: (i, 0)), pl.BlockSpec((k, tn), lambda j, i: (0, j))]
    in_specs += [spec(a.shape[1], kind) for a, kind in extras]
    return pl.pallas_call(
        functools.partial(_mm_kernel, epi, len(extras)),
        grid=(n // tn, m // tm),
        in_specs=in_specs,
        out_specs=[spec(nc, kind) for nc, _, kind in outs],
        out_shape=[jax.ShapeDtypeStruct((m, nc), dt) for nc, dt, _ in outs],
        compiler_params=_params("parallel", "parallel"),
        name=name,
    )(x, w, *[a for a, _ in extras])


def _epi_gelu(acc):
    return jax.nn.gelu(acc)


def _epi_gelu_group_ln(acc, g_ref, b_ref):
    y = jax.nn.gelu(acc)
    g = g_ref[...]
    b = b_ref[...]
    cols = []
    for gi in range(acc.shape[1] // GM_GROUP_W):
        sl = slice(gi * GM_GROUP_W, (gi + 1) * GM_GROUP_W)
        yg = y[:, sl]
        mu = jnp.mean(yg, axis=-1, keepdims=True)
        dlt = yg - mu
        var = jnp.mean(dlt * dlt, axis=-1, keepdims=True)
        cols.append(dlt * lax.rsqrt(var + EPS) * g[:, sl] + b[:, sl])
    return jnp.concatenate(cols, axis=1)


def _epi_sigmoid(acc):
    return jax.nn.sigmoid(acc)


def _epi_latent(acc, qn_ref, kvn_ref, cos_ref, sin_ref):
    c_q = _rms(acc[:, :Q_LORA], qn_ref[...])
    c_kv = _rms(acc[:, Q_LORA:Q_LORA + KV_LORA], kvn_ref[...])
    k_r = _rope_slot(acc[:, Q_LORA + KV_LORA:], cos_ref[...], sin_ref[...])
    return c_q, c_kv, k_r


def _q_proj_kernel(scale, c_ref, w_ref, cos_ref, sin_ref, o_ref):
    acc = lax.dot_general(w_ref[...], c_ref[...], (((1,), (1,)), ((), ())),
                          preferred_element_type=F32)
    cos_t = cos_ref[...] * scale
    sin_t = sin_ref[...] * scale
    half = LANES // 2
    for hd in range(MLA_HEADS):
        base = hd * HEAD_PAD
        o_ref[base:base + QK_NOPE, :] = (acc[base:base + QK_NOPE, :] * scale).astype(o_ref.dtype)
        r = acc[base + QK_NOPE:base + HEAD_PAD, :]
        rot = jnp.concatenate([r[half:, :], r[:half, :]], axis=0)
        o_ref[base + QK_NOPE:base + HEAD_PAD, :] = (r * cos_t + rot * sin_t).astype(o_ref.dtype)


def _q_proj(c_q, w_qt, cos_tt, sin_tt, scale):
    n, r = c_q.shape
    qw = w_qt.shape[0]
    tm = _tile(n, 512)
    tab = pl.BlockSpec((LANES, tm), lambda i: (0, i))
    return pl.pallas_call(
        functools.partial(_q_proj_kernel, scale),
        grid=(n // tm,),
        in_specs=[pl.BlockSpec((tm, r), lambda i: (i, 0)), pl.BlockSpec((qw, r), lambda i: (0, 0)), tab, tab],
        out_specs=pl.BlockSpec((qw, tm), lambda i: (0, i)),
        out_shape=jax.ShapeDtypeStruct((qw, n), BF16),
        compiler_params=_params("parallel"),
        name="mla_q_proj",
    )(c_q, w_qt, cos_tt, sin_tt)


def _kv_kernel(c_ref, wk_ref, wvt_ref, kr_ref, k_ref, vt_ref):
    c = c_ref[...]
    kn = jnp.dot(c, wk_ref[...], preferred_element_type=F32).astype(k_ref.dtype)
    kr = kr_ref[...]
    for hd in range(MLA_HEADS):
        k_ref[:, hd * HEAD_PAD:hd * HEAD_PAD + QK_NOPE] = kn[:, hd * QK_NOPE:(hd + 1) * QK_NOPE]
        k_ref[:, hd * HEAD_PAD + QK_NOPE:(hd + 1) * HEAD_PAD] = kr
    vt = lax.dot_general(wvt_ref[...], c, (((1,), (1,)), ((), ())),
                         preferred_element_type=F32).astype(vt_ref.dtype)
    pad_rows = V_ROWS - V_HEAD
    ones_row = (lax.broadcasted_iota(jnp.int32, (pad_rows, vt.shape[1]), 0) == 0).astype(vt_ref.dtype)
    for hd in range(MLA_HEADS):
        vt_ref[hd * V_ROWS:hd * V_ROWS + V_HEAD, :] = vt[hd * V_HEAD:(hd + 1) * V_HEAD, :]
        vt_ref[hd * V_ROWS + V_HEAD:(hd + 1) * V_ROWS, :] = ones_row


def _kv_proj(c_kv, w_uk, w_uvt, k_rope):
    n, r = c_kv.shape
    tm = _tile(n, 512)
    kw = MLA_HEADS * HEAD_PAD
    vw = MLA_HEADS * V_ROWS
    return pl.pallas_call(
        _kv_kernel,
        grid=(n // tm,),
        in_specs=[pl.BlockSpec((tm, r), lambda i: (i, 0)),
                  pl.BlockSpec(w_uk.shape, lambda i: (0, 0)),
                  pl.BlockSpec(w_uvt.shape, lambda i: (0, 0)),
                  pl.BlockSpec((tm, LANES), lambda i: (i, 0))],
        out_specs=[pl.BlockSpec((tm, kw), lambda i: (i, 0)), pl.BlockSpec((vw, tm), lambda i: (0, i))],
        out_shape=[jax.ShapeDtypeStruct((n, kw), BF16), jax.ShapeDtypeStruct((vw, n), BF16)],
        compiler_params=_params("parallel"),
        name="mla_kv_proj",
    )(c_kv, w_uk, w_uvt, k_rope)


def _attn_kernel(tk, p_dtype, qt_ref, k_ref, vt_ref, o_ref, s_a, s_b, p_a, p_b):
    tq = qt_ref.shape[1]
    n_chunks = k_ref.shape[0] // tk

    def scores(i, s_out):
        off = pl.multiple_of(i * tk, tk)
        s = jnp.dot(k_ref[pl.ds(off, tk), :], qt_ref[...], preferred_element_type=F32)
        s_out[...] = s
        return jnp.max(s, axis=0, keepdims=True)

    def softmax(s_in, p_out, mx, m):
        m_new = jnp.maximum(m, mx)
        x = s_in[...] - m_new
        p_out[...] = jnp.exp2(x.astype(p_dtype)).astype(p_out.dtype)
        return m_new, jnp.exp2(m - m_new)

    def values(i, p_in, alpha, acc):
        off = pl.multiple_of(i * tk, tk)
        return alpha * acc + jnp.dot(vt_ref[:, pl.ds(off, tk)], p_in[...], preferred_element_type=F32)

    def step(i, odd, carry, first=False, last=False):
        s_cur, s_nxt, p_cur, p_prv = (s_b, s_a, p_b, p_a) if odd else (s_a, s_b, p_a, p_b)
        mx, m, alpha, acc = carry
        if not first:
            acc = values(i - 1, p_prv, alpha, acc)
        mx_nxt = mx if last else scores(i + 1, s_nxt)
        m, alpha = softmax(s_cur, p_cur, mx, m)
        return mx_nxt, m, alpha, acc

    carry = (scores(0, s_a), jnp.full((1, tq), -jnp.inf, F32), jnp.ones((1, tq), F32),
             jnp.zeros((vt_ref.shape[0], tq), F32))
    for i in range(n_chunks):
        carry = step(i, i % 2 == 1, carry, first=i == 0, last=i == n_chunks - 1)
    _, _, alpha, acc = carry
    acc = values(n_chunks - 1, p_b if n_chunks % 2 == 0 else p_a, alpha, acc)
    o_ref[...] = jnp.transpose(acc[:V_HEAD, :] / acc[V_HEAD:V_HEAD + 1, :]).astype(o_ref.dtype)


def _attention(qt, k, vt, batch, seq, p_dtype):
    n = k.shape[0]
    tq = _tile(seq, 512)
    tk = _tile(seq // 2, 512)
    assert (seq // tk) % 2 == 0
    nq = seq // tq
    return pl.pallas_call(
        functools.partial(_attn_kernel, tk, p_dtype),
        grid=(batch, MLA_HEADS, nq),
        scratch_shapes=[pltpu.VMEM((tk, tq), F32), pltpu.VMEM((tk, tq), F32),
                        pltpu.VMEM((tk, tq), BF16), pltpu.VMEM((tk, tq), BF16)],
        in_specs=[pl.BlockSpec((HEAD_PAD, tq), lambda b, h, i: (h, b * nq + i)),
                  pl.BlockSpec((seq, HEAD_PAD), lambda b, h, i: (b, h)),
                  pl.BlockSpec((V_ROWS, seq), lambda b, h, i: (h, b))],
        out_specs=pl.BlockSpec((tq, V_HEAD), lambda b, h, i: (b * nq + i, h)),
        out_shape=jax.ShapeDtypeStruct((n, MLA_HEADS * V_HEAD), BF16),
        compiler_params=_params("parallel", "parallel", "arbitrary"),
        name="mla_attention",
    )(qt, k, vt)


def _gmlp_kernel(u_ref, v_ref, ws_ref, bs_ref, o_ref):
    n_chunks = u_ref.shape[0] // CHUNK
    for c in range(n_chunks):
        rows = slice(c * CHUNK, (c + 1) * CHUNK)
        for g in range(GM_GROUPS):
            cols = slice(g * GM_GROUP_W, (g + 1) * GM_GROUP_W)
            mixed = jnp.dot(ws_ref[g], v_ref[rows, cols], preferred_element_type=F32)
            o_ref[rows, cols] = (u_ref[rows, cols].astype(F32) * (mixed + bs_ref[:, cols])).astype(o_ref.dtype)


def _gmlp_mix(u, v, ws, bs_tile):
    n, w = u.shape
    tm = _tile(n, 4 * CHUNK)
    blk = pl.BlockSpec((tm, w), lambda i: (i, 0))
    return pl.pallas_call(
        _gmlp_kernel,
        grid=(n // tm,),
        in_specs=[blk, blk, pl.BlockSpec(ws.shape, lambda i: (0, 0, 0)),
                  pl.BlockSpec(bs_tile.shape, lambda i: (0, 0))],
        out_specs=blk,
        out_shape=jax.ShapeDtypeStruct((n, w), BF16),
        compiler_params=_params("parallel"),
        name="gmlp_spatial",
    )(u, v, ws, bs_tile)


def _merge_kernel(a_ref, wa_ref, b_ref, wb_ref, ga_ref, gb_ref, o_ref):
    ya = jnp.dot(a_ref[...], wa_ref[...], preferred_element_type=F32)
    yb = jnp.dot(b_ref[...], wb_ref[...], preferred_element_type=F32)
    o_ref[...] = (ga_ref[...].astype(F32) * ya + gb_ref[...].astype(F32) * yb).astype(o_ref.dtype)


def _branch_merge(a, wa, b, wb, gates):
    n, ka = a.shape
    kb = b.shape[1]
    d = wa.shape[1]
    tm = _tile(n, 1024)
    tn = _tile(d, 1024)
    nj = d // tn
    return pl.pallas_call(
        _merge_kernel,
        grid=(nj, n // tm),
        in_specs=[pl.BlockSpec((tm, ka), lambda j, i: (i, 0)), pl.BlockSpec((ka, tn), lambda j, i: (0, j)),
                  pl.BlockSpec((tm, kb), lambda j, i: (i, 0)), pl.BlockSpec((kb, tn), lambda j, i: (0, j)),
                  pl.BlockSpec((tm, tn), lambda j, i: (i, j)),
                  pl.BlockSpec((tm, tn), lambda j, i: (i, j + nj))],
        out_specs=pl.BlockSpec((tm, tn), lambda j, i: (i, j)),
        out_shape=jax.ShapeDtypeStruct((n, d), BF16),
        compiler_params=_params("parallel", "parallel"),
        name="branch_merge",
    )(a, wa, b, wb, gates, gates)


def _epi_norm_residual(acc, h_ref, g_ref):
    return h_ref[...] + _rms(acc, g_ref[...])


def _swiglu_step(x, wg_ref, wu_ref, wd_ref):
    g = jnp.dot(x, wg_ref[...], preferred_element_type=F32)
    u = jnp.dot(x, wu_ref[...], preferred_element_type=F32)
    a = (jax.nn.silu(g) * u).astype(x.dtype)
    return jnp.dot(a, wd_ref[...], preferred_element_type=F32)


def _dense_ffn_kernel(x_ref, wg_ref, wu_ref, wd_ref, o_ref):
    f = pl.program_id(1)
    @pl.when(f == 0)
    def _():
        o_ref[...] = jnp.zeros_like(o_ref)

    o_ref[...] += _swiglu_step(x_ref[...], wg_ref, wu_ref, wd_ref)


def _dense_ffn(x, w_gu, w_down):
    n, d = x.shape
    dff = w_down.shape[0]
    tm = _tile(n, 1024)
    tf = _tile(dff, 512)
    nf = dff // tf
    return pl.pallas_call(
        _dense_ffn_kernel,
        grid=(n // tm, nf),
        in_specs=[pl.BlockSpec((tm, d), lambda i, f: (i, 0)),
                  pl.BlockSpec((d, tf), lambda i, f: (0, f)),
                  pl.BlockSpec((d, tf), lambda i, f: (0, f + nf)),
                  pl.BlockSpec((tf, d), lambda i, f: (f, 0))],
        out_specs=pl.BlockSpec((tm, d), lambda i, f: (i, 0)),
        out_shape=jax.ShapeDtypeStruct((n, d), F32),
        compiler_params=_params("parallel", "arbitrary"),
        name="dense_swiglu",
    )(x, w_gu, w_gu, w_down)


def _moe_ffn_kernel(te_ref, th_ref, nu_ref, x_ref, wg_ref, wu_ref, wd_ref, o_ref):
    t = pl.program_id(0)
    f = pl.program_id(1)
    n_half = th_ref[t]

    @pl.when(f == 0)
    def _():
        o_ref[...] = jnp.zeros_like(o_ref)

    @pl.when(n_half > 0)
    def _():
        wg = wg_ref[...].astype(BF16)
        wu = wu_ref[...].astype(BF16)
        wd = wd_ref[...].astype(BF16)

        def half(rows):
            x = _unpack_bf16_pairs(x_ref[rows, :])
            g = jnp.dot(x, wg, preferred_element_type=F32)
            u = jnp.dot(x, wu, preferred_element_type=F32)
            y = jnp.dot((jax.nn.silu(g) * u).astype(BF16), wd, preferred_element_type=F32)
            o_ref[rows, :] += y

        half(slice(0, MOE_BLOCK))

        @pl.when(n_half > 1)
        def _():
            half(slice(MOE_BLOCK, MOE_TILE))


def _moe_ffn(xs, tile_e, tile_halves, n_used, w_gu, w_down):
    n_slots = xs.shape[0]
    d = w_down.shape[2]
    dff = w_down.shape[1]
    tm = MOE_TILE
    tf = _tile(dff, 512)
    nf = dff // tf

    def _tt(t, nu):
        return jnp.minimum(t, jnp.maximum(nu[0] - 1, 0))

    def _e(t, te, nu):
        return te[_tt(t, nu)]

    def _ff(t, f, nu):
        return jnp.where(t < nu[0], f, nf - 1)

    grid_spec = pltpu.PrefetchScalarGridSpec(
        num_scalar_prefetch=3,
        grid=(n_slots // tm, nf),
        in_specs=[
            pl.BlockSpec((tm, d // 2), lambda t, f, te, th, nu: (_tt(t, nu), 0)),
            pl.BlockSpec((None, d, tf), lambda t, f, te, th, nu: (_e(t, te, nu), 0, _ff(t, f, nu))),
            pl.BlockSpec((None, d, tf), lambda t, f, te, th, nu: (_e(t, te, nu), 0, _ff(t, f, nu) + nf)),
            pl.BlockSpec((None, tf, d), lambda t, f, te, th, nu: (_e(t, te, nu), _ff(t, f, nu), 0)),
        ],
        out_specs=pl.BlockSpec((tm, d), lambda t, f, te, th, nu: (t, 0), pipeline_mode=pl.Buffered(1)),
    )
    return pl.pallas_call(
        _moe_ffn_kernel,
        grid_spec=grid_spec,
        out_shape=jax.ShapeDtypeStruct((n_slots, d), F32),
        compiler_params=_params("arbitrary", "arbitrary"),
        name="moe_swiglu",
    )(tile_e, tile_halves, n_used, xs, w_gu, w_gu, w_down)


def _post_ple_math(h, f, gf_ref, p_ref, wp_ref, wg_ref, gp_ref, o_ref):
    h = h + _rms(f, gf_ref[...])
    e = jnp.dot(p_ref[...].astype(BF16), wp_ref[...], preferred_element_type=F32)
    gate = jax.nn.sigmoid(jnp.dot(h.astype(BF16), wg_ref[...], preferred_element_type=F32))
    o_ref[...] = h + _rms(gate * e, gp_ref[...])


def _ple_kernel(h_ref, f_ref, gf_ref, p_ref, wp_ref, wg_ref, gp_ref, o_ref):
    _post_ple_math(h_ref[...], f_ref[...], gf_ref, p_ref, wp_ref, wg_ref, gp_ref, o_ref)


def _ple_specs(n, d, pd, tm):
    blk = pl.BlockSpec((tm, d), lambda i: (i, 0))
    row = pl.BlockSpec((1, d), lambda i: (0, 0))
    tail = [row, pl.BlockSpec((tm, pd), lambda i: (i, 0)), pl.BlockSpec((pd, d), lambda i: (0, 0)),
            pl.BlockSpec((d, d), lambda i: (0, 0)), row]
    return blk, tail


def _ffn_post_ple(h, f, g_post, p, w_ple_in, w_ple_gate, g_ple):
    n, d = h.shape
    tm = _tile(n, 256)
    blk, tail = _ple_specs(n, d, p.shape[1], tm)
    return pl.pallas_call(
        _ple_kernel,
        grid=(n // tm,),
        in_specs=[blk, blk] + tail,
        out_specs=blk,
        out_shape=jax.ShapeDtypeStruct((n, d), F32),
        compiler_params=_params("parallel"),
        name="ffn_post_ple",
    )(h, f, g_post.reshape(1, d), p, w_ple_in, w_ple_gate, g_ple.reshape(1, d))


def _ple_combine_kernel(d_cur_ref, d_nxt_ref, h_ref, rw_ref, gf_ref, p_ref, wp_ref, wg_ref, gp_ref,
                        ys_ref, o_ref, gbuf, sem):
    i = pl.program_id(0)
    last = pl.num_programs(0) - 1
    slot = i % 2
    tm = h_ref.shape[0]

    def fetch(d_ref, s):
        def body(r, c):
            for k in range(TOP_K):
                src = d_ref[0, 0, TOP_K * r + k]
                pltpu.make_async_copy(ys_ref.at[pl.ds(src, 1)], gbuf.at[s, k, pl.ds(r, 1)], sem.at[s]).start()
            return c
        lax.fori_loop(0, tm, body, 0, unroll=8)

    @pl.when(i == 0)
    def _():
        fetch(d_cur_ref, 0)

    @pl.when(i < last)
    def _():
        fetch(d_nxt_ref, 1 - slot)

    _row_copy_wait(ys_ref.at[pl.ds(0, 1)], gbuf.at[slot, 0, pl.ds(0, 1)], sem.at[slot], TOP_K * tm)
    f = rw_ref[:, 0:1] * gbuf[slot, 0]
    for k in range(1, TOP_K):
        f = f + rw_ref[:, k:k + 1] * gbuf[slot, k]
    _post_ple_math(h_ref[...], f, gf_ref, p_ref, wp_ref, wg_ref, gp_ref, o_ref)


def _moe_combine_post_ple(h, ys, dest, route_w, g_post, p, w_ple_in, w_ple_gate, g_ple):
    n, d = h.shape
    tm = _tile(n, 256)
    nt = n // tm
    blk, tail = _ple_specs(n, d, p.shape[1], tm)
    dest3 = dest.reshape(nt, 1, TOP_K * tm)
    idx = functools.partial(pl.BlockSpec, (1, 1, TOP_K * tm), memory_space=pltpu.SMEM)
    return pl.pallas_call(
        _ple_combine_kernel,
        grid=(nt,),
        in_specs=[idx(lambda i: (i, 0, 0)), idx(lambda i: (jnp.minimum(i + 1, nt - 1), 0, 0)),
                  blk, pl.BlockSpec((tm, LANES), lambda i: (i, 0))] + tail
                 + [pl.BlockSpec(memory_space=pl.ANY)],
        out_specs=blk,
        out_shape=jax.ShapeDtypeStruct((n, d), F32),
        scratch_shapes=[pltpu.VMEM((2, TOP_K, tm, d), F32), pltpu.SemaphoreType.DMA((2,))],
        compiler_params=_params("arbitrary"),
        name="moe_combine_post_ple",
    )(dest3, dest3, h, route_w, g_post.reshape(1, d), p, w_ple_in, w_ple_gate, g_ple.reshape(1, d), ys)


def _slot_tables(route_i, counts_f, n_tok):
    a = n_tok * TOP_K
    counts = counts_f[0, :N_EXPERTS].astype(jnp.int32)
    padded = (counts + MOE_TILE - 1) // MOE_TILE * MOE_TILE
    pad_end = jnp.cumsum(padded)
    pad_start = pad_end - padded
    experts = route_i[:, ROUTE_E:ROUTE_E + TOP_K]
    dest = pad_start[experts] + route_i[:, ROUTE_RANK:ROUTE_RANK + TOP_K]
    n_tiles = -(-a // MOE_TILE) + N_EXPERTS
    tile_start = jnp.arange(n_tiles, dtype=jnp.int32) * MOE_TILE
    tile_e = jnp.minimum(jnp.searchsorted(pad_end, tile_start, side='right'), N_EXPERTS - 1).astype(jnp.int32)
    rows = jnp.clip(pad_start[tile_e] + counts[tile_e] - tile_start, 0, MOE_TILE)
    tile_halves = ((rows + MOE_BLOCK - 1) // MOE_BLOCK).astype(jnp.int32)
    n_used = (pad_end[-1] // MOE_TILE).astype(jnp.int32).reshape(1)
    return dest.astype(jnp.int32), tile_e, tile_halves, n_used, n_tiles * MOE_TILE


def _rope_slot_cols(w):
    half = QK_ROPE // 2
    z = jnp.zeros(w.shape[:-1] + (half,), w.dtype)
    return jnp.concatenate([w[..., :half], z, w[..., half:], z], axis=-1)


def _prep_layer(i, w_in, gm_ws, gm_bs, w_uq, w_ukv, w_branch_a, w_branch_b, w_out, w_ple_in, w_ple_gate):
    d = w_in.shape[1]
    wi = w_in[i]
    o = 0
    w_u = wi[:, o:o + GM_W]; o += GM_W
    w_v = wi[:, o:o + GM_W]; o += GM_W
    w_cq = wi[:, o:o + Q_LORA]; o += Q_LORA
    w_ckv = wi[:, o:o + KV_LORA]; o += KV_LORA
    w_kr = wi[:, o:o + QK_ROPE]; o += QK_ROPE
    w_g = wi[:, o:o + 2 * d]
    w_lat = jnp.concatenate([w_cq, w_ckv, _rope_slot_cols(w_kr)], axis=1)
    wq = w_uq[i].reshape(Q_LORA, MLA_HEADS, QK_NOPE + QK_ROPE)
    wq = jnp.concatenate([wq[..., :QK_NOPE], _rope_slot_cols(wq[..., QK_NOPE:])], axis=-1)
    wkv = w_ukv[i].reshape(KV_LORA, MLA_HEADS, QK_NOPE + V_HEAD)
    bs_tile = jnp.repeat(gm_bs[i].T, GM_GROUP_W, axis=1)
    return dict(
        w_u=w_u.astype(BF16), w_v=w_v.astype(BF16), w_lat=w_lat.astype(BF16), w_g=w_g.astype(BF16),
        w_qt=wq.reshape(Q_LORA, MLA_HEADS * HEAD_PAD).T.astype(BF16),
        w_uk=wkv[..., :QK_NOPE].reshape(KV_LORA, MLA_HEADS * QK_NOPE).astype(BF16),
        w_uvt=wkv[..., QK_NOPE:].reshape(KV_LORA, MLA_HEADS * V_HEAD).T.astype(BF16),
        ws=gm_ws[i].astype(BF16), bs_tile=bs_tile,
        w_a=w_branch_a[i].astype(BF16), w_b=w_branch_b[i].astype(BF16), w_o=w_out[i].astype(BF16),
        w_pi=w_ple_in[i].astype(BF16), w_pg=w_ple_gate[i].astype(BF16),
    )


def kernel(x, p, positions, norm_pre_mix, norm_post_mix, norm_pre_ffn, norm_post_ffn, w_in, gm_ln_g, gm_ln_b, gm_ws, gm_bs, mla_q_norm, w_uq, mla_kv_norm, w_ukv, w_branch_a, w_branch_b, w_out, w_dense_gu, w_dense_down, w_router, w_exp_gu, w_exp_down, w_ple_in, w_ple_gate, ple_norm):
    batch, seq, d = x.shape
    depth = p.shape[0]
    n = batch * seq
    scale = (QK_NOPE + QK_ROPE) ** -0.5 * LOG2_E
    cos_t, sin_t, cos_tt, sin_tt = _rope_tables(positions)
    h = x.reshape(n, d)

    for li in range(depth):
        wts = _prep_layer(li, w_in, gm_ws, gm_bs, w_uq, w_ukv, w_branch_a, w_branch_b, w_out,
                          w_ple_in, w_ple_gate)
        hn = _norm_bf16(h, norm_pre_mix[li])
        (u,) = _mm(hn, wts["w_u"], _epi_gelu, tm=1024, tn=GM_W, outs=[(GM_W, BF16, "tile")],
                   name="in_proj_u")
        (v,) = _mm(hn, wts["w_v"], _epi_gelu_group_ln, tm=512, tn=GM_W,
                   extras=[(gm_ln_g[li].reshape(1, GM_W), "vec"), (gm_ln_b[li].reshape(1, GM_W), "vec")],
                   outs=[(GM_W, BF16, "tile")], name="in_proj_v")
        c_q, c_kv, k_r = _mm(
            hn, wts["w_lat"], _epi_latent, tm=512, tn=wts["w_lat"].shape[1],
            extras=[(mla_q_norm[li].reshape(1, Q_LORA), "vec"), (mla_kv_norm[li].reshape(1, KV_LORA), "vec"),
                    (cos_t, "rows"), (sin_t, "rows")],
            outs=[(Q_LORA, BF16, "rows"), (KV_LORA, BF16, "rows"), (LANES, BF16, "rows")],
            name="in_proj_latent")
        (gates,) = _mm(hn, wts["w_g"], _epi_sigmoid, tm=1024, tn=1024, outs=[(2 * d, BF16, "tile")],
                       name="in_proj_gates")
        qt = _q_proj(c_q, wts["w_qt"], cos_tt, sin_tt, scale)
        k, vt = _kv_proj(c_kv, wts["w_uk"], wts["w_uvt"], k_r)
        attn = _attention(qt, k, vt, batch, seq, F32 if li == 0 else BF16)
        gm = _gmlp_mix(u, v, wts["ws"], wts["bs_tile"])
        merged = _branch_merge(gm, wts["w_a"], attn, wts["w_b"], gates)
        (h,) = _mm(merged, wts["w_o"], _epi_norm_residual, tm=512, tn=d,
                   extras=[(h, "rows"), (norm_post_mix[li].reshape(1, d), "vec")],
                   outs=[(d, F32, "tile")], name="out_proj")
        p_li = p[li].reshape(n, -1)
        if li % 2 == 0:
            hn = _norm_bf16(h, norm_pre_ffn[li])
            f = _dense_ffn(hn, w_dense_gu[li // 2].astype(BF16), w_dense_down[li // 2].astype(BF16))
            h = _ffn_post_ple(h, f, norm_post_ffn[li], p_li, wts["w_pi"], wts["w_pg"], ple_norm[li])
        else:
            route_i, route_w, counts = _router(h, norm_pre_ffn[li], w_router[li // 2])
            dest, tile_e, tile_halves, n_used, n_slots = _slot_tables(route_i, counts, n)
            xs = _dispatch(h, norm_pre_ffn[li], dest, n_slots)
            ys = _moe_ffn(xs, tile_e, tile_halves, n_used, w_exp_gu[li // 2], w_exp_down[li // 2])
            h = _moe_combine_post_ple(h, ys, dest, route_w, norm_post_ffn[li], p_li, wts["w_pi"],
                                      wts["w_pg"], ple_norm[li])
    return h.reshape(batch, seq, d)
```

```python
import functools

import jax
import jax.numpy as jnp
from jax import lax
from jax.experimental import pallas as pl
from jax.experimental.pallas import tpu as pltpu

F32 = jnp.float32
BF16 = jnp.bfloat16

GM_GROUPS = 8
GM_GROUP_W = 128
GM_W = GM_GROUPS * GM_GROUP_W
CHUNK = 128
MLA_HEADS = 8
Q_LORA = 512
KV_LORA = 256
QK_NOPE = 128
QK_ROPE = 64
V_HEAD = 128
ROPE_THETA = 10000.0
N_EXPERTS = 8
TOP_K = 2
MOE_BLOCK = 512
MOE_TILE = 2 * MOE_BLOCK
EPS = 1e-6
LOG2_E = 1.4426950408889634

LANES = 128
HEAD_PAD = 2 * LANES
BF16_SUBLANES = 16
V_ROWS = V_HEAD + BF16_SUBLANES
ATTN_TQ = 2048
ATTN_TK = 512
V7X_VMEM_BYTES = 64 * 1024 * 1024
VMEM_LIMIT = V7X_VMEM_BYTES - 8 * 1024 * 1024


def _params(*sem):
    return pltpu.CompilerParams(dimension_semantics=sem, vmem_limit_bytes=VMEM_LIMIT)


def _tile(n, want):
    t = min(n, want)
    while n % t:
        t //= 2
    return t


def _rms(x, g):
    return x * lax.rsqrt(jnp.mean(x * x, axis=-1, keepdims=True) + EPS) * g


def _rope_table_kernel(pos_c_ref, pos_r_ref, inv_r_ref, sgn_r_ref, inv_c_ref, sgn_c_ref,
                       cos_ref, sin_ref, cos_t_ref, sin_t_ref):
    ang = pos_c_ref[...].astype(F32) * inv_r_ref[...]
    cos_ref[...] = jnp.cos(ang) * jnp.abs(sgn_r_ref[...])
    sin_ref[...] = jnp.sin(ang) * sgn_r_ref[...]
    ang_t = inv_c_ref[...] * pos_r_ref[...].astype(F32)
    cos_t_ref[...] = jnp.cos(ang_t) * jnp.abs(sgn_c_ref[...])
    sin_t_ref[...] = jnp.sin(ang_t) * sgn_c_ref[...]


def _rope_tables(positions):
    n = positions.size
    tm = _tile(n, 1024)
    half = QK_ROPE // 2
    inv = ROPE_THETA ** (-jnp.arange(0, QK_ROPE, 2, dtype=F32) / QK_ROPE)
    z = jnp.zeros((half,), F32)
    inv_slot = jnp.concatenate([inv, z, inv, z])
    one = jnp.ones((half,), F32)
    sgn_slot = jnp.concatenate([-one, z, one, z])
    row = pl.BlockSpec((1, LANES), lambda i: (0, 0))
    col = pl.BlockSpec((LANES, 1), lambda i: (0, 0))
    tab = pl.BlockSpec((tm, LANES), lambda i: (i, 0))
    tab_t = pl.BlockSpec((LANES, tm), lambda i: (0, i))
    return pl.pallas_call(
        _rope_table_kernel,
        grid=(n // tm,),
        in_specs=[pl.BlockSpec((tm, 1), lambda i: (i, 0)), pl.BlockSpec((1, tm), lambda i: (0, i)),
                  row, row, col, col],
        out_specs=[tab, tab, tab_t, tab_t],
        out_shape=[jax.ShapeDtypeStruct((n, LANES), F32)] * 2 + [jax.ShapeDtypeStruct((LANES, n), F32)] * 2,
        compiler_params=_params("parallel"),
        name="rope_tables",
    )(positions.reshape(n, 1), positions.reshape(1, n), inv_slot.reshape(1, LANES),
      sgn_slot.reshape(1, LANES), inv_slot.reshape(LANES, 1), sgn_slot.reshape(LANES, 1))


def _rope_slot(x, cos_t, sin_t):
    return x * cos_t + pltpu.roll(x, LANES // 2, axis=1) * sin_t


def _norm_kernel(h_ref, g_ref, o_ref):
    o_ref[...] = _rms(h_ref[...], g_ref[...]).astype(o_ref.dtype)


def _norm_bf16(h, g):
    n, d = h.shape
    tm = _tile(n, 512)
    return pl.pallas_call(
        _norm_kernel,
        grid=(n // tm,),
        in_specs=[pl.BlockSpec((tm, d), lambda i: (i, 0)), pl.BlockSpec((1, d), lambda i: (0, 0))],
        out_specs=pl.BlockSpec((tm, d), lambda i: (i, 0)),
        out_shape=jax.ShapeDtypeStruct((n, d), BF16),
        compiler_params=_params("parallel"),
        name="rmsnorm",
    )(h, g.reshape(1, d))


ROUTE_E, ROUTE_RANK = 0, TOP_K


def _router_kernel(h_ref, g_ref, wr_ref, ri_ref, rw_ref, cnt_ref, carry):
    @pl.when(pl.program_id(0) == 0)
    def _():
        carry[...] = jnp.zeros_like(carry)

    y = _rms(h_ref[...], g_ref[...])
    logits = jnp.dot(y, wr_ref[...], preferred_element_type=F32, precision=lax.Precision.HIGHEST)
    tm = logits.shape[0]
    lane = lax.broadcasted_iota(jnp.int32, logits.shape, 1)
    lane_f = lane.astype(F32)
    lg = jnp.where(lane < N_EXPERTS, logits, -jnp.inf)
    tops, hots = [], []
    for _ in range(TOP_K):
        mx = jnp.max(lg, axis=1, keepdims=True)
        idx = jnp.min(jnp.where(lg == mx, lane_f, float(LANES)), axis=1, keepdims=True)
        hot = lane_f == idx
        tops.append((mx, idx))
        hots.append(hot)
        lg = jnp.where(hot, -jnp.inf, lg)
    e1 = jnp.exp(tops[1][0] - tops[0][0])
    gates = (1.0 / (1.0 + e1), e1 / (1.0 + e1))
    hot_any = hots[0].astype(F32) + hots[1].astype(F32)
    tri = (lax.broadcasted_iota(jnp.int32, (tm, tm), 0) > lax.broadcasted_iota(jnp.int32, (tm, tm), 1))
    before = jnp.dot(tri.astype(BF16), hot_any.astype(BF16), preferred_element_type=F32) + carry[...]
    ranks = [jnp.sum(jnp.where(hot, before, 0.0), axis=1, keepdims=True) for hot in hots]
    carry[...] += jnp.sum(hot_any, axis=0, keepdims=True)
    cnt_ref[...] = carry[...]
    rec = jnp.zeros(logits.shape, jnp.int32)
    wrec = jnp.zeros(logits.shape, F32)
    for k in range(TOP_K):
        rec = jnp.where(lane == ROUTE_E + k, tops[k][1].astype(jnp.int32), rec)
        rec = jnp.where(lane == ROUTE_RANK + k, ranks[k].astype(jnp.int32), rec)
        wrec = jnp.where(lane == k, gates[k], wrec)
    ri_ref[...] = rec
    rw_ref[...] = wrec


def _router(h, g, w_router):
    n, d = h.shape
    tm = _tile(n, 512)
    wr = jnp.zeros((d, LANES), F32).at[:, :N_EXPERTS].set(w_router)
    rec = pl.BlockSpec((tm, LANES), lambda i: (i, 0))
    return pl.pallas_call(
        _router_kernel,
        grid=(n // tm,),
        in_specs=[pl.BlockSpec((tm, d), lambda i: (i, 0)), pl.BlockSpec((1, d), lambda i: (0, 0)),
                  pl.BlockSpec((d, LANES), lambda i: (0, 0))],
        out_specs=[rec, rec, pl.BlockSpec((1, LANES), lambda i: (0, 0))],
        out_shape=[jax.ShapeDtypeStruct((n, LANES), jnp.int32), jax.ShapeDtypeStruct((n, LANES), F32),
                   jax.ShapeDtypeStruct((1, LANES), F32)],
        scratch_shapes=[pltpu.VMEM((1, LANES), F32)],
        compiler_params=_params("arbitrary"),
        name="moe_router",
    )(h, g.reshape(1, d), wr)


def _pack_bf16_pairs(y):
    half = y.shape[1] // 2
    lo = lax.bitcast_convert_type(y[:, :half].astype(BF16).astype(F32), jnp.uint32)
    hi = lax.bitcast_convert_type(y[:, half:].astype(BF16).astype(F32), jnp.uint32)
    return hi | (lo >> 16)


def _unpack_bf16_pairs(w):
    lo = lax.bitcast_convert_type(w << 16, F32).astype(BF16)
    hi = lax.bitcast_convert_type(w & jnp.uint32(0xFFFF0000), F32).astype(BF16)
    return jnp.concatenate([lo, hi], axis=1)


def _row_copy_wait(src_row, dst_row, sem, count):
    def body(_, c):
        pltpu.make_async_copy(src_row, dst_row, sem).wait()
        return c
    lax.fori_loop(0, count, body, 0, unroll=16)


def _dispatch_kernel(dest_ref, h_ref, g_ref, xs_zero_ref, xs_ref, pbuf, sem):
    del xs_zero_ref
    i = pl.program_id(0)
    last = pl.num_programs(0) - 1
    slot = i % 2
    tm = h_ref.shape[0]

    def wait_slot(s):
        _row_copy_wait(pbuf.at[s, pl.ds(0, 1)], xs_ref.at[pl.ds(0, 1)], sem.at[s], TOP_K * tm)

    @pl.when(i >= 2)
    def _():
        wait_slot(slot)

    pbuf[slot] = _pack_bf16_pairs(_rms(h_ref[...], g_ref[...]))

    def send(r, c):
        for k in range(TOP_K):
            dst = dest_ref[0, 0, TOP_K * r + k]
            pltpu.make_async_copy(pbuf.at[slot, pl.ds(r, 1)], xs_ref.at[pl.ds(dst, 1)], sem.at[slot]).start()
        return c
    lax.fori_loop(0, tm, send, 0, unroll=8)

    @pl.when(i == last)
    def _():
        @pl.when(i >= 1)
        def _():
            wait_slot(1 - slot)
        wait_slot(slot)


def _dispatch(h, g, dest, n_slots):
    n, d = h.shape
    tm = _tile(n, 512)
    nt = n // tm
    xs0 = jnp.zeros((n_slots, d // 2), jnp.uint32)
    return pl.pallas_call(
        _dispatch_kernel,
        grid=(nt,),
        in_specs=[pl.BlockSpec((1, 1, TOP_K * tm), lambda i: (i, 0, 0), memory_space=pltpu.SMEM),
                  pl.BlockSpec((tm, d), lambda i: (i, 0)), pl.BlockSpec((1, d), lambda i: (0, 0)),
                  pl.BlockSpec(memory_space=pl.ANY)],
        out_specs=pl.BlockSpec(memory_space=pl.ANY),
        out_shape=jax.ShapeDtypeStruct((n_slots, d // 2), jnp.uint32),
        scratch_shapes=[pltpu.VMEM((2, tm, d // 2), jnp.uint32), pltpu.SemaphoreType.DMA((2,))],
        input_output_aliases={3: 0},
        compiler_params=_params("arbitrary"),
        name="moe_dispatch",
    )(dest.reshape(nt, 1, TOP_K * tm), h, g.reshape(1, d), xs0)


def _mm_kernel(epi, n_extra, x_ref, w_ref, *rest):
    extras = rest[:n_extra]
    outs = rest[n_extra:]
    acc = jnp.dot(x_ref[...], w_ref[...], preferred_element_type=F32)
    res = epi(acc, *extras)
    if not isinstance(res, (tuple, list)):
        res = (res,)
    for o, r in zip(outs, res):
        o[...] = r.astype(o.dtype)


def _mm(x, w, epi, *, tm, tn, extras=(), outs, name):
    m, k = x.shape
    n = w.shape[1]
    tm = _tile(m, tm)
    tn = _tile(n, tn)

    def spec(ncols, kind):
        if kind == "tile":
            return pl.BlockSpec((tm, tn), lambda j, i: (i, j))
        if kind == "rows":
            return pl.BlockSpec((tm, ncols), lambda j, i: (i, 0))
        assert kind == "vec", kind
        return pl.BlockSpec((1, ncols), lambda j, i: (0, 0))

    in_specs = [pl.BlockSpec((tm, k), lambda j, i: (i, 0)), pl.BlockSpec((k, tn), lambda j, i: (0, j))]
    in_specs += [spec(a.shape[1], kind) for a, kind in extras]
    return pl.pallas_call(
        functools.partial(_mm_kernel, epi, len(extras)),
        grid=(n // tn, m // tm),
        in_specs=in_specs,
        out_specs=[spec(nc, kind) for nc, _, kind in outs],
        out_shape=[jax.ShapeDtypeStruct((m, nc), dt) for nc, dt, _ in outs],
        compiler_params=_params("parallel", "parallel"),
        name=name,
    )(x, w, *[a for a, _ in extras])


def _epi_gelu(acc):
    return jax.nn.gelu(acc)


def _epi_gelu_group_ln(acc, g_ref, b_ref):
    y = jax.nn.gelu(acc)
    g = g_ref[...]
    b = b_ref[...]
    cols = []
    for gi in range(acc.shape[1] // GM_GROUP_W):
        sl = slice(gi * GM_GROUP_W, (gi + 1) * GM_GROUP_W)
        yg = y[:, sl]
        mu = jnp.mean(yg, axis=-1, keepdims=True)
        dlt = yg - mu
        var = jnp.mean(dlt * dlt, axis=-1, keepdims=True)
        cols.append(dlt * lax.rsqrt(var + EPS) * g[:, sl] + b[:, sl])
    return jnp.concatenate(cols, axis=1)


def _epi_sigmoid(acc):
    return jax.nn.sigmoid(acc)


def _epi_latent(acc, qn_ref, kvn_ref, cos_ref, sin_ref):
    c_q = _rms(acc[:, :Q_LORA], qn_ref[...])
    c_kv = _rms(acc[:, Q_LORA:Q_LORA + KV_LORA], kvn_ref[...])
    k_r = _rope_slot(acc[:, Q_LORA + KV_LORA:], cos_ref[...], sin_ref[...])
    return c_q, c_kv, k_r


def _q_proj_kernel(scale, c_ref, w_ref, cos_ref, sin_ref, o_ref):
    acc = lax.dot_general(w_ref[...], c_ref[...], (((1,), (1,)), ((), ())),
                          preferred_element_type=F32)
    cos_t = cos_ref[...] * scale
    sin_t = sin_ref[...] * scale
    half = LANES // 2
    for hd in range(MLA_HEADS):
        base = hd * HEAD_PAD
        o_ref[base:base + QK_NOPE, :] = (acc[base:base + QK_NOPE, :] * scale).astype(o_ref.dtype)
        r = acc[base + QK_NOPE:base + HEAD_PAD, :]
        rot = jnp.concatenate([r[half:, :], r[:half, :]], axis=0)
        o_ref[base + QK_NOPE:base + HEAD_PAD, :] = (r * cos_t + rot * sin_t).astype(o_ref.dtype)


def _q_proj(c_q, w_qt, cos_tt, sin_tt, scale):
    n, r = c_q.shape
    qw = w_qt.shape[0]
    tm = _tile(n, 512)
    tab = pl.BlockSpec((LANES, tm), lambda i: (0, i))
    return pl.pallas_call(
        functools.partial(_q_proj_kernel, scale),
        grid=(n // tm,),
        in_specs=[pl.BlockSpec((tm, r), lambda i: (i, 0)), pl.BlockSpec((qw, r), lambda i: (0, 0)), tab, tab],
        out_specs=pl.BlockSpec((qw, tm), lambda i: (0, i)),
        out_shape=jax.ShapeDtypeStruct((qw, n), BF16),
        compiler_params=_params("parallel"),
        name="mla_q_proj",
    )(c_q, w_qt, cos_tt, sin_tt)


def _kv_kernel(c_ref, wk_ref, wvt_ref, kr_ref, k_ref, vt_ref):
    c = c_ref[...]
    kn = jnp.dot(c, wk_ref[...], preferred_element_type=F32).astype(k_ref.dtype)
    kr = kr_ref[...]
    for hd in range(MLA_HEADS):
        k_ref[:, hd * HEAD_PAD:hd * HEAD_PAD + QK_NOPE] = kn[:, hd * QK_NOPE:(hd + 1) * QK_NOPE]
        k_ref[:, hd * HEAD_PAD + QK_NOPE:(hd + 1) * HEAD_PAD] = kr
    vt = lax.dot_general(wvt_ref[...], c, (((1,), (1,)), ((), ())),
                         preferred_element_type=F32).astype(vt_ref.dtype)
    pad_rows = V_ROWS - V_HEAD
    ones_row = (lax.broadcasted_iota(jnp.int32, (pad_rows, vt.shape[1]), 0) == 0).astype(vt_ref.dtype)
    for hd in range(MLA_HEADS):
        vt_ref[hd * V_ROWS:hd * V_ROWS + V_HEAD, :] = vt[hd * V_HEAD:(hd + 1) * V_HEAD, :]
        vt_ref[hd * V_ROWS + V_HEAD:(hd + 1) * V_ROWS, :] = ones_row


def _kv_proj(c_kv, w_uk, w_uvt, k_rope):
    n, r = c_kv.shape
    tm = _tile(n, 512)
    kw = MLA_HEADS * HEAD_PAD
    vw = MLA_HEADS * V_ROWS
    return pl.pallas_call(
        _kv_kernel,
        grid=(n // tm,),
        in_specs=[pl.BlockSpec((tm, r), lambda i: (i, 0)),
                  pl.BlockSpec(w_uk.shape, lambda i: (0, 0)),
                  pl.BlockSpec(w_uvt.shape, lambda i: (0, 0)),
                  pl.BlockSpec((tm, LANES), lambda i: (i, 0))],
        out_specs=[pl.BlockSpec((tm, kw), lambda i: (i, 0)), pl.BlockSpec((vw, tm), lambda i: (0, i))],
        out_shape=[jax.ShapeDtypeStruct((n, kw), BF16), jax.ShapeDtypeStruct((vw, n), BF16)],
        compiler_params=_params("parallel"),
        name="mla_kv_proj",
    )(c_kv, w_uk, w_uvt, k_rope)


def _attn_kernel(tk, p_dtype, qt_ref, k_ref, vt_ref, o_ref, s_a, s_b, p_a, p_b):
    tq = qt_ref.shape[1]
    n_chunks = k_ref.shape[0] // tk

    def scores(i, s_out):
        off = pl.multiple_of(i * tk, tk)
        s = jnp.dot(k_ref[pl.ds(off, tk), :], qt_ref[...], preferred_element_type=F32)
        s_out[...] = s
        return jnp.max(s, axis=0, keepdims=True)

    def softmax(s_in, p_out, mx, m):
        m_new = jnp.maximum(m, mx)
        x = s_in[...] - m_new
        p_out[...] = jnp.exp2(x.astype(p_dtype)).astype(p_out.dtype)
        return m_new, jnp.exp2(m - m_new)

    def values(i, p_in, alpha, acc):
        off = pl.multiple_of(i * tk, tk)
        return alpha * acc + jnp.dot(vt_ref[:, pl.ds(off, tk)], p_in[...], preferred_element_type=F32)

    def step(i, odd, carry, first=False, last=False):
        s_cur, s_nxt, p_cur, p_prv = (s_b, s_a, p_b, p_a) if odd else (s_a, s_b, p_a, p_b)
        mx, m, alpha, acc = carry
        if not first:
            acc = values(i - 1, p_prv, alpha, acc)
        mx_nxt = mx if last else scores(i + 1, s_nxt)
        m, alpha = softmax(s_cur, p_cur, mx, m)
        return mx_nxt, m, alpha, acc

    carry = (scores(0, s_a), jnp.full((1, tq), -jnp.inf, F32), jnp.ones((1, tq), F32),
             jnp.zeros((vt_ref.shape[0], tq), F32))
    for i in range(n_chunks):
        carry = step(i, i % 2 == 1, carry, first=i == 0, last=i == n_chunks - 1)
    _, _, alpha, acc = carry
    acc = values(n_chunks - 1, p_b if n_chunks % 2 == 0 else p_a, alpha, acc)
    o_ref[...] = jnp.transpose(acc[:V_HEAD, :] / acc[V_HEAD:V_HEAD + 1, :]).astype(o_ref.dtype)


def _attention(qt, k, vt, batch, seq, p_dtype):
    n = k.shape[0]
    tq = _tile(seq, ATTN_TQ)
    tk = _tile(seq // 2, ATTN_TK)
    assert (seq // tk) % 2 == 0
    nq = seq // tq
    return pl.pallas_call(
        functools.partial(_attn_kernel, tk, p_dtype),
        grid=(batch, MLA_HEADS, nq),
        scratch_shapes=[pltpu.VMEM((tk, tq), F32), pltpu.VMEM((tk, tq), F32),
                        pltpu.VMEM((tk, tq), BF16), pltpu.VMEM((tk, tq), BF16)],
        in_specs=[pl.BlockSpec((HEAD_PAD, tq), lambda b, h, i: (h, b * nq + i)),
                  pl.BlockSpec((seq, HEAD_PAD), lambda b, h, i: (b, h)),
                  pl.BlockSpec((V_ROWS, seq), lambda b, h, i: (h, b))],
        out_specs=pl.BlockSpec((tq, V_HEAD), lambda b, h, i: (b * nq + i, h)),
        out_shape=jax.ShapeDtypeStruct((n, MLA_HEADS * V_HEAD), BF16),
        compiler_params=_params("parallel", "parallel", "arbitrary"),
        name="mla_attention",
    )(qt, k, vt)


def _gmlp_kernel(u_ref, v_ref, ws_ref, bs_ref, o_ref):
    n_chunks = u_ref.shape[0] // CHUNK
    for c in range(n_chunks):
        rows = slice(c * CHUNK, (c + 1) * CHUNK)
        for g in range(GM_GROUPS):
            cols = slice(g * GM_GROUP_W, (g + 1) * GM_GROUP_W)
            mixed = jnp.dot(ws_ref[g], v_ref[rows, cols], preferred_element_type=F32)
            o_ref[rows, cols] = (u_ref[rows, cols].astype(F32) * (mixed + bs_ref[:, cols])).astype(o_ref.dtype)


def _gmlp_mix(u, v, ws, bs_tile):
    n, w = u.shape
    tm = _tile(n, 4 * CHUNK)
    blk = pl.BlockSpec((tm, w), lambda i: (i, 0))
    return pl.pallas_call(
        _gmlp_kernel,
        grid=(n // tm,),
        in_specs=[blk, blk, pl.BlockSpec(ws.shape, lambda i: (0, 0, 0)),
                  pl.BlockSpec(bs_tile.shape, lambda i: (0, 0))],
        out_specs=blk,
        out_shape=jax.ShapeDtypeStruct((n, w), BF16),
        compiler_params=_params("parallel"),
        name="gmlp_spatial",
    )(u, v, ws, bs_tile)


def _merge_kernel(a_ref, wa_ref, b_ref, wb_ref, ga_ref, gb_ref, o_ref):
    ya = jnp.dot(a_ref[...], wa_ref[...], preferred_element_type=F32)
    yb = jnp.dot(b_ref[...], wb_ref[...], preferred_element_type=F32)
    o_ref[...] = (ga_ref[...].astype(F32) * ya + gb_ref[...].astype(F32) * yb).astype(o_ref.dtype)


def _branch_merge(a, wa, b, wb, gates):
    n, ka = a.shape
    kb = b.shape[1]
    d = wa.shape[1]
    tm = _tile(n, 1024)
    tn = _tile(d, 1024)
    nj = d // tn
    return pl.pallas_call(
        _merge_kernel,
        grid=(nj, n // tm),
        in_specs=[pl.BlockSpec((tm, ka), lambda j, i: (i, 0)), pl.BlockSpec((ka, tn), lambda j, i: (0, j)),
                  pl.BlockSpec((tm, kb), lambda j, i: (i, 0)), pl.BlockSpec((kb, tn), lambda j, i: (0, j)),
                  pl.BlockSpec((tm, tn), lambda j, i: (i, j)),
                  pl.BlockSpec((tm, tn), lambda j, i: (i, j + nj))],
        out_specs=pl.BlockSpec((tm, tn), lambda j, i: (i, j)),
        out_shape=jax.ShapeDtypeStruct((n, d), BF16),
        compiler_params=_params("parallel", "parallel"),
        name="branch_merge",
    )(a, wa, b, wb, gates, gates)


def _epi_norm_residual(acc, h_ref, g_ref):
    return h_ref[...] + _rms(acc, g_ref[...])


def _swiglu_step(x, wg_ref, wu_ref, wd_ref):
    g = jnp.dot(x, wg_ref[...], preferred_element_type=F32)
    u = jnp.dot(x, wu_ref[...], preferred_element_type=F32)
    a = (jax.nn.silu(g) * u).astype(x.dtype)
    return jnp.dot(a, wd_ref[...], preferred_element_type=F32)


def _dense_ffn_kernel(x_ref, wg_ref, wu_ref, wd_ref, o_ref):
    f = pl.program_id(1)
    @pl.when(f == 0)
    def _():
        o_ref[...] = jnp.zeros_like(o_ref)

    o_ref[...] += _swiglu_step(x_ref[...], wg_ref, wu_ref, wd_ref)


def _dense_ffn(x, w_gu, w_down):
    n, d = x.shape
    dff = w_down.shape[0]
    tm = _tile(n, 1024)
    tf = _tile(dff, 512)
    nf = dff // tf
    return pl.pallas_call(
        _dense_ffn_kernel,
        grid=(n // tm, nf),
        in_specs=[pl.BlockSpec((tm, d), lambda i, f: (i, 0)),
                  pl.BlockSpec((d, tf), lambda i, f: (0, f)),
                  pl.BlockSpec((d, tf), lambda i, f: (0, f + nf)),
                  pl.BlockSpec((tf, d), lambda i, f: (f, 0))],
        out_specs=pl.BlockSpec((tm, d), lambda i, f: (i, 0)),
        out_shape=jax.ShapeDtypeStruct((n, d), F32),
        compiler_params=_params("parallel", "arbitrary"),
        name="dense_swiglu",
    )(x, w_gu, w_gu, w_down)


def _moe_ffn_kernel(te_ref, th_ref, nu_ref, x_ref, wg_ref, wu_ref, wd_ref, o_ref):
    t = pl.program_id(0)
    f = pl.program_id(1)
    n_half = th_ref[t]

    @pl.when(f == 0)
    def _():
        o_ref[...] = jnp.zeros_like(o_ref)

    @pl.when(n_half > 0)
    def _():
        wg = wg_ref[...].astype(BF16)
        wu = wu_ref[...].astype(BF16)
        wd = wd_ref[...].astype(BF16)

        def half(rows):
            x = _unpack_bf16_pairs(x_ref[rows, :])
            g = jnp.dot(x, wg, preferred_element_type=F32)
            u = jnp.dot(x, wu, preferred_element_type=F32)
            y = jnp.dot((jax.nn.silu(g) * u).astype(BF16), wd, preferred_element_type=F32)
            o_ref[rows, :] += y

        half(slice(0, MOE_BLOCK))

        @pl.when(n_half > 1)
        def _():
            half(slice(MOE_BLOCK, MOE_TILE))


def _moe_ffn(xs, tile_e, tile_halves, n_used, w_gu, w_down):
    n_slots = xs.shape[0]
    d = w_down.shape[2]
    dff = w_down.shape[1]
    tm = MOE_TILE
    tf = _tile(dff, 512)
    nf = dff // tf

    def _tt(t, nu):
        return jnp.minimum(t, jnp.maximum(nu[0] - 1, 0))

    def _e(t, te, nu):
        return te[_tt(t, nu)]

    def _ff(t, f, nu):
        return jnp.where(t < nu[0], f, nf - 1)

    grid_spec = pltpu.PrefetchScalarGridSpec(
        num_scalar_prefetch=3,
        grid=(n_slots // tm, nf),
        in_specs=[
            pl.BlockSpec((tm, d // 2), lambda t, f, te, th, nu: (_tt(t, nu), 0)),
            pl.BlockSpec((None, d, tf), lambda t, f, te, th, nu: (_e(t, te, nu), 0, _ff(t, f, nu))),
            pl.BlockSpec((None, d, tf), lambda t, f, te, th, nu: (_e(t, te, nu), 0, _ff(t, f, nu) + nf)),
            pl.BlockSpec((None, tf, d), lambda t, f, te, th, nu: (_e(t, te, nu), _ff(t, f, nu), 0)),
        ],
        out_specs=pl.BlockSpec((tm, d), lambda t, f, te, th, nu: (t, 0), pipeline_mode=pl.Buffered(1)),
    )
    return pl.pallas_call(
        _moe_ffn_kernel,
        grid_spec=grid_spec,
        out_shape=jax.ShapeDtypeStruct((n_slots, d), F32),
        compiler_params=_params("arbitrary", "arbitrary"),
        name="moe_swiglu",
    )(tile_e, tile_halves, n_used, xs, w_gu, w_gu, w_down)


def _post_ple_math(h, f, gf_ref, p_ref, wp_ref, wg_ref, gp_ref, o_ref):
    h = h + _rms(f, gf_ref[...])
    e = jnp.dot(p_ref[...].astype(BF16), wp_ref[...], preferred_element_type=F32)
    gate = jax.nn.sigmoid(jnp.dot(h.astype(BF16), wg_ref[...], preferred_element_type=F32))
    o_ref[...] = h + _rms(gate * e, gp_ref[...])


def _ple_kernel(h_ref, f_ref, gf_ref, p_ref, wp_ref, wg_ref, gp_ref, o_ref):
    _post_ple_math(h_ref[...], f_ref[...], gf_ref, p_ref, wp_ref, wg_ref, gp_ref, o_ref)


def _ple_specs(n, d, pd, tm):
    blk = pl.BlockSpec((tm, d), lambda i: (i, 0))
    row = pl.BlockSpec((1, d), lambda i: (0, 0))
    tail = [row, pl.BlockSpec((tm, pd), lambda i: (i, 0)), pl.BlockSpec((pd, d), lambda i: (0, 0)),
            pl.BlockSpec((d, d), lambda i: (0, 0)), row]
    return blk, tail


def _ffn_post_ple(h, f, g_post, p, w_ple_in, w_ple_gate, g_ple):
    n, d = h.shape
    tm = _tile(n, 256)
    blk, tail = _ple_specs(n, d, p.shape[1], tm)
    return pl.pallas_call(
        _ple_kernel,
        grid=(n // tm,),
        in_specs=[blk, blk] + tail,
        out_specs=blk,
        out_shape=jax.ShapeDtypeStruct((n, d), F32),
        compiler_params=_params("parallel"),
        name="ffn_post_ple",
    )(h, f, g_post.reshape(1, d), p, w_ple_in, w_ple_gate, g_ple.reshape(1, d))


def _ple_combine_kernel(d_cur_ref, d_nxt_ref, h_ref, rw_ref, gf_ref, p_ref, wp_ref, wg_ref, gp_ref,
                        ys_ref, o_ref, gbuf, sem):
    i = pl.program_id(0)
    last = pl.num_programs(0) - 1
    slot = i % 2
    tm = h_ref.shape[0]

    def fetch(d_ref, s):
        def body(r, c):
            for k in range(TOP_K):
                src = d_ref[0, 0, TOP_K * r + k]
                pltpu.make_async_copy(ys_ref.at[pl.ds(src, 1)], gbuf.at[s, k, pl.ds(r, 1)], sem.at[s]).start()
            return c
        lax.fori_loop(0, tm, body, 0, unroll=8)

    @pl.when(i == 0)
    def _():
        fetch(d_cur_ref, 0)

    @pl.when(i < last)
    def _():
        fetch(d_nxt_ref, 1 - slot)

    _row_copy_wait(ys_ref.at[pl.ds(0, 1)], gbuf.at[slot, 0, pl.ds(0, 1)], sem.at[slot], TOP_K * tm)
    f = rw_ref[:, 0:1] * gbuf[slot, 0]
    for k in range(1, TOP_K):
        f = f + rw_ref[:, k:k + 1] * gbuf[slot, k]
    _post_ple_math(h_ref[...], f, gf_ref, p_ref, wp_ref, wg_ref, gp_ref, o_ref)


def _moe_combine_post_ple(h, ys, dest, route_w, g_post, p, w_ple_in, w_ple_gate, g_ple):
    n, d = h.shape
    tm = _tile(n, 256)
    nt = n // tm
    blk, tail = _ple_specs(n, d, p.shape[1], tm)
    dest3 = dest.reshape(nt, 1, TOP_K * tm)
    idx = functools.partial(pl.BlockSpec, (1, 1, TOP_K * tm), memory_space=pltpu.SMEM)
    return pl.pallas_call(
        _ple_combine_kernel,
        grid=(nt,),
        in_specs=[idx(lambda i: (i, 0, 0)), idx(lambda i: (jnp.minimum(i + 1, nt - 1), 0, 0)),
                  blk, pl.BlockSpec((tm, LANES), lambda i: (i, 0))] + tail
                 + [pl.BlockSpec(memory_space=pl.ANY)],
        out_specs=blk,
        out_shape=jax.ShapeDtypeStruct((n, d), F32),
        scratch_shapes=[pltpu.VMEM((2, TOP_K, tm, d), F32), pltpu.SemaphoreType.DMA((2,))],
        compiler_params=_params("arbitrary"),
        name="moe_combine_post_ple",
    )(dest3, dest3, h, route_w, g_post.reshape(1, d), p, w_ple_in, w_ple_gate, g_ple.reshape(1, d), ys)


def _slot_tables(route_i, counts_f, n_tok):
    a = n_tok * TOP_K
    counts = counts_f[0, :N_EXPERTS].astype(jnp.int32)
    padded = (counts + MOE_TILE - 1) // MOE_TILE * MOE_TILE
    pad_end = jnp.cumsum(padded)
    pad_start = pad_end - padded
    experts = route_i[:, ROUTE_E:ROUTE_E + TOP_K]
    dest = pad_start[experts] + route_i[:, ROUTE_RANK:ROUTE_RANK + TOP_K]
    n_tiles = -(-a // MOE_TILE) + N_EXPERTS
    tile_start = jnp.arange(n_tiles, dtype=jnp.int32) * MOE_TILE
    tile_e = jnp.minimum(jnp.searchsorted(pad_end, tile_start, side='right'), N_EXPERTS - 1).astype(jnp.int32)
    rows = jnp.clip(pad_start[tile_e] + counts[tile_e] - tile_start, 0, MOE_TILE)
    tile_halves = ((rows + MOE_BLOCK - 1) // MOE_BLOCK).astype(jnp.int32)
    n_used = (pad_end[-1] // MOE_TILE).astype(jnp.int32).reshape(1)
    return dest.astype(jnp.int32), tile_e, tile_halves, n_used, n_tiles * MOE_TILE


def _rope_slot_cols(w):
    half = QK_ROPE // 2
    z = jnp.zeros(w.shape[:-1] + (half,), w.dtype)
    return jnp.concatenate([w[..., :half], z, w[..., half:], z], axis=-1)


def _prep_layer(i, w_in, gm_ws, gm_bs, w_uq, w_ukv, w_branch_a, w_branch_b, w_out, w_ple_in, w_ple_gate):
    d = w_in.shape[1]
    wi = w_in[i]
    o = 0
    w_u = wi[:, o:o + GM_W]; o += GM_W
    w_v = wi[:, o:o + GM_W]; o += GM_W
    w_cq = wi[:, o:o + Q_LORA]; o += Q_LORA
    w_ckv = wi[:, o:o + KV_LORA]; o += KV_LORA
    w_kr = wi[:, o:o + QK_ROPE]; o += QK_ROPE
    w_g = wi[:, o:o + 2 * d]
    w_lat = jnp.concatenate([w_cq, w_ckv, _rope_slot_cols(w_kr)], axis=1)
    wq = w_uq[i].reshape(Q_LORA, MLA_HEADS, QK_NOPE + QK_ROPE)
    wq = jnp.concatenate([wq[..., :QK_NOPE], _rope_slot_cols(wq[..., QK_NOPE:])], axis=-1)
    wkv = w_ukv[i].reshape(KV_LORA, MLA_HEADS, QK_NOPE + V_HEAD)
    bs_tile = jnp.repeat(gm_bs[i].T, GM_GROUP_W, axis=1)
    return dict(
        w_u=w_u.astype(BF16), w_v=w_v.astype(BF16), w_lat=w_lat.astype(BF16), w_g=w_g.astype(BF16),
        w_qt=wq.reshape(Q_LORA, MLA_HEADS * HEAD_PAD).T.astype(BF16),
        w_uk=wkv[..., :QK_NOPE].reshape(KV_LORA, MLA_HEADS * QK_NOPE).astype(BF16),
        w_uvt=wkv[..., QK_NOPE:].reshape(KV_LORA, MLA_HEADS * V_HEAD).T.astype(BF16),
        ws=gm_ws[i].astype(BF16), bs_tile=bs_tile,
        w_a=w_branch_a[i].astype(BF16), w_b=w_branch_b[i].astype(BF16), w_o=w_out[i].astype(BF16),
        w_pi=w_ple_in[i].astype(BF16), w_pg=w_ple_gate[i].astype(BF16),
    )


def kernel(x, p, positions, norm_pre_mix, norm_post_mix, norm_pre_ffn, norm_post_ffn, w_in, gm_ln_g, gm_ln_b, gm_ws, gm_bs, mla_q_norm, w_uq, mla_kv_norm, w_ukv, w_branch_a, w_branch_b, w_out, w_dense_gu, w_dense_down, w_router, w_exp_gu, w_exp_down, w_ple_in, w_ple_gate, ple_norm):
    batch, seq, d = x.shape
    depth = p.shape[0]
    n = batch * seq
    scale = (QK_NOPE + QK_ROPE) ** -0.5 * LOG2_E
    cos_t, sin_t, cos_tt, sin_tt = _rope_tables(positions)
    h = x.reshape(n, d)

    for li in range(depth):
        wts = _prep_layer(li, w_in, gm_ws, gm_bs, w_uq, w_ukv, w_branch_a, w_branch_b, w_out,
                          w_ple_in, w_ple_gate)
        hn = _norm_bf16(h, norm_pre_mix[li])
        (u,) = _mm(hn, wts["w_u"], _epi_gelu, tm=1024, tn=GM_W, outs=[(GM_W, BF16, "tile")],
                   name="in_proj_u")
        (v,) = _mm(hn, wts["w_v"], _epi_gelu_group_ln, tm=512, tn=GM_W,
                   extras=[(gm_ln_g[li].reshape(1, GM_W), "vec"), (gm_ln_b[li].reshape(1, GM_W), "vec")],
                   outs=[(GM_W, BF16, "tile")], name="in_proj_v")
        c_q, c_kv, k_r = _mm(
            hn, wts["w_lat"], _epi_latent, tm=512, tn=wts["w_lat"].shape[1],
            extras=[(mla_q_norm[li].reshape(1, Q_LORA), "vec"), (mla_kv_norm[li].reshape(1, KV_LORA), "vec"),
                    (cos_t, "rows"), (sin_t, "rows")],
            outs=[(Q_LORA, BF16, "rows"), (KV_LORA, BF16, "rows"), (LANES, BF16, "rows")],
            name="in_proj_latent")
        (gates,) = _mm(hn, wts["w_g"], _epi_sigmoid, tm=1024, tn=1024, outs=[(2 * d, BF16, "tile")],
                       name="in_proj_gates")
        qt = _q_proj(c_q, wts["w_qt"], cos_tt, sin_tt, scale)
        k, vt = _kv_proj(c_kv, wts["w_uk"], wts["w_uvt"], k_r)
        attn = _attention(qt, k, vt, batch, seq, F32)
        gm = _gmlp_mix(u, v, wts["ws"], wts["bs_tile"])
        merged = _branch_merge(gm, wts["w_a"], attn, wts["w_b"], gates)
        (h,) = _mm(merged, wts["w_o"], _epi_norm_residual, tm=512, tn=d,
                   extras=[(h, "rows"), (norm_post_mix[li].reshape(1, d), "vec")],
                   outs=[(d, F32, "tile")], name="out_proj")
        p_li = p[li].reshape(n, -1)
        if li % 2 == 0:
            hn = _norm_bf16(h, norm_pre_ffn[li])
            f = _dense_ffn(hn, w_dense_gu[li // 2].astype(BF16), w_dense_down[li // 2].astype(BF16))
            h = _ffn_post_ple(h, f, norm_post_ffn[li], p_li, wts["w_pi"], wts["w_pg"], ple_norm[li])
        else:
            route_i, route_w, counts = _router(h, norm_pre_ffn[li], w_router[li // 2])
            dest, tile_e, tile_halves, n_used, n_slots = _slot_tables(route_i, counts, n)
            xs = _dispatch(h, norm_pre_ffn[li], dest, n_slots)
            ys = _moe_ffn(xs, tile_e, tile_halves, n_used, w_exp_gu[li // 2], w_exp_down[li // 2])
            h = _moe_combine_post_ple(h, ys, dest, route_w, norm_post_ffn[li], p_li, wts["w_pi"],
                                      wts["w_pg"], ple_norm[li])
    return h.reshape(batch, seq, d)
```

```python
import functools

import jax
import jax.numpy as jnp
from jax import lax
from jax.experimental import pallas as pl
from jax.experimental.pallas import tpu as pltpu

F32 = jnp.float32
BF16 = jnp.bfloat16

GM_GROUPS = 8
GM_GROUP_W = 128
GM_W = GM_GROUPS * GM_GROUP_W
CHUNK = 128
MLA_HEADS = 8
Q_LORA = 512
KV_LORA = 256
QK_NOPE = 128
QK_ROPE = 64
V_HEAD = 128
ROPE_THETA = 10000.0
N_EXPERTS = 8
TOP_K = 2
MOE_BLOCK = 512
MOE_TILE = 2 * MOE_BLOCK
MOE_GROUP = MOE_TILE // 8
EPS = 1e-6
LOG2_E = 1.4426950408889634

LANES = 128
HEAD_PAD = 2 * LANES
BF16_SUBLANES = 16
V_ROWS = V_HEAD + BF16_SUBLANES
ATTN_TQ = 2048
ATTN_TK = 512
V7X_VMEM_BYTES = 64 * 1024 * 1024
VMEM_LIMIT = V7X_VMEM_BYTES - 8 * 1024 * 1024


def _params(*sem):
    return pltpu.CompilerParams(dimension_semantics=sem, vmem_limit_bytes=VMEM_LIMIT)


def _tile(n, want):
    t = min(n, want)
    while n % t:
        t //= 2
    return t


def _rms(x, g):
    return x * lax.rsqrt(jnp.mean(x * x, axis=-1, keepdims=True) + EPS) * g


def _rope_table_kernel(pos_c_ref, pos_r_ref, inv_r_ref, sgn_r_ref, inv_c_ref, sgn_c_ref,
                       cos_ref, sin_ref, cos_t_ref, sin_t_ref):
    ang = pos_c_ref[...].astype(F32) * inv_r_ref[...]
    cos_ref[...] = jnp.cos(ang) * jnp.abs(sgn_r_ref[...])
    sin_ref[...] = jnp.sin(ang) * sgn_r_ref[...]
    ang_t = inv_c_ref[...] * pos_r_ref[...].astype(F32)
    cos_t_ref[...] = jnp.cos(ang_t) * jnp.abs(sgn_c_ref[...])
    sin_t_ref[...] = jnp.sin(ang_t) * sgn_c_ref[...]


def _rope_tables(positions):
    n = positions.size
    tm = _tile(n, 1024)
    half = QK_ROPE // 2
    inv = ROPE_THETA ** (-jnp.arange(0, QK_ROPE, 2, dtype=F32) / QK_ROPE)
    z = jnp.zeros((half,), F32)
    inv_slot = jnp.concatenate([inv, z, inv, z])
    one = jnp.ones((half,), F32)
    sgn_slot = jnp.concatenate([-one, z, one, z])
    row = pl.BlockSpec((1, LANES), lambda i: (0, 0))
    col = pl.BlockSpec((LANES, 1), lambda i: (0, 0))
    tab = pl.BlockSpec((tm, LANES), lambda i: (i, 0))
    tab_t = pl.BlockSpec((LANES, tm), lambda i: (0, i))
    return pl.pallas_call(
        _rope_table_kernel,
        grid=(n // tm,),
        in_specs=[pl.BlockSpec((tm, 1), lambda i: (i, 0)), pl.BlockSpec((1, tm), lambda i: (0, i)),
                  row, row, col, col],
        out_specs=[tab, tab, tab_t, tab_t],
        out_shape=[jax.ShapeDtypeStruct((n, LANES), F32)] * 2 + [jax.ShapeDtypeStruct((LANES, n), F32)] * 2,
        compiler_params=_params("parallel"),
        name="rope_tables",
    )(positions.reshape(n, 1), positions.reshape(1, n), inv_slot.reshape(1, LANES),
      sgn_slot.reshape(1, LANES), inv_slot.reshape(LANES, 1), sgn_slot.reshape(LANES, 1))


def _rope_slot(x, cos_t, sin_t):
    return x * cos_t + pltpu.roll(x, LANES // 2, axis=1) * sin_t


def _norm_kernel(h_ref, g_ref, o_ref):
    o_ref[...] = _rms(h_ref[...], g_ref[...]).astype(o_ref.dtype)


def _norm_bf16(h, g):
    n, d = h.shape
    tm = _tile(n, 512)
    return pl.pallas_call(
        _norm_kernel,
        grid=(n // tm,),
        in_specs=[pl.BlockSpec((tm, d), lambda i: (i, 0)), pl.BlockSpec((1, d), lambda i: (0, 0))],
        out_specs=pl.BlockSpec((tm, d), lambda i: (i, 0)),
        out_shape=jax.ShapeDtypeStruct((n, d), BF16),
        compiler_params=_params("parallel"),
        name="rmsnorm",
    )(h, g.reshape(1, d))


ROUTE_E, ROUTE_RANK = 0, TOP_K


def _router_kernel(h_ref, g_ref, wr_ref, ri_ref, rw_ref, cnt_ref, carry):
    @pl.when(pl.program_id(0) == 0)
    def _():
        carry[...] = jnp.zeros_like(carry)

    y = _rms(h_ref[...], g_ref[...])
    logits = jnp.dot(y, wr_ref[...], preferred_element_type=F32, precision=lax.Precision.HIGHEST)
    tm = logits.shape[0]
    lane = lax.broadcasted_iota(jnp.int32, logits.shape, 1)
    lane_f = lane.astype(F32)
    lg = jnp.where(lane < N_EXPERTS, logits, -jnp.inf)
    tops, hots = [], []
    for _ in range(TOP_K):
        mx = jnp.max(lg, axis=1, keepdims=True)
        idx = jnp.min(jnp.where(lg == mx, lane_f, float(LANES)), axis=1, keepdims=True)
        hot = lane_f == idx
        tops.append((mx, idx))
        hots.append(hot)
        lg = jnp.where(hot, -jnp.inf, lg)
    e1 = jnp.exp(tops[1][0] - tops[0][0])
    gates = (1.0 / (1.0 + e1), e1 / (1.0 + e1))
    hot_any = hots[0].astype(F32) + hots[1].astype(F32)
    tri = (lax.broadcasted_iota(jnp.int32, (tm, tm), 0) > lax.broadcasted_iota(jnp.int32, (tm, tm), 1))
    before = jnp.dot(tri.astype(BF16), hot_any.astype(BF16), preferred_element_type=F32) + carry[...]
    ranks = [jnp.sum(jnp.where(hot, before, 0.0), axis=1, keepdims=True) for hot in hots]
    carry[...] += jnp.sum(hot_any, axis=0, keepdims=True)
    cnt_ref[...] = carry[...]
    rec = jnp.zeros(logits.shape, jnp.int32)
    wrec = jnp.zeros(logits.shape, F32)
    for k in range(TOP_K):
        rec = jnp.where(lane == ROUTE_E + k, tops[k][1].astype(jnp.int32), rec)
        rec = jnp.where(lane == ROUTE_RANK + k, ranks[k].astype(jnp.int32), rec)
        wrec = jnp.where(lane == k, gates[k], wrec)
    ri_ref[...] = rec
    rw_ref[...] = wrec


def _router(h, g, w_router):
    n, d = h.shape
    tm = _tile(n, 512)
    wr = jnp.zeros((d, LANES), F32).at[:, :N_EXPERTS].set(w_router)
    rec = pl.BlockSpec((tm, LANES), lambda i: (i, 0))
    return pl.pallas_call(
        _router_kernel,
        grid=(n // tm,),
        in_specs=[pl.BlockSpec((tm, d), lambda i: (i, 0)), pl.BlockSpec((1, d), lambda i: (0, 0)),
                  pl.BlockSpec((d, LANES), lambda i: (0, 0))],
        out_specs=[rec, rec, pl.BlockSpec((1, LANES), lambda i: (0, 0))],
        out_shape=[jax.ShapeDtypeStruct((n, LANES), jnp.int32), jax.ShapeDtypeStruct((n, LANES), F32),
                   jax.ShapeDtypeStruct((1, LANES), F32)],
        scratch_shapes=[pltpu.VMEM((1, LANES), F32)],
        compiler_params=_params("arbitrary"),
        name="moe_router",
    )(h, g.reshape(1, d), wr)


def _pack_bf16_pairs(y):
    half = y.shape[1] // 2
    lo = lax.bitcast_convert_type(y[:, :half].astype(BF16).astype(F32), jnp.uint32)
    hi = lax.bitcast_convert_type(y[:, half:].astype(BF16).astype(F32), jnp.uint32)
    return hi | (lo >> 16)


def _unpack_bf16_pairs(w):
    lo = lax.bitcast_convert_type(w << 16, F32).astype(BF16)
    hi = lax.bitcast_convert_type(w & jnp.uint32(0xFFFF0000), F32).astype(BF16)
    return jnp.concatenate([lo, hi], axis=1)


def _row_copy_wait(src_row, dst_row, sem, count):
    def body(_, c):
        pltpu.make_async_copy(src_row, dst_row, sem).wait()
        return c
    lax.fori_loop(0, count, body, 0, unroll=16)


def _dispatch_kernel(dest_ref, h_ref, g_ref, xs_zero_ref, xs_ref, pbuf, sem):
    del xs_zero_ref
    i = pl.program_id(0)
    last = pl.num_programs(0) - 1
    slot = i % 2
    tm = h_ref.shape[0]

    def wait_slot(s):
        _row_copy_wait(pbuf.at[s, pl.ds(0, 1)], xs_ref.at[pl.ds(0, 1)], sem.at[s], TOP_K * tm)

    @pl.when(i >= 2)
    def _():
        wait_slot(slot)

    pbuf[slot] = _pack_bf16_pairs(_rms(h_ref[...], g_ref[...]))

    def send(r, c):
        for k in range(TOP_K):
            dst = dest_ref[0, 0, TOP_K * r + k]
            pltpu.make_async_copy(pbuf.at[slot, pl.ds(r, 1)], xs_ref.at[pl.ds(dst, 1)], sem.at[slot]).start()
        return c
    lax.fori_loop(0, tm, send, 0, unroll=8)

    @pl.when(i == last)
    def _():
        @pl.when(i >= 1)
        def _():
            wait_slot(1 - slot)
        wait_slot(slot)


def _dispatch(h, g, dest, n_slots):
    n, d = h.shape
    tm = _tile(n, 512)
    nt = n // tm
    xs0 = jnp.zeros((n_slots, d // 2), jnp.uint32)
    return pl.pallas_call(
        _dispatch_kernel,
        grid=(nt,),
        in_specs=[pl.BlockSpec((1, 1, TOP_K * tm), lambda i: (i, 0, 0), memory_space=pltpu.SMEM),
                  pl.BlockSpec((tm, d), lambda i: (i, 0)), pl.BlockSpec((1, d), lambda i: (0, 0)),
                  pl.BlockSpec(memory_space=pl.ANY)],
        out_specs=pl.BlockSpec(memory_space=pl.ANY),
        out_shape=jax.ShapeDtypeStruct((n_slots, d // 2), jnp.uint32),
        scratch_shapes=[pltpu.VMEM((2, tm, d // 2), jnp.uint32), pltpu.SemaphoreType.DMA((2,))],
        input_output_aliases={3: 0},
        compiler_params=_params("arbitrary"),
        name="moe_dispatch",
    )(dest.reshape(nt, 1, TOP_K * tm), h, g.reshape(1, d), xs0)


def _mm_kernel(epi, n_extra, x_ref, w_ref, *rest):
    extras = rest[:n_extra]
    outs = rest[n_extra:]
    acc = jnp.dot(x_ref[...], w_ref[...], preferred_element_type=F32)
    res = epi(acc, *extras)
    if not isinstance(res, (tuple, list)):
        res = (res,)
    for o, r in zip(outs, res):
        o[...] = r.astype(o.dtype)


def _mm(x, w, epi, *, tm, tn, extras=(), outs, name):
    m, k = x.shape
    n = w.shape[1]
    tm = _tile(m, tm)
    tn = _tile(n, tn)

    def spec(ncols, kind):
        if kind == "tile":
            return pl.BlockSpec((tm, tn), lambda j, i: (i, j))
        if kind == "rows":
            return pl.BlockSpec((tm, ncols), lambda j, i: (i, 0))
        assert kind == "vec", kind
        return pl.BlockSpec((1, ncols), lambda j, i: (0, 0))

    in_specs = [pl.BlockSpec((tm, k), lambda j, i: (i, 0)), pl.BlockSpec((k, tn), lambda j, i: (0, j))]
    in_specs += [spec(a.shape[1], kind) for a, kind in extras]
    return pl.pallas_call(
        functools.partial(_mm_kernel, epi, len(extras)),
        grid=(n // tn, m // tm),
        in_specs=in_specs,
        out_specs=[spec(nc, kind) for nc, _, kind in outs],
        out_shape=[jax.ShapeDtypeStruct((m, nc), dt) for nc, dt, _ in outs],
        compiler_params=_params("parallel", "parallel"),
        name=name,
    )(x, w, *[a for a, _ in extras])


def _epi_gelu(acc):
    return jax.nn.gelu(acc)


def _epi_gelu_group_ln(acc, g_ref, b_ref):
    y = jax.nn.gelu(acc)
    g = g_ref[...]
    b = b_ref[...]
    cols = []
    for gi in range(acc.shape[1] // GM_GROUP_W):
        sl = slice(gi * GM_GROUP_W, (gi + 1) * GM_GROUP_W)
        yg = y[:, sl]
        mu = jnp.mean(yg, axis=-1, keepdims=True)
        dlt = yg - mu
        var = jnp.mean(dlt * dlt, axis=-1, keepdims=True)
        cols.append(dlt * lax.rsqrt(var + EPS) * g[:, sl] + b[:, sl])
    return jnp.concatenate(cols, axis=1)


def _epi_sigmoid(acc):
    return jax.nn.sigmoid(acc)


def _epi_latent(acc, qn_ref, kvn_ref, cos_ref, sin_ref):
    c_q = _rms(acc[:, :Q_LORA], qn_ref[...])
    c_kv = _rms(acc[:, Q_LORA:Q_LORA + KV_LORA], kvn_ref[...])
    k_r = _rope_slot(acc[:, Q_LORA + KV_LORA:], cos_ref[...], sin_ref[...])
    return c_q, c_kv, k_r


def _in_proj_kernel(h_ref, gn_ref, w_ref, lg_ref, lb_ref, qn_ref, kvn_ref, cos_ref, sin_ref,
                    hn_ref, u_ref, v_ref, cq_ref, ckv_ref, kr_ref):
    hn = _rms(h_ref[...], gn_ref[...]).astype(hn_ref.dtype)
    hn_ref[...] = hn
    acc = jnp.dot(hn, w_ref[...], preferred_element_type=F32)
    u_ref[...] = _epi_gelu(acc[:, :GM_W]).astype(u_ref.dtype)
    v_ref[...] = _epi_gelu_group_ln(acc[:, GM_W:2 * GM_W], lg_ref, lb_ref).astype(v_ref.dtype)
    c_q, c_kv, k_r = _epi_latent(acc[:, 2 * GM_W:], qn_ref, kvn_ref, cos_ref, sin_ref)
    cq_ref[...] = c_q.astype(cq_ref.dtype)
    ckv_ref[...] = c_kv.astype(ckv_ref.dtype)
    kr_ref[...] = k_r.astype(kr_ref.dtype)


def _in_proj(h, g_norm, w_uvl, ln_g, ln_b, q_norm, kv_norm, cos_t, sin_t):
    n, d = h.shape
    tm = _tile(n, 512)
    vec = lambda a: (a.reshape(1, -1), pl.BlockSpec((1, a.size), lambda i: (0, 0)))
    rows = lambda w: pl.BlockSpec((tm, w), lambda i: (i, 0))
    vecs = [vec(g_norm), vec(ln_g), vec(ln_b), vec(q_norm), vec(kv_norm)]
    widths = [d, GM_W, GM_W, Q_LORA, KV_LORA, LANES]
    return pl.pallas_call(
        _in_proj_kernel,
        grid=(n // tm,),
        in_specs=[rows(d), vecs[0][1],
                  pl.BlockSpec(w_uvl.shape, lambda i: (0, 0), pipeline_mode=pl.Buffered(1)),
                  vecs[1][1], vecs[2][1], vecs[3][1], vecs[4][1], rows(LANES), rows(LANES)],
        out_specs=[rows(w) for w in widths],
        out_shape=[jax.ShapeDtypeStruct((n, w), BF16) for w in widths],
        compiler_params=_params("parallel"),
        name="in_proj_uvl",
    )(h, vecs[0][0], w_uvl, vecs[1][0], vecs[2][0], vecs[3][0], vecs[4][0], cos_t, sin_t)


def _q_proj_kernel(scale, c_ref, w_ref, cos_ref, sin_ref, o_ref):
    acc = lax.dot_general(w_ref[...], c_ref[...], (((1,), (1,)), ((), ())),
                          preferred_element_type=F32)
    cos_t = cos_ref[...] * scale
    sin_t = sin_ref[...] * scale
    half = LANES // 2
    for hd in range(MLA_HEADS):
        base = hd * HEAD_PAD
        o_ref[base:base + QK_NOPE, :] = (acc[base:base + QK_NOPE, :] * scale).astype(o_ref.dtype)
        r = acc[base + QK_NOPE:base + HEAD_PAD, :]
        rot = jnp.concatenate([r[half:, :], r[:half, :]], axis=0)
        o_ref[base + QK_NOPE:base + HEAD_PAD, :] = (r * cos_t + rot * sin_t).astype(o_ref.dtype)


def _q_proj(c_q, w_qt, cos_tt, sin_tt, scale):
    n, r = c_q.shape
    qw = w_qt.shape[0]
    tm = _tile(n, 512)
    tab = pl.BlockSpec((LANES, tm), lambda i: (0, i))
    return pl.pallas_call(
        functools.partial(_q_proj_kernel, scale),
        grid=(n // tm,),
        in_specs=[pl.BlockSpec((tm, r), lambda i: (i, 0)), pl.BlockSpec((qw, r), lambda i: (0, 0)), tab, tab],
        out_specs=pl.BlockSpec((qw, tm), lambda i: (0, i)),
        out_shape=jax.ShapeDtypeStruct((qw, n), BF16),
        compiler_params=_params("parallel"),
        name="mla_q_proj",
    )(c_q, w_qt, cos_tt, sin_tt)


def _kv_kernel(c_ref, wk_ref, wvt_ref, kr_ref, k_ref, vt_ref):
    c = c_ref[...]
    kn = jnp.dot(c, wk_ref[...], preferred_element_type=F32).astype(k_ref.dtype)
    kr = kr_ref[...]
    for hd in range(MLA_HEADS):
        k_ref[:, hd * HEAD_PAD:hd * HEAD_PAD + QK_NOPE] = kn[:, hd * QK_NOPE:(hd + 1) * QK_NOPE]
        k_ref[:, hd * HEAD_PAD + QK_NOPE:(hd + 1) * HEAD_PAD] = kr
    vt = lax.dot_general(wvt_ref[...], c, (((1,), (1,)), ((), ())),
                         preferred_element_type=F32).astype(vt_ref.dtype)
    pad_rows = V_ROWS - V_HEAD
    ones_row = (lax.broadcasted_iota(jnp.int32, (pad_rows, vt.shape[1]), 0) == 0).astype(vt_ref.dtype)
    for hd in range(MLA_HEADS):
        vt_ref[hd * V_ROWS:hd * V_ROWS + V_HEAD, :] = vt[hd * V_HEAD:(hd + 1) * V_HEAD, :]
        vt_ref[hd * V_ROWS + V_HEAD:(hd + 1) * V_ROWS, :] = ones_row


def _kv_proj(c_kv, w_uk, w_uvt, k_rope):
    n, r = c_kv.shape
    tm = _tile(n, 512)
    kw = MLA_HEADS * HEAD_PAD
    vw = MLA_HEADS * V_ROWS
    return pl.pallas_call(
        _kv_kernel,
        grid=(n // tm,),
        in_specs=[pl.BlockSpec((tm, r), lambda i: (i, 0)),
                  pl.BlockSpec(w_uk.shape, lambda i: (0, 0)),
                  pl.BlockSpec(w_uvt.shape, lambda i: (0, 0)),
                  pl.BlockSpec((tm, LANES), lambda i: (i, 0))],
        out_specs=[pl.BlockSpec((tm, kw), lambda i: (i, 0)), pl.BlockSpec((vw, tm), lambda i: (0, i))],
        out_shape=[jax.ShapeDtypeStruct((n, kw), BF16), jax.ShapeDtypeStruct((vw, n), BF16)],
        compiler_params=_params("parallel"),
        name="mla_kv_proj",
    )(c_kv, w_uk, w_uvt, k_rope)


def _attn_kernel(tk, p_dtype, qt_ref, k_ref, vt_ref, o_ref, s_a, s_b, p_a, p_b):
    tq = qt_ref.shape[1]
    n_chunks = k_ref.shape[0] // tk

    def scores(i, s_out):
        off = pl.multiple_of(i * tk, tk)
        s = jnp.dot(k_ref[pl.ds(off, tk), :], qt_ref[...], preferred_element_type=F32)
        s_out[...] = s
        return jnp.max(s, axis=0, keepdims=True)

    def softmax(s_in, p_out, mx, m):
        m_new = jnp.maximum(m, mx)
        x = s_in[...] - m_new
        p_out[...] = jnp.exp2(x.astype(p_dtype)).astype(p_out.dtype)
        return m_new, jnp.exp2(m - m_new)

    def values(i, p_in, alpha, acc):
        off = pl.multiple_of(i * tk, tk)
        return alpha * acc + jnp.dot(vt_ref[:, pl.ds(off, tk)], p_in[...], preferred_element_type=F32)

    def step(i, odd, carry, first=False, last=False):
        s_cur, s_nxt, p_cur, p_prv = (s_b, s_a, p_b, p_a) if odd else (s_a, s_b, p_a, p_b)
        mx, m, alpha, acc = carry
        if not first:
            acc = values(i - 1, p_prv, alpha, acc)
        mx_nxt = mx if last else scores(i + 1, s_nxt)
        m, alpha = softmax(s_cur, p_cur, mx, m)
        return mx_nxt, m, alpha, acc

    carry = (scores(0, s_a), jnp.full((1, tq), -jnp.inf, F32), jnp.ones((1, tq), F32),
             jnp.zeros((vt_ref.shape[0], tq), F32))
    for i in range(n_chunks):
        carry = step(i, i % 2 == 1, carry, first=i == 0, last=i == n_chunks - 1)
    _, _, alpha, acc = carry
    acc = values(n_chunks - 1, p_b if n_chunks % 2 == 0 else p_a, alpha, acc)
    o_ref[...] = jnp.transpose(acc[:V_HEAD, :] / acc[V_HEAD:V_HEAD + 1, :]).astype(o_ref.dtype)


def _attention(qt, k, vt, batch, seq, p_dtype):
    n = k.shape[0]
    tq = _tile(seq, ATTN_TQ)
    tk = _tile(seq // 2, ATTN_TK)
    assert (seq // tk) % 2 == 0
    nq = seq // tq
    return pl.pallas_call(
        functools.partial(_attn_kernel, tk, p_dtype),
        grid=(batch, MLA_HEADS, nq),
        scratch_shapes=[pltpu.VMEM((tk, tq), F32), pltpu.VMEM((tk, tq), F32),
                        pltpu.VMEM((tk, tq), BF16), pltpu.VMEM((tk, tq), BF16)],
        in_specs=[pl.BlockSpec((HEAD_PAD, tq), lambda b, h, i: (h, b * nq + i)),
                  pl.BlockSpec((seq, HEAD_PAD), lambda b, h, i: (b, h)),
                  pl.BlockSpec((V_ROWS, seq), lambda b, h, i: (h, b))],
        out_specs=pl.BlockSpec((tq, V_HEAD), lambda b, h, i: (b * nq + i, h)),
        out_shape=jax.ShapeDtypeStruct((n, MLA_HEADS * V_HEAD), BF16),
        compiler_params=_params("parallel", "parallel", "arbitrary"),
        name="mla_attention",
    )(qt, k, vt)


def _gmlp_kernel(u_ref, v_ref, ws_ref, bs_ref, o_ref):
    n_chunks = u_ref.shape[0] // CHUNK
    for c in range(n_chunks):
        rows = slice(c * CHUNK, (c + 1) * CHUNK)
        for g in range(GM_GROUPS):
            cols = slice(g * GM_GROUP_W, (g + 1) * GM_GROUP_W)
            mixed = jnp.dot(ws_ref[g], v_ref[rows, cols], preferred_element_type=F32)
            o_ref[rows, cols] = (u_ref[rows, cols].astype(F32) * (mixed + bs_ref[:, cols])).astype(o_ref.dtype)


def _gmlp_mix(u, v, ws, bs_tile):
    n, w = u.shape
    tm = _tile(n, 4 * CHUNK)
    blk = pl.BlockSpec((tm, w), lambda i: (i, 0))
    return pl.pallas_call(
        _gmlp_kernel,
        grid=(n // tm,),
        in_specs=[blk, blk, pl.BlockSpec(ws.shape, lambda i: (0, 0, 0)),
                  pl.BlockSpec(bs_tile.shape, lambda i: (0, 0))],
        out_specs=blk,
        out_shape=jax.ShapeDtypeStruct((n, w), BF16),
        compiler_params=_params("parallel"),
        name="gmlp_spatial",
    )(u, v, ws, bs_tile)


def _merge_kernel(a_ref, wa_ref, b_ref, wb_ref, ga_ref, gb_ref, o_ref):
    ya = jnp.dot(a_ref[...], wa_ref[...], preferred_element_type=F32)
    yb = jnp.dot(b_ref[...], wb_ref[...], preferred_element_type=F32)
    o_ref[...] = (ga_ref[...].astype(F32) * ya + gb_ref[...].astype(F32) * yb).astype(o_ref.dtype)


def _branch_merge(a, wa, b, wb, gates):
    n, ka = a.shape
    kb = b.shape[1]
    d = wa.shape[1]
    tm = _tile(n, 1024)
    tn = _tile(d, 1024)
    nj = d // tn
    return pl.pallas_call(
        _merge_kernel,
        grid=(nj, n // tm),
        in_specs=[pl.BlockSpec((tm, ka), lambda j, i: (i, 0)), pl.BlockSpec((ka, tn), lambda j, i: (0, j)),
                  pl.BlockSpec((tm, kb), lambda j, i: (i, 0)), pl.BlockSpec((kb, tn), lambda j, i: (0, j)),
                  pl.BlockSpec((tm, tn), lambda j, i: (i, j)),
                  pl.BlockSpec((tm, tn), lambda j, i: (i, j + nj))],
        out_specs=pl.BlockSpec((tm, tn), lambda j, i: (i, j)),
        out_shape=jax.ShapeDtypeStruct((n, d), BF16),
        compiler_params=_params("parallel", "parallel"),
        name="branch_merge",
    )(a, wa, b, wb, gates, gates)


def _epi_norm_residual(acc, h_ref, g_ref):
    return h_ref[...] + _rms(acc, g_ref[...])


def _swiglu_step(x, wg_ref, wu_ref, wd_ref):
    g = jnp.dot(x, wg_ref[...], preferred_element_type=F32)
    u = jnp.dot(x, wu_ref[...], preferred_element_type=F32)
    a = (jax.nn.silu(g) * u).astype(x.dtype)
    return jnp.dot(a, wd_ref[...], preferred_element_type=F32)


def _dense_ffn_kernel(x_ref, wg_ref, wu_ref, wd_ref, o_ref):
    f = pl.program_id(1)
    @pl.when(f == 0)
    def _():
        o_ref[...] = jnp.zeros_like(o_ref)

    o_ref[...] += _swiglu_step(x_ref[...], wg_ref, wu_ref, wd_ref)


def _dense_ffn(x, w_gu, w_down):
    n, d = x.shape
    dff = w_down.shape[0]
    tm = _tile(n, 1024)
    tf = _tile(dff, 512)
    nf = dff // tf
    return pl.pallas_call(
        _dense_ffn_kernel,
        grid=(n // tm, nf),
        in_specs=[pl.BlockSpec((tm, d), lambda i, f: (i, 0)),
                  pl.BlockSpec((d, tf), lambda i, f: (0, f)),
                  pl.BlockSpec((d, tf), lambda i, f: (0, f + nf)),
                  pl.BlockSpec((tf, d), lambda i, f: (f, 0))],
        out_specs=pl.BlockSpec((tm, d), lambda i, f: (i, 0)),
        out_shape=jax.ShapeDtypeStruct((n, d), F32),
        compiler_params=_params("parallel", "arbitrary"),
        name="dense_swiglu",
    )(x, w_gu, w_gu, w_down)


def _moe_ffn_kernel(te_ref, th_ref, nu_ref, x_ref, wg_ref, wu_ref, wd_ref, o_ref):
    t = pl.program_id(0)
    f = pl.program_id(1)
    n_grp = th_ref[t]
    full_grps = MOE_TILE // MOE_GROUP
    half_grps = full_grps // 2

    @pl.when(f == 0)
    def _():
        o_ref[...] = jnp.zeros_like(o_ref)

    @pl.when(n_grp > 0)
    def _():
        wg = wg_ref[...].astype(BF16)
        wu = wu_ref[...].astype(BF16)
        wd = wd_ref[...].astype(BF16)

        def ffn_rows(start, size):
            rows = pl.ds(start, size)
            x = _unpack_bf16_pairs(x_ref[rows, :])
            g = jnp.dot(x, wg, preferred_element_type=F32)
            u = jnp.dot(x, wu, preferred_element_type=F32)
            y = jnp.dot((jax.nn.silu(g) * u).astype(BF16), wd, preferred_element_type=F32)
            o_ref[rows, :] += y

        @pl.when(n_grp == full_grps)
        def _():
            ffn_rows(0, MOE_TILE)

        @pl.when(n_grp < full_grps)
        def _():
            has_half = n_grp >= half_grps
            rem = n_grp - jnp.where(has_half, half_grps, 0)
            base = jnp.where(has_half, MOE_TILE // 2, 0)

            @pl.when(has_half)
            def _():
                ffn_rows(0, MOE_TILE // 2)

            @pl.when(rem >= 2)
            def _():
                ffn_rows(pl.multiple_of(base, MOE_GROUP), 2 * MOE_GROUP)

            @pl.when(rem % 2 == 1)
            def _():
                ffn_rows(pl.multiple_of(base + jnp.where(rem >= 2, 2 * MOE_GROUP, 0), MOE_GROUP), MOE_GROUP)


def _moe_ffn(xs, tile_e, tile_groups, n_used, w_gu, w_down):
    assert MOE_TILE == 8 * MOE_GROUP
    n_slots = xs.shape[0]
    d = w_down.shape[2]
    dff = w_down.shape[1]
    tm = MOE_TILE
    tf = _tile(dff, 512)
    nf = dff // tf

    def _tt(t, nu):
        return jnp.minimum(t, jnp.maximum(nu[0] - 1, 0))

    def _e(t, te, nu):
        return te[_tt(t, nu)]

    def _ff(t, f, nu):
        return jnp.where(t < nu[0], f, nf - 1)

    grid_spec = pltpu.PrefetchScalarGridSpec(
        num_scalar_prefetch=3,
        grid=(n_slots // tm, nf),
        in_specs=[
            pl.BlockSpec((tm, d // 2), lambda t, f, te, th, nu: (_tt(t, nu), 0)),
            pl.BlockSpec((None, d, tf), lambda t, f, te, th, nu: (_e(t, te, nu), 0, _ff(t, f, nu))),
            pl.BlockSpec((None, d, tf), lambda t, f, te, th, nu: (_e(t, te, nu), 0, _ff(t, f, nu) + nf)),
            pl.BlockSpec((None, tf, d), lambda t, f, te, th, nu: (_e(t, te, nu), _ff(t, f, nu), 0)),
        ],
        out_specs=pl.BlockSpec((tm, d), lambda t, f, te, th, nu: (t, 0), pipeline_mode=pl.Buffered(1)),
    )
    return pl.pallas_call(
        _moe_ffn_kernel,
        grid_spec=grid_spec,
        out_shape=jax.ShapeDtypeStruct((n_slots, d), F32),
        compiler_params=_params("arbitrary", "arbitrary"),
        name="moe_swiglu",
    )(tile_e, tile_groups, n_used, xs, w_gu, w_gu, w_down)


def _post_ple_math(h, f, gf_ref, p_ref, wp_ref, wg_ref, gp_ref, o_ref):
    h = h + _rms(f, gf_ref[...])
    e = jnp.dot(p_ref[...].astype(BF16), wp_ref[...], preferred_element_type=F32)
    gate = jax.nn.sigmoid(jnp.dot(h.astype(BF16), wg_ref[...], preferred_element_type=F32))
    o_ref[...] = h + _rms(gate * e, gp_ref[...])


def _ple_kernel(h_ref, f_ref, gf_ref, p_ref, wp_ref, wg_ref, gp_ref, o_ref):
    _post_ple_math(h_ref[...], f_ref[...], gf_ref, p_ref, wp_ref, wg_ref, gp_ref, o_ref)


def _ple_specs(n, d, pd, tm):
    blk = pl.BlockSpec((tm, d), lambda i: (i, 0))
    row = pl.BlockSpec((1, d), lambda i: (0, 0))
    tail = [row, pl.BlockSpec((tm, pd), lambda i: (i, 0)), pl.BlockSpec((pd, d), lambda i: (0, 0)),
            pl.BlockSpec((d, d), lambda i: (0, 0)), row]
    return blk, tail


def _ffn_post_ple(h, f, g_post, p, w_ple_in, w_ple_gate, g_ple):
    n, d = h.shape
    tm = _tile(n, 256)
    blk, tail = _ple_specs(n, d, p.shape[1], tm)
    return pl.pallas_call(
        _ple_kernel,
        grid=(n // tm,),
        in_specs=[blk, blk] + tail,
        out_specs=blk,
        out_shape=jax.ShapeDtypeStruct((n, d), F32),
        compiler_params=_params("parallel"),
        name="ffn_post_ple",
    )(h, f, g_post.reshape(1, d), p, w_ple_in, w_ple_gate, g_ple.reshape(1, d))


def _ple_combine_kernel(d_cur_ref, d_nxt_ref, h_ref, rw_ref, gf_ref, p_ref, wp_ref, wg_ref, gp_ref,
                        ys_ref, o_ref, gbuf, sem):
    i = pl.program_id(0)
    last = pl.num_programs(0) - 1
    slot = i % 2
    tm = h_ref.shape[0]

    def fetch(d_ref, s):
        def body(r, c):
            for k in range(TOP_K):
                src = d_ref[0, 0, TOP_K * r + k]
                pltpu.make_async_copy(ys_ref.at[pl.ds(src, 1)], gbuf.at[s, k, pl.ds(r, 1)], sem.at[s]).start()
            return c
        lax.fori_loop(0, tm, body, 0, unroll=8)

    @pl.when(i == 0)
    def _():
        fetch(d_cur_ref, 0)

    @pl.when(i < last)
    def _():
        fetch(d_nxt_ref, 1 - slot)

    _row_copy_wait(ys_ref.at[pl.ds(0, 1)], gbuf.at[slot, 0, pl.ds(0, 1)], sem.at[slot], TOP_K * tm)
    f = rw_ref[:, 0:1] * gbuf[slot, 0]
    for k in range(1, TOP_K):
        f = f + rw_ref[:, k:k + 1] * gbuf[slot, k]
    _post_ple_math(h_ref[...], f, gf_ref, p_ref, wp_ref, wg_ref, gp_ref, o_ref)


def _moe_combine_post_ple(h, ys, dest, route_w, g_post, p, w_ple_in, w_ple_gate, g_ple):
    n, d = h.shape
    tm = _tile(n, 256)
    nt = n // tm
    blk, tail = _ple_specs(n, d, p.shape[1], tm)
    dest3 = dest.reshape(nt, 1, TOP_K * tm)
    idx = functools.partial(pl.BlockSpec, (1, 1, TOP_K * tm), memory_space=pltpu.SMEM)
    return pl.pallas_call(
        _ple_combine_kernel,
        grid=(nt,),
        in_specs=[idx(lambda i: (i, 0, 0)), idx(lambda i: (jnp.minimum(i + 1, nt - 1), 0, 0)),
                  blk, pl.BlockSpec((tm, LANES), lambda i: (i, 0))] + tail
                 + [pl.BlockSpec(memory_space=pl.ANY)],
        out_specs=blk,
        out_shape=jax.ShapeDtypeStruct((n, d), F32),
        scratch_shapes=[pltpu.VMEM((2, TOP_K, tm, d), F32), pltpu.SemaphoreType.DMA((2,))],
        compiler_params=_params("arbitrary"),
        name="moe_combine_post_ple",
    )(dest3, dest3, h, route_w, g_post.reshape(1, d), p, w_ple_in, w_ple_gate, g_ple.reshape(1, d), ys)


def _slot_tables(route_i, counts_f, n_tok):
    a = n_tok * TOP_K
    counts = counts_f[0, :N_EXPERTS].astype(jnp.int32)
    padded = (counts + MOE_TILE - 1) // MOE_TILE * MOE_TILE
    pad_end = jnp.cumsum(padded)
    pad_start = pad_end - padded
    experts = route_i[:, ROUTE_E:ROUTE_E + TOP_K]
    dest = pad_start[experts] + route_i[:, ROUTE_RANK:ROUTE_RANK + TOP_K]
    n_tiles = -(-a // MOE_TILE) + N_EXPERTS
    tile_start = jnp.arange(n_tiles, dtype=jnp.int32) * MOE_TILE
    tile_e = jnp.minimum(jnp.searchsorted(pad_end, tile_start, side='right'), N_EXPERTS - 1).astype(jnp.int32)
    rows = jnp.clip(pad_start[tile_e] + counts[tile_e] - tile_start, 0, MOE_TILE)
    tile_groups = ((rows + MOE_GROUP - 1) // MOE_GROUP).astype(jnp.int32)
    n_used = (pad_end[-1] // MOE_TILE).astype(jnp.int32).reshape(1)
    return dest.astype(jnp.int32), tile_e, tile_groups, n_used, n_tiles * MOE_TILE


def _rope_slot_cols(w):
    half = QK_ROPE // 2
    z = jnp.zeros(w.shape[:-1] + (half,), w.dtype)
    return jnp.concatenate([w[..., :half], z, w[..., half:], z], axis=-1)


def _prep_layer(i, w_in, gm_ws, gm_bs, w_uq, w_ukv, w_branch_a, w_branch_b, w_out, w_ple_in, w_ple_gate):
    d = w_in.shape[1]
    wi = w_in[i]
    o = 0
    w_u = wi[:, o:o + GM_W]; o += GM_W
    w_v = wi[:, o:o + GM_W]; o += GM_W
    w_cq = wi[:, o:o + Q_LORA]; o += Q_LORA
    w_ckv = wi[:, o:o + KV_LORA]; o += KV_LORA
    w_kr = wi[:, o:o + QK_ROPE]; o += QK_ROPE
    w_g = wi[:, o:o + 2 * d]
    w_uvl = jnp.concatenate([w_u, w_v, w_cq, w_ckv, _rope_slot_cols(w_kr)], axis=1)
    wq = w_uq[i].reshape(Q_LORA, MLA_HEADS, QK_NOPE + QK_ROPE)
    wq = jnp.concatenate([wq[..., :QK_NOPE], _rope_slot_cols(wq[..., QK_NOPE:])], axis=-1)
    wkv = w_ukv[i].reshape(KV_LORA, MLA_HEADS, QK_NOPE + V_HEAD)
    bs_tile = jnp.repeat(gm_bs[i].T, GM_GROUP_W, axis=1)
    return dict(
        w_uvl=w_uvl.astype(BF16), w_g=w_g.astype(BF16),
        w_qt=wq.reshape(Q_LORA, MLA_HEADS * HEAD_PAD).T.astype(BF16),
        w_uk=wkv[..., :QK_NOPE].reshape(KV_LORA, MLA_HEADS * QK_NOPE).astype(BF16),
        w_uvt=wkv[..., QK_NOPE:].reshape(KV_LORA, MLA_HEADS * V_HEAD).T.astype(BF16),
        ws=gm_ws[i].astype(BF16), bs_tile=bs_tile,
        w_a=w_branch_a[i].astype(BF16), w_b=w_branch_b[i].astype(BF16), w_o=w_out[i].astype(BF16),
        w_pi=w_ple_in[i].astype(BF16), w_pg=w_ple_gate[i].astype(BF16),
    )


def kernel(x, p, positions, norm_pre_mix, norm_post_mix, norm_pre_ffn, norm_post_ffn, w_in, gm_ln_g, gm_ln_b, gm_ws, gm_bs, mla_q_norm, w_uq, mla_kv_norm, w_ukv, w_branch_a, w_branch_b, w_out, w_dense_gu, w_dense_down, w_router, w_exp_gu, w_exp_down, w_ple_in, w_ple_gate, ple_norm):
    batch, seq, d = x.shape
    depth = p.shape[0]
    n = batch * seq
    scale = (QK_NOPE + QK_ROPE) ** -0.5 * LOG2_E
    cos_t, sin_t, cos_tt, sin_tt = _rope_tables(positions)
    h = x.reshape(n, d)

    for li in range(depth):
        wts = _prep_layer(li, w_in, gm_ws, gm_bs, w_uq, w_ukv, w_branch_a, w_branch_b, w_out,
                          w_ple_in, w_ple_gate)
        hn, u, v, c_q, c_kv, k_r = _in_proj(h, norm_pre_mix[li], wts["w_uvl"], gm_ln_g[li], gm_ln_b[li],
                                            mla_q_norm[li], mla_kv_norm[li], cos_t, sin_t)
        (gates,) = _mm(hn, wts["w_g"], _epi_sigmoid, tm=1024, tn=1024, outs=[(2 * d, BF16, "tile")],
                       name="in_proj_gates")
        qt = _q_proj(c_q, wts["w_qt"], cos_tt, sin_tt, scale)
        k, vt = _kv_proj(c_kv, wts["w_uk"], wts["w_uvt"], k_r)
        attn = _attention(qt, k, vt, batch, seq, F32)
        gm = _gmlp_mix(u, v, wts["ws"], wts["bs_tile"])
        merged = _branch_merge(gm, wts["w_a"], attn, wts["w_b"], gates)
        (h,) = _mm(merged, wts["w_o"], _epi_norm_residual, tm=512, tn=d,
                   extras=[(h, "rows"), (norm_post_mix[li].reshape(1, d), "vec")],
                   outs=[(d, F32, "tile")], name="out_proj")
        p_li = p[li].reshape(n, -1)
        if li % 2 == 0:
            hn = _norm_bf16(h, norm_pre_ffn[li])
            f = _dense_ffn(hn, w_dense_gu[li // 2].astype(BF16), w_dense_down[li // 2].astype(BF16))
            h = _ffn_post_ple(h, f, norm_post_ffn[li], p_li, wts["w_pi"], wts["w_pg"], ple_norm[li])
        else:
            route_i, route_w, counts = _router(h, norm_pre_ffn[li], w_router[li // 2])
            dest, tile_e, tile_groups, n_used, n_slots = _slot_tables(route_i, counts, n)
            xs = _dispatch(h, norm_pre_ffn[li], dest, n_slots)
            ys = _moe_ffn(xs, tile_e, tile_groups, n_used, w_exp_gu[li // 2], w_exp_down[li // 2])
            h = _moe_combine_post_ple(h, ys, dest, route_w, norm_post_ffn[li], p_li, wts["w_pi"],
                                      wts["w_pg"], ple_norm[li])
    return h.reshape(batch, seq, d)
```

```python
import functools

import jax
import jax.numpy as jnp
from jax import lax
from jax.experimental import pallas as pl
from jax.experimental.pallas import tpu as pltpu

F32 = jnp.float32
BF16 = jnp.bfloat16

GM_GROUPS = 8
GM_GROUP_W = 128
GM_W = GM_GROUPS * GM_GROUP_W
CHUNK = 128
MLA_HEADS = 8
Q_LORA = 512
KV_LORA = 256
QK_NOPE = 128
QK_ROPE = 64
V_HEAD = 128
ROPE_THETA = 10000.0
N_EXPERTS = 8
TOP_K = 2
MOE_BLOCK = 512
MOE_TILE = 2 * MOE_BLOCK
MOE_GROUP = MOE_TILE // 8
EPS = 1e-6
LOG2_E = 1.4426950408889634

LANES = 128
HEAD_PAD = 2 * LANES
BF16_SUBLANES = 16
V_ROWS = V_HEAD + BF16_SUBLANES

ATTN_TQ = 2048
ATTN_TK = 512
ROWS_RESIDENT = 512
ROWS_STREAM = 1024
COLS_STREAM = 1024
FF_TILE = 512
ROWS_COMBINE = 512
V7X_VMEM_BYTES = 64 * 1024 * 1024
VMEM_LIMIT = V7X_VMEM_BYTES - 8 * 1024 * 1024


def _params(*sem):
    return pltpu.CompilerParams(dimension_semantics=sem, vmem_limit_bytes=VMEM_LIMIT)


def _tile(n, want):
    t = min(n, want)
    while n % t:
        t //= 2
    return t


def _rms(x, g):
    return x * lax.rsqrt(jnp.mean(x * x, axis=-1, keepdims=True) + EPS) * g


def _rope_table_kernel(pos_c_ref, pos_r_ref, inv_r_ref, sgn_r_ref, inv_c_ref, sgn_c_ref,
                       cos_ref, sin_ref, cos_t_ref, sin_t_ref):
    ang = pos_c_ref[...].astype(F32) * inv_r_ref[...]
    cos_ref[...] = jnp.cos(ang) * jnp.abs(sgn_r_ref[...])
    sin_ref[...] = jnp.sin(ang) * sgn_r_ref[...]
    ang_t = inv_c_ref[...] * pos_r_ref[...].astype(F32)
    cos_t_ref[...] = jnp.cos(ang_t) * jnp.abs(sgn_c_ref[...])
    sin_t_ref[...] = jnp.sin(ang_t) * sgn_c_ref[...]


def _rope_tables(positions):
    n = positions.size
    tm = _tile(n, ROWS_STREAM)
    half = QK_ROPE // 2
    inv = ROPE_THETA ** (-jnp.arange(0, QK_ROPE, 2, dtype=F32) / QK_ROPE)
    z = jnp.zeros((half,), F32)
    inv_slot = jnp.concatenate([inv, z, inv, z])
    one = jnp.ones((half,), F32)
    sgn_slot = jnp.concatenate([-one, z, one, z])
    row = pl.BlockSpec((1, LANES), lambda i: (0, 0))
    col = pl.BlockSpec((LANES, 1), lambda i: (0, 0))
    tab = pl.BlockSpec((tm, LANES), lambda i: (i, 0))
    tab_t = pl.BlockSpec((LANES, tm), lambda i: (0, i))
    return pl.pallas_call(
        _rope_table_kernel,
        grid=(n // tm,),
        in_specs=[pl.BlockSpec((tm, 1), lambda i: (i, 0)), pl.BlockSpec((1, tm), lambda i: (0, i)),
                  row, row, col, col],
        out_specs=[tab, tab, tab_t, tab_t],
        out_shape=[jax.ShapeDtypeStruct((n, LANES), F32)] * 2 + [jax.ShapeDtypeStruct((LANES, n), F32)] * 2,
        compiler_params=_params("parallel"),
        name="rope_tables",
    )(positions.reshape(n, 1), positions.reshape(1, n), inv_slot.reshape(1, LANES),
      sgn_slot.reshape(1, LANES), inv_slot.reshape(LANES, 1), sgn_slot.reshape(LANES, 1))


def _rope_slot(x, cos_t, sin_t):
    return x * cos_t + pltpu.roll(x, LANES // 2, axis=1) * sin_t


def _norm_kernel(h_ref, g_ref, o_ref):
    o_ref[...] = _rms(h_ref[...], g_ref[...]).astype(o_ref.dtype)


def _norm_bf16(h, g):
    n, d = h.shape
    tm = _tile(n, ROWS_RESIDENT)
    return pl.pallas_call(
        _norm_kernel,
        grid=(n // tm,),
        in_specs=[pl.BlockSpec((tm, d), lambda i: (i, 0)), pl.BlockSpec((1, d), lambda i: (0, 0))],
        out_specs=pl.BlockSpec((tm, d), lambda i: (i, 0)),
        out_shape=jax.ShapeDtypeStruct((n, d), BF16),
        compiler_params=_params("parallel"),
        name="rmsnorm",
    )(h, g.reshape(1, d))


ROUTE_E, ROUTE_RANK = 0, TOP_K


def _router_kernel(h_ref, g_ref, wr_ref, ri_ref, rw_ref, cnt_ref, carry):
    @pl.when(pl.program_id(0) == 0)
    def _():
        carry[...] = jnp.zeros_like(carry)

    y = _rms(h_ref[...], g_ref[...])
    w = wr_ref[...]
    y_hi, w_hi = y.astype(BF16), w.astype(BF16)
    y_lo = (y - y_hi.astype(F32)).astype(BF16)
    w_lo = (w - w_hi.astype(F32)).astype(BF16)
    logits = (jnp.dot(y_hi, w_hi, preferred_element_type=F32)
              + jnp.dot(y_lo, w_hi, preferred_element_type=F32)
              + jnp.dot(y_hi, w_lo, preferred_element_type=F32))
    tm = logits.shape[0]
    lane = lax.broadcasted_iota(jnp.int32, logits.shape, 1)
    lane_f = lane.astype(F32)
    lg = jnp.where(lane < N_EXPERTS, logits, -jnp.inf)
    tops, hots = [], []
    for _ in range(TOP_K):
        mx = jnp.max(lg, axis=1, keepdims=True)
        idx = jnp.min(jnp.where(lg == mx, lane_f, float(LANES)), axis=1, keepdims=True)
        hot = lane_f == idx
        tops.append((mx, idx))
        hots.append(hot)
        lg = jnp.where(hot, -jnp.inf, lg)
    e1 = jnp.exp(tops[1][0] - tops[0][0])
    gates = (1.0 / (1.0 + e1), e1 / (1.0 + e1))
    hot_any = hots[0].astype(F32) + hots[1].astype(F32)
    tri = (lax.broadcasted_iota(jnp.int32, (tm, tm), 0) > lax.broadcasted_iota(jnp.int32, (tm, tm), 1))
    before = jnp.dot(tri.astype(BF16), hot_any.astype(BF16), preferred_element_type=F32) + carry[...]
    ranks = [jnp.sum(jnp.where(hot, before, 0.0), axis=1, keepdims=True) for hot in hots]
    carry[...] += jnp.sum(hot_any, axis=0, keepdims=True)
    cnt_ref[...] = carry[...]
    rec = jnp.zeros(logits.shape, jnp.int32)
    wrec = jnp.zeros(logits.shape, F32)
    for k in range(TOP_K):
        rec = jnp.where(lane == ROUTE_E + k, tops[k][1].astype(jnp.int32), rec)
        rec = jnp.where(lane == ROUTE_RANK + k, ranks[k].astype(jnp.int32), rec)
        wrec = jnp.where(lane == k, gates[k], wrec)
    ri_ref[...] = rec
    rw_ref[...] = wrec


def _router(h, g, w_router):
    n, d = h.shape
    tm = _tile(n, ROWS_RESIDENT)
    wr = jnp.zeros((d, LANES), F32).at[:, :N_EXPERTS].set(w_router)
    rec = pl.BlockSpec((tm, LANES), lambda i: (i, 0))
    return pl.pallas_call(
        _router_kernel,
        grid=(n // tm,),
        in_specs=[pl.BlockSpec((tm, d), lambda i: (i, 0)), pl.BlockSpec((1, d), lambda i: (0, 0)),
                  pl.BlockSpec((d, LANES), lambda i: (0, 0))],
        out_specs=[rec, rec, pl.BlockSpec((1, LANES), lambda i: (0, 0))],
        out_shape=[jax.ShapeDtypeStruct((n, LANES), jnp.int32), jax.ShapeDtypeStruct((n, LANES), F32),
                   jax.ShapeDtypeStruct((1, LANES), F32)],
        scratch_shapes=[pltpu.VMEM((1, LANES), F32)],
        compiler_params=_params("arbitrary"),
        name="moe_router",
    )(h, g.reshape(1, d), wr)


def _pack_bf16_pairs(y):
    half = y.shape[1] // 2
    lo = lax.bitcast_convert_type(y[:, :half].astype(BF16).astype(F32), jnp.uint32)
    hi = lax.bitcast_convert_type(y[:, half:].astype(BF16).astype(F32), jnp.uint32)
    return hi | (lo >> 16)


def _unpack_bf16_pairs(w):
    lo = lax.bitcast_convert_type(w << 16, F32).astype(BF16)
    hi = lax.bitcast_convert_type(w & jnp.uint32(0xFFFF0000), F32).astype(BF16)
    return jnp.concatenate([lo, hi], axis=1)


def _row_copy_wait(src_row, dst_row, sem, count):
    def body(_, c):
        pltpu.make_async_copy(src_row, dst_row, sem).wait()
        return c
    lax.fori_loop(0, count, body, 0, unroll=16)


def _dispatch_kernel(dest_ref, h_ref, g_ref, xs_zero_ref, xs_ref, pbuf, sem):
    del xs_zero_ref
    i = pl.program_id(0)
    last = pl.num_programs(0) - 1
    slot = i % 2
    tm = h_ref.shape[0]

    def wait_slot(s):
        _row_copy_wait(pbuf.at[s, pl.ds(0, 1)], xs_ref.at[pl.ds(0, 1)], sem.at[s], TOP_K * tm)

    @pl.when(i >= 2)
    def _():
        wait_slot(slot)

    pbuf[slot] = _pack_bf16_pairs(_rms(h_ref[...], g_ref[...]))

    def send(r, c):
        for k in range(TOP_K):
            dst = dest_ref[0, 0, TOP_K * r + k]
            pltpu.make_async_copy(pbuf.at[slot, pl.ds(r, 1)], xs_ref.at[pl.ds(dst, 1)], sem.at[slot]).start()
        return c
    lax.fori_loop(0, tm, send, 0, unroll=8)

    @pl.when(i == last)
    def _():
        @pl.when(i >= 1)
        def _():
            wait_slot(1 - slot)
        wait_slot(slot)


def _dispatch(h, g, dest, n_slots):
    n, d = h.shape
    tm = _tile(n, ROWS_RESIDENT)
    nt = n // tm
    xs0 = jnp.zeros((n_slots, d // 2), jnp.uint32)
    return pl.pallas_call(
        _dispatch_kernel,
        grid=(nt,),
        in_specs=[pl.BlockSpec((1, 1, TOP_K * tm), lambda i: (i, 0, 0), memory_space=pltpu.SMEM),
                  pl.BlockSpec((tm, d), lambda i: (i, 0)), pl.BlockSpec((1, d), lambda i: (0, 0)),
                  pl.BlockSpec(memory_space=pl.ANY)],
        out_specs=pl.BlockSpec(memory_space=pl.ANY),
        out_shape=jax.ShapeDtypeStruct((n_slots, d // 2), jnp.uint32),
        scratch_shapes=[pltpu.VMEM((2, tm, d // 2), jnp.uint32), pltpu.SemaphoreType.DMA((2,))],
        input_output_aliases={3: 0},
        compiler_params=_params("arbitrary"),
        name="moe_dispatch",
    )(dest.reshape(nt, 1, TOP_K * tm), h, g.reshape(1, d), xs0)


def _mm_kernel(epi, n_extra, x_ref, w_ref, *rest):
    extras = rest[:n_extra]
    outs = rest[n_extra:]
    acc = jnp.dot(x_ref[...], w_ref[...], preferred_element_type=F32)
    res = epi(acc, *extras)
    if not isinstance(res, (tuple, list)):
        res = (res,)
    for o, r in zip(outs, res):
        o[...] = r.astype(o.dtype)


def _mm(x, w, epi, *, tm, tn, extras=(), outs, name):
    m, k = x.shape
    n = w.shape[1]
    tm = _tile(m, tm)
    tn = _tile(n, tn)

    def spec(ncols, kind):
        if kind == "tile":
            return pl.BlockSpec((tm, tn), lambda j, i: (i, j))
        if kind == "rows":
            return pl.BlockSpec((tm, ncols), lambda j, i: (i, 0))
        assert kind == "vec", kind
        return pl.BlockSpec((1, ncols), lambda j, i: (0, 0))

    in_specs = [pl.BlockSpec((tm, k), lambda j, i: (i, 0)), pl.BlockSpec((k, tn), lambda j, i: (0, j))]
    in_specs += [spec(a.shape[1], kind) for a, kind in extras]
    return pl.pallas_call(
        functools.partial(_mm_kernel, epi, len(extras)),
        grid=(n // tn, m // tm),
        in_specs=in_specs,
        out_specs=[spec(nc, kind) for nc, _, kind in outs],
        out_shape=[jax.ShapeDtypeStruct((m, nc), dt) for nc, dt, _ in outs],
        compiler_params=_params("parallel", "parallel"),
        name=name,
    )(x, w, *[a for a, _ in extras])


def _epi_gelu(acc):
    return jax.nn.gelu(acc)


def _epi_gelu_group_ln(acc, g_ref, b_ref):
    y = jax.nn.gelu(acc)
    g = g_ref[...]
    b = b_ref[...]
    cols = []
    for gi in range(acc.shape[1] // GM_GROUP_W):
        sl = slice(gi * GM_GROUP_W, (gi + 1) * GM_GROUP_W)
        yg = y[:, sl]
        mu = jnp.mean(yg, axis=-1, keepdims=True)
        dlt = yg - mu
        var = jnp.mean(dlt * dlt, axis=-1, keepdims=True)
        cols.append(dlt * lax.rsqrt(var + EPS) * g[:, sl] + b[:, sl])
    return jnp.concatenate(cols, axis=1)


def _epi_sigmoid(acc):
    return jax.nn.sigmoid(acc)


def _epi_latent(acc, qn_ref, kvn_ref, cos_ref, sin_ref):
    c_q = _rms(acc[:, :Q_LORA], qn_ref[...])
    c_kv = _rms(acc[:, Q_LORA:Q_LORA + KV_LORA], kvn_ref[...])
    k_r = _rope_slot(acc[:, Q_LORA + KV_LORA:], cos_ref[...], sin_ref[...])
    return c_q, c_kv, k_r


def _in_proj_kernel(h_ref, gn_ref, w_ref, lg_ref, lb_ref, qn_ref, kvn_ref, cos_ref, sin_ref,
                    hn_ref, u_ref, v_ref, cq_ref, ckv_ref, kr_ref):
    hn = _rms(h_ref[...], gn_ref[...]).astype(hn_ref.dtype)
    hn_ref[...] = hn
    acc = jnp.dot(hn, w_ref[...], preferred_element_type=F32)
    u_ref[...] = _epi_gelu(acc[:, :GM_W]).astype(u_ref.dtype)
    v_ref[...] = _epi_gelu_group_ln(acc[:, GM_W:2 * GM_W], lg_ref, lb_ref).astype(v_ref.dtype)
    c_q, c_kv, k_r = _epi_latent(acc[:, 2 * GM_W:], qn_ref, kvn_ref, cos_ref, sin_ref)
    cq_ref[...] = c_q.astype(cq_ref.dtype)
    ckv_ref[...] = c_kv.astype(ckv_ref.dtype)
    kr_ref[...] = k_r.astype(kr_ref.dtype)


def _in_proj(h, g_norm, w_uvl, ln_g, ln_b, q_norm, kv_norm, cos_t, sin_t):
    n, d = h.shape
    tm = _tile(n, ROWS_RESIDENT)
    vec = lambda a: (a.reshape(1, -1), pl.BlockSpec((1, a.size), lambda i: (0, 0)))
    rows = lambda w: pl.BlockSpec((tm, w), lambda i: (i, 0))
    vecs = [vec(g_norm), vec(ln_g), vec(ln_b), vec(q_norm), vec(kv_norm)]
    widths = [d, GM_W, GM_W, Q_LORA, KV_LORA, LANES]
    return pl.pallas_call(
        _in_proj_kernel,
        grid=(n // tm,),
        in_specs=[rows(d), vecs[0][1],
                  pl.BlockSpec(w_uvl.shape, lambda i: (0, 0), pipeline_mode=pl.Buffered(1)),
                  vecs[1][1], vecs[2][1], vecs[3][1], vecs[4][1], rows(LANES), rows(LANES)],
        out_specs=[rows(w) for w in widths],
        out_shape=[jax.ShapeDtypeStruct((n, w), BF16) for w in widths],
        compiler_params=_params("parallel"),
        name="in_proj_uvl",
    )(h, vecs[0][0], w_uvl, vecs[1][0], vecs[2][0], vecs[3][0], vecs[4][0], cos_t, sin_t)


def _q_proj_kernel(scale, c_ref, w_ref, cos_ref, sin_ref, o_ref):
    acc = lax.dot_general(w_ref[...], c_ref[...], (((1,), (1,)), ((), ())),
                          preferred_element_type=F32)
    cos_t = cos_ref[...] * scale
    sin_t = sin_ref[...] * scale
    half = LANES // 2
    for hd in range(MLA_HEADS):
        base = hd * HEAD_PAD
        o_ref[base:base + QK_NOPE, :] = (acc[base:base + QK_NOPE, :] * scale).astype(o_ref.dtype)
        r = acc[base + QK_NOPE:base + HEAD_PAD, :]
        rot = jnp.concatenate([r[half:, :], r[:half, :]], axis=0)
        o_ref[base + QK_NOPE:base + HEAD_PAD, :] = (r * cos_t + rot * sin_t).astype(o_ref.dtype)


def _q_proj(c_q, w_qt, cos_tt, sin_tt, scale):
    n, r = c_q.shape
    qw = w_qt.shape[0]
    tm = _tile(n, ROWS_RESIDENT)
    tab = pl.BlockSpec((LANES, tm), lambda i: (0, i))
    return pl.pallas_call(
        functools.partial(_q_proj_kernel, scale),
        grid=(n // tm,),
        in_specs=[pl.BlockSpec((tm, r), lambda i: (i, 0)), pl.BlockSpec((qw, r), lambda i: (0, 0)), tab, tab],
        out_specs=pl.BlockSpec((qw, tm), lambda i: (0, i)),
        out_shape=jax.ShapeDtypeStruct((qw, n), BF16),
        compiler_params=_params("parallel"),
        name="mla_q_proj",
    )(c_q, w_qt, cos_tt, sin_tt)


def _kv_kernel(c_ref, wk_ref, wvt_ref, kr_ref, k_ref, vt_ref):
    c = c_ref[...]
    kn = jnp.dot(c, wk_ref[...], preferred_element_type=F32).astype(k_ref.dtype)
    kr = kr_ref[...]
    for hd in range(MLA_HEADS):
        k_ref[:, hd * HEAD_PAD:hd * HEAD_PAD + QK_NOPE] = kn[:, hd * QK_NOPE:(hd + 1) * QK_NOPE]
        k_ref[:, hd * HEAD_PAD + QK_NOPE:(hd + 1) * HEAD_PAD] = kr
    vt = lax.dot_general(wvt_ref[...], c, (((1,), (1,)), ((), ())),
                         preferred_element_type=F32).astype(vt_ref.dtype)
    pad_rows = V_ROWS - V_HEAD
    ones_row = (lax.broadcasted_iota(jnp.int32, (pad_rows, vt.shape[1]), 0) == 0).astype(vt_ref.dtype)
    for hd in range(MLA_HEADS):
        vt_ref[hd * V_ROWS:hd * V_ROWS + V_HEAD, :] = vt[hd * V_HEAD:(hd + 1) * V_HEAD, :]
        vt_ref[hd * V_ROWS + V_HEAD:(hd + 1) * V_ROWS, :] = ones_row


def _kv_proj(c_kv, w_uk, w_uvt, k_rope):
    n, r = c_kv.shape
    tm = _tile(n, ROWS_RESIDENT)
    kw = MLA_HEADS * HEAD_PAD
    vw = MLA_HEADS * V_ROWS
    return pl.pallas_call(
        _kv_kernel,
        grid=(n // tm,),
        in_specs=[pl.BlockSpec((tm, r), lambda i: (i, 0)),
                  pl.BlockSpec(w_uk.shape, lambda i: (0, 0)),
                  pl.BlockSpec(w_uvt.shape, lambda i: (0, 0)),
                  pl.BlockSpec((tm, LANES), lambda i: (i, 0))],
        out_specs=[pl.BlockSpec((tm, kw), lambda i: (i, 0)), pl.BlockSpec((vw, tm), lambda i: (0, i))],
        out_shape=[jax.ShapeDtypeStruct((n, kw), BF16), jax.ShapeDtypeStruct((vw, n), BF16)],
        compiler_params=_params("parallel"),
        name="mla_kv_proj",
    )(c_kv, w_uk, w_uvt, k_rope)


def _attn_kernel(tk, p_dtype, qt_ref, k_ref, vt_ref, o_ref, s_a, s_b, p_a, p_b):
    tq = qt_ref.shape[1]
    n_chunks = k_ref.shape[0] // tk

    def scores(i, s_out):
        off = pl.multiple_of(i * tk, tk)
        s = jnp.dot(k_ref[pl.ds(off, tk), :], qt_ref[...], preferred_element_type=F32)
        s_out[...] = s
        return jnp.max(s, axis=0, keepdims=True)

    def softmax(s_in, p_out, mx, m):
        m_new = jnp.maximum(m, mx)
        x = s_in[...] - m_new
        p_out[...] = jnp.exp2(x.astype(p_dtype)).astype(p_out.dtype)
        return m_new, jnp.exp2(m - m_new)

    def values(i, p_in, alpha, acc):
        off = pl.multiple_of(i * tk, tk)
        return alpha * acc + jnp.dot(vt_ref[:, pl.ds(off, tk)], p_in[...], preferred_element_type=F32)

    def step(i, odd, carry, first=False, last=False):
        s_cur, s_nxt, p_cur, p_prv = (s_b, s_a, p_b, p_a) if odd else (s_a, s_b, p_a, p_b)
        mx, m, alpha, acc = carry
        if not first:
            acc = values(i - 1, p_prv, alpha, acc)
        mx_nxt = mx if last else scores(i + 1, s_nxt)
        m, alpha = softmax(s_cur, p_cur, mx, m)
        return mx_nxt, m, alpha, acc

    carry = (scores(0, s_a), jnp.full((1, tq), -jnp.inf, F32), jnp.ones((1, tq), F32),
             jnp.zeros((vt_ref.shape[0], tq), F32))
    for i in range(n_chunks):
        carry = step(i, i % 2 == 1, carry, first=i == 0, last=i == n_chunks - 1)
    _, _, alpha, acc = carry
    acc = values(n_chunks - 1, p_b if n_chunks % 2 == 0 else p_a, alpha, acc)
    o_ref[...] = jnp.transpose(acc[:V_HEAD, :] / acc[V_HEAD:V_HEAD + 1, :]).astype(o_ref.dtype)


def _attention(qt, k, vt, batch, seq, p_dtype):
    n = k.shape[0]
    tq = _tile(seq, ATTN_TQ)
    tk = _tile(seq // 2, ATTN_TK)
    assert (seq // tk) % 2 == 0
    nq = seq // tq
    return pl.pallas_call(
        functools.partial(_attn_kernel, tk, p_dtype),
        grid=(batch, MLA_HEADS, nq),
        scratch_shapes=[pltpu.VMEM((tk, tq), F32), pltpu.VMEM((tk, tq), F32),
                        pltpu.VMEM((tk, tq), BF16), pltpu.VMEM((tk, tq), BF16)],
        in_specs=[pl.BlockSpec((HEAD_PAD, tq), lambda b, h, i: (h, b * nq + i)),
                  pl.BlockSpec((seq, HEAD_PAD), lambda b, h, i: (b, h)),
                  pl.BlockSpec((V_ROWS, seq), lambda b, h, i: (h, b))],
        out_specs=pl.BlockSpec((tq, V_HEAD), lambda b, h, i: (b * nq + i, h)),
        out_shape=jax.ShapeDtypeStruct((n, MLA_HEADS * V_HEAD), BF16),
        compiler_params=_params("parallel", "parallel", "arbitrary"),
        name="mla_attention",
    )(qt, k, vt)


def _gmlp_kernel(u_ref, v_ref, ws_ref, bs_ref, o_ref):
    n_chunks = u_ref.shape[0] // CHUNK
    for c in range(n_chunks):
        rows = slice(c * CHUNK, (c + 1) * CHUNK)
        for g in range(GM_GROUPS):
            cols = slice(g * GM_GROUP_W, (g + 1) * GM_GROUP_W)
            mixed = jnp.dot(ws_ref[g], v_ref[rows, cols], preferred_element_type=F32)
            o_ref[rows, cols] = (u_ref[rows, cols].astype(F32) * (mixed + bs_ref[:, cols])).astype(o_ref.dtype)


def _gmlp_mix(u, v, ws, bs_tile):
    n, w = u.shape
    tm = _tile(n, 4 * CHUNK)
    blk = pl.BlockSpec((tm, w), lambda i: (i, 0))
    return pl.pallas_call(
        _gmlp_kernel,
        grid=(n // tm,),
        in_specs=[blk, blk, pl.BlockSpec(ws.shape, lambda i: (0, 0, 0)),
                  pl.BlockSpec(bs_tile.shape, lambda i: (0, 0))],
        out_specs=blk,
        out_shape=jax.ShapeDtypeStruct((n, w), BF16),
        compiler_params=_params("parallel"),
        name="gmlp_spatial",
    )(u, v, ws, bs_tile)


def _merge_kernel(a_ref, wa_ref, b_ref, wb_ref, ga_ref, gb_ref, o_ref):
    ya = jnp.dot(a_ref[...], wa_ref[...], preferred_element_type=F32)
    yb = jnp.dot(b_ref[...], wb_ref[...], preferred_element_type=F32)
    o_ref[...] = (ga_ref[...].astype(F32) * ya + gb_ref[...].astype(F32) * yb).astype(o_ref.dtype)


def _branch_merge(a, wa, b, wb, gates):
    n, ka = a.shape
    kb = b.shape[1]
    d = wa.shape[1]
    tm = _tile(n, ROWS_STREAM)
    tn = _tile(d, COLS_STREAM)
    nj = d // tn
    return pl.pallas_call(
        _merge_kernel,
        grid=(nj, n // tm),
        in_specs=[pl.BlockSpec((tm, ka), lambda j, i: (i, 0)), pl.BlockSpec((ka, tn), lambda j, i: (0, j)),
                  pl.BlockSpec((tm, kb), lambda j, i: (i, 0)), pl.BlockSpec((kb, tn), lambda j, i: (0, j)),
                  pl.BlockSpec((tm, tn), lambda j, i: (i, j)),
                  pl.BlockSpec((tm, tn), lambda j, i: (i, j + nj))],
        out_specs=pl.BlockSpec((tm, tn), lambda j, i: (i, j)),
        out_shape=jax.ShapeDtypeStruct((n, d), BF16),
        compiler_params=_params("parallel", "parallel"),
        name="branch_merge",
    )(a, wa, b, wb, gates, gates)


def _epi_norm_residual(acc, h_ref, g_ref):
    return h_ref[...] + _rms(acc, g_ref[...])


def _swiglu_step(x, wg_ref, wu_ref, wd_ref):
    g = jnp.dot(x, wg_ref[...], preferred_element_type=F32)
    u = jnp.dot(x, wu_ref[...], preferred_element_type=F32)
    a = (jax.nn.silu(g) * u).astype(x.dtype)
    return jnp.dot(a, wd_ref[...], preferred_element_type=F32)


def _dense_ffn_kernel(x_ref, wg_ref, wu_ref, wd_ref, o_ref):
    f = pl.program_id(1)
    @pl.when(f == 0)
    def _():
        o_ref[...] = jnp.zeros_like(o_ref)

    o_ref[...] += _swiglu_step(x_ref[...], wg_ref, wu_ref, wd_ref)


def _dense_ffn(x, w_gu, w_down):
    n, d = x.shape
    dff = w_down.shape[0]
    tm = _tile(n, ROWS_STREAM)
    tf = _tile(dff, FF_TILE)
    nf = dff // tf
    return pl.pallas_call(
        _dense_ffn_kernel,
        grid=(n // tm, nf),
        in_specs=[pl.BlockSpec((tm, d), lambda i, f: (i, 0)),
                  pl.BlockSpec((d, tf), lambda i, f: (0, f)),
                  pl.BlockSpec((d, tf), lambda i, f: (0, f + nf)),
                  pl.BlockSpec((tf, d), lambda i, f: (f, 0))],
        out_specs=pl.BlockSpec((tm, d), lambda i, f: (i, 0)),
        out_shape=jax.ShapeDtypeStruct((n, d), F32),
        compiler_params=_params("parallel", "arbitrary"),
        name="dense_swiglu",
    )(x, w_gu, w_gu, w_down)


def _moe_ffn_kernel(te_ref, th_ref, nu_ref, x_ref, wg_ref, wu_ref, wd_ref, o_ref):
    t = pl.program_id(0)
    f = pl.program_id(1)
    n_grp = th_ref[t]
    full_grps = MOE_TILE // MOE_GROUP
    half_grps = full_grps // 2

    @pl.when(f == 0)
    def _():
        o_ref[...] = jnp.zeros_like(o_ref)

    @pl.when(n_grp > 0)
    def _():
        wg = wg_ref[...].astype(BF16)
        wu = wu_ref[...].astype(BF16)
        wd = wd_ref[...].astype(BF16)

        def ffn_rows(start, size):
            rows = pl.ds(start, size)
            x = _unpack_bf16_pairs(x_ref[rows, :])
            g = jnp.dot(x, wg, preferred_element_type=F32)
            u = jnp.dot(x, wu, preferred_element_type=F32)
            y = jnp.dot((jax.nn.silu(g) * u).astype(BF16), wd, preferred_element_type=F32)
            o_ref[rows, :] += y

        @pl.when(n_grp == full_grps)
        def _():
            ffn_rows(0, MOE_TILE)

        @pl.when(n_grp < full_grps)
        def _():
            has_half = n_grp >= half_grps
            rem = n_grp - jnp.where(has_half, half_grps, 0)
            base = jnp.where(has_half, MOE_TILE // 2, 0)

            @pl.when(has_half)
            def _():
                ffn_rows(0, MOE_TILE // 2)

            @pl.when(rem >= 2)
            def _():
                ffn_rows(pl.multiple_of(base, MOE_GROUP), 2 * MOE_GROUP)

            @pl.when(rem % 2 == 1)
            def _():
                ffn_rows(pl.multiple_of(base + jnp.where(rem >= 2, 2 * MOE_GROUP, 0), MOE_GROUP), MOE_GROUP)


def _moe_ffn(xs, tile_e, tile_groups, n_used, w_gu, w_down):
    assert MOE_TILE == 8 * MOE_GROUP
    n_slots = xs.shape[0]
    d = w_down.shape[2]
    dff = w_down.shape[1]
    tm = MOE_TILE
    tf = _tile(dff, FF_TILE)
    nf = dff // tf

    def _tt(t, nu):
        return jnp.minimum(t, jnp.maximum(nu[0] - 1, 0))

    def _e(t, te, nu):
        return te[_tt(t, nu)]

    def _ff(t, f, nu):
        return jnp.where(t < nu[0], f, nf - 1)

    grid_spec = pltpu.PrefetchScalarGridSpec(
        num_scalar_prefetch=3,
        grid=(n_slots // tm, nf),
        in_specs=[
            pl.BlockSpec((tm, d // 2), lambda t, f, te, th, nu: (_tt(t, nu), 0)),
            pl.BlockSpec((None, d, tf), lambda t, f, te, th, nu: (_e(t, te, nu), 0, _ff(t, f, nu))),
            pl.BlockSpec((None, d, tf), lambda t, f, te, th, nu: (_e(t, te, nu), 0, _ff(t, f, nu) + nf)),
            pl.BlockSpec((None, tf, d), lambda t, f, te, th, nu: (_e(t, te, nu), _ff(t, f, nu), 0)),
        ],
        out_specs=pl.BlockSpec((tm, d), lambda t, f, te, th, nu: (t, 0), pipeline_mode=pl.Buffered(1)),
    )
    return pl.pallas_call(
        _moe_ffn_kernel,
        grid_spec=grid_spec,
        out_shape=jax.ShapeDtypeStruct((n_slots, d), F32),
        compiler_params=_params("arbitrary", "arbitrary"),
        name="moe_swiglu",
    )(tile_e, tile_groups, n_used, xs, w_gu, w_gu, w_down)


def _post_ple_math(h, f, gf_ref, p_ref, wp_ref, wg_ref, gp_ref, o_ref):
    h = h + _rms(f, gf_ref[...])
    e = jnp.dot(p_ref[...].astype(BF16), wp_ref[...], preferred_element_type=F32)
    gate = jax.nn.sigmoid(jnp.dot(h.astype(BF16), wg_ref[...], preferred_element_type=F32))
    o_ref[...] = h + _rms(gate * e, gp_ref[...])


def _ple_kernel(h_ref, f_ref, gf_ref, p_ref, wp_ref, wg_ref, gp_ref, o_ref):
    _post_ple_math(h_ref[...], f_ref[...], gf_ref, p_ref, wp_ref, wg_ref, gp_ref, o_ref)


def _ple_specs(n, d, pd, tm):
    blk = pl.BlockSpec((tm, d), lambda i: (i, 0))
    row = pl.BlockSpec((1, d), lambda i: (0, 0))
    tail = [row, pl.BlockSpec((tm, pd), lambda i: (i, 0)), pl.BlockSpec((pd, d), lambda i: (0, 0)),
            pl.BlockSpec((d, d), lambda i: (0, 0)), row]
    return blk, tail


def _ffn_post_ple(h, f, g_post, p, w_ple_in, w_ple_gate, g_ple):
    n, d = h.shape
    tm = _tile(n, ROWS_RESIDENT)
    blk, tail = _ple_specs(n, d, p.shape[1], tm)
    return pl.pallas_call(
        _ple_kernel,
        grid=(n // tm,),
        in_specs=[blk, blk] + tail,
        out_specs=blk,
        out_shape=jax.ShapeDtypeStruct((n, d), F32),
        compiler_params=_params("parallel"),
        name="ffn_post_ple",
    )(h, f, g_post.reshape(1, d), p, w_ple_in, w_ple_gate, g_ple.reshape(1, d))


def _ple_combine_kernel(d_cur_ref, d_nxt_ref, h_ref, rw_ref, gf_ref, p_ref, wp_ref, wg_ref, gp_ref,
                        ys_ref, o_ref, gbuf, sem):
    i = pl.program_id(0)
    last = pl.num_programs(0) - 1
    slot = i % 2
    tm = h_ref.shape[0]

    def fetch(d_ref, s):
        def body(r, c):
            for k in range(TOP_K):
                src = d_ref[0, 0, TOP_K * r + k]
                pltpu.make_async_copy(ys_ref.at[pl.ds(src, 1)], gbuf.at[s, k, pl.ds(r, 1)], sem.at[s]).start()
            return c
        lax.fori_loop(0, tm, body, 0, unroll=8)

    @pl.when(i == 0)
    def _():
        fetch(d_cur_ref, 0)

    @pl.when(i < last)
    def _():
        fetch(d_nxt_ref, 1 - slot)

    _row_copy_wait(ys_ref.at[pl.ds(0, 1)], gbuf.at[slot, 0, pl.ds(0, 1)], sem.at[slot], TOP_K * tm)
    f = rw_ref[:, 0:1] * gbuf[slot, 0]
    for k in range(1, TOP_K):
        f = f + rw_ref[:, k:k + 1] * gbuf[slot, k]
    _post_ple_math(h_ref[...], f, gf_ref, p_ref, wp_ref, wg_ref, gp_ref, o_ref)


def _moe_combine_post_ple(h, ys, dest, route_w, g_post, p, w_ple_in, w_ple_gate, g_ple):
    n, d = h.shape
    tm = _tile(n, ROWS_COMBINE)
    nt = n // tm
    blk, tail = _ple_specs(n, d, p.shape[1], tm)
    dest3 = dest.reshape(nt, 1, TOP_K * tm)
    idx = functools.partial(pl.BlockSpec, (1, 1, TOP_K * tm), memory_space=pltpu.SMEM)
    return pl.pallas_call(
        _ple_combine_kernel,
        grid=(nt,),
        in_specs=[idx(lambda i: (i, 0, 0)), idx(lambda i: (jnp.minimum(i + 1, nt - 1), 0, 0)),
                  blk, pl.BlockSpec((tm, LANES), lambda i: (i, 0))] + tail
                 + [pl.BlockSpec(memory_space=pl.ANY)],
        out_specs=blk,
        out_shape=jax.ShapeDtypeStruct((n, d), F32),
        scratch_shapes=[pltpu.VMEM((2, TOP_K, tm, d), F32), pltpu.SemaphoreType.DMA((2,))],
        compiler_params=_params("arbitrary"),
        name="moe_combine_post_ple",
    )(dest3, dest3, h, route_w, g_post.reshape(1, d), p, w_ple_in, w_ple_gate, g_ple.reshape(1, d), ys)


def _slot_tables(route_i, counts_f, n_tok):
    a = n_tok * TOP_K
    counts = counts_f[0, :N_EXPERTS].astype(jnp.int32)
    padded = (counts + MOE_TILE - 1) // MOE_TILE * MOE_TILE
    pad_end = jnp.cumsum(padded)
    pad_start = pad_end - padded
    experts = route_i[:, ROUTE_E:ROUTE_E + TOP_K]
    dest = pad_start[experts] + route_i[:, ROUTE_RANK:ROUTE_RANK + TOP_K]
    n_tiles = -(-a // MOE_TILE) + N_EXPERTS
    tile_start = jnp.arange(n_tiles, dtype=jnp.int32) * MOE_TILE
    tile_e = jnp.minimum(jnp.searchsorted(pad_end, tile_start, side='right'), N_EXPERTS - 1).astype(jnp.int32)
    rows = jnp.clip(pad_start[tile_e] + counts[tile_e] - tile_start, 0, MOE_TILE)
    tile_groups = ((rows + MOE_GROUP - 1) // MOE_GROUP).astype(jnp.int32)
    n_used = (pad_end[-1] // MOE_TILE).astype(jnp.int32).reshape(1)
    return dest.astype(jnp.int32), tile_e, tile_groups, n_used, n_tiles * MOE_TILE


def _rope_slot_cols(w):
    half = QK_ROPE // 2
    z = jnp.zeros(w.shape[:-1] + (half,), w.dtype)
    return jnp.concatenate([w[..., :half], z, w[..., half:], z], axis=-1)


def _prep_layer(i, w_in, gm_ws, gm_bs, w_uq, w_ukv, w_branch_a, w_branch_b, w_out, w_ple_in, w_ple_gate):
    d = w_in.shape[1]
    wi = w_in[i]
    o = 0
    w_u = wi[:, o:o + GM_W]; o += GM_W
    w_v = wi[:, o:o + GM_W]; o += GM_W
    w_cq = wi[:, o:o + Q_LORA]; o += Q_LORA
    w_ckv = wi[:, o:o + KV_LORA]; o += KV_LORA
    w_kr = wi[:, o:o + QK_ROPE]; o += QK_ROPE
    w_g = wi[:, o:o + 2 * d]
    w_uvl = jnp.concatenate([w_u, w_v, w_cq, w_ckv, _rope_slot_cols(w_kr)], axis=1)
    wq = w_uq[i].reshape(Q_LORA, MLA_HEADS, QK_NOPE + QK_ROPE)
    wq = jnp.concatenate([wq[..., :QK_NOPE], _rope_slot_cols(wq[..., QK_NOPE:])], axis=-1)
    wkv = w_ukv[i].reshape(KV_LORA, MLA_HEADS, QK_NOPE + V_HEAD)
    bs_tile = jnp.repeat(gm_bs[i].T, GM_GROUP_W, axis=1)
    return dict(
        w_uvl=w_uvl.astype(BF16), w_g=w_g.astype(BF16),
        w_qt=wq.reshape(Q_LORA, MLA_HEADS * HEAD_PAD).T.astype(BF16),
        w_uk=wkv[..., :QK_NOPE].reshape(KV_LORA, MLA_HEADS * QK_NOPE).astype(BF16),
        w_uvt=wkv[..., QK_NOPE:].reshape(KV_LORA, MLA_HEADS * V_HEAD).T.astype(BF16),
        ws=gm_ws[i].astype(BF16), bs_tile=bs_tile,
        w_a=w_branch_a[i].astype(BF16), w_b=w_branch_b[i].astype(BF16), w_o=w_out[i].astype(BF16),
        w_pi=w_ple_in[i].astype(BF16), w_pg=w_ple_gate[i].astype(BF16),
    )


def kernel(x, p, positions, norm_pre_mix, norm_post_mix, norm_pre_ffn, norm_post_ffn, w_in, gm_ln_g, gm_ln_b, gm_ws, gm_bs, mla_q_norm, w_uq, mla_kv_norm, w_ukv, w_branch_a, w_branch_b, w_out, w_dense_gu, w_dense_down, w_router, w_exp_gu, w_exp_down, w_ple_in, w_ple_gate, ple_norm):
    batch, seq, d = x.shape
    depth = p.shape[0]
    n = batch * seq
    scale = (QK_NOPE + QK_ROPE) ** -0.5 * LOG2_E
    cos_t, sin_t, cos_tt, sin_tt = _rope_tables(positions)
    h = x.reshape(n, d)

    for li in range(depth):
        wts = _prep_layer(li, w_in, gm_ws, gm_bs, w_uq, w_ukv, w_branch_a, w_branch_b, w_out,
                          w_ple_in, w_ple_gate)
        hn, u, v, c_q, c_kv, k_r = _in_proj(h, norm_pre_mix[li], wts["w_uvl"], gm_ln_g[li], gm_ln_b[li],
                                            mla_q_norm[li], mla_kv_norm[li], cos_t, sin_t)
        (gates,) = _mm(hn, wts["w_g"], _epi_sigmoid, tm=ROWS_STREAM, tn=COLS_STREAM, outs=[(2 * d, BF16, "tile")],
                       name="in_proj_gates")
        qt = _q_proj(c_q, wts["w_qt"], cos_tt, sin_tt, scale)
        k, vt = _kv_proj(c_kv, wts["w_uk"], wts["w_uvt"], k_r)
        attn = _attention(qt, k, vt, batch, seq, F32)
        gm = _gmlp_mix(u, v, wts["ws"], wts["bs_tile"])
        merged = _branch_merge(gm, wts["w_a"], attn, wts["w_b"], gates)
        (h,) = _mm(merged, wts["w_o"], _epi_norm_residual, tm=ROWS_RESIDENT, tn=d,
                   extras=[(h, "rows"), (norm_post_mix[li].reshape(1, d), "vec")],
                   outs=[(d, F32, "tile")], name="out_proj")
        p_li = p[li].reshape(n, -1)
        if li % 2 == 0:
            hn = _norm_bf16(h, norm_pre_ffn[li])
            f = _dense_ffn(hn, w_dense_gu[li // 2].astype(BF16), w_dense_down[li // 2].astype(BF16))
            h = _ffn_post_ple(h, f, norm_post_ffn[li], p_li, wts["w_pi"], wts["w_pg"], ple_norm[li])
        else:
            route_i, route_w, counts = _router(h, norm_pre_ffn[li], w_router[li // 2])
            dest, tile_e, tile_groups, n_used, n_slots = _slot_tables(route_i, counts, n)
            xs = _dispatch(h, norm_pre_ffn[li], dest, n_slots)
            ys = _moe_ffn(xs, tile_e, tile_groups, n_used, w_exp_gu[li // 2], w_exp_down[li // 2])
            h = _moe_combine_post_ple(h, ys, dest, route_w, norm_post_ffn[li], p_li, wts["w_pi"],
                                      wts["w_pg"], ple_norm[li])
    return h.reshape(batch, seq, d)
```

```python
import functools

import jax
import jax.numpy as jnp
from jax import lax
from jax.experimental import pallas as pl
from jax.experimental.pallas import tpu as pltpu

F32 = jnp.float32
BF16 = jnp.bfloat16

GM_GROUPS = 8
GM_GROUP_W = 128
GM_W = GM_GROUPS * GM_GROUP_W
CHUNK = 128
MLA_HEADS = 8
Q_LORA = 512
KV_LORA = 256
QK_NOPE = 128
QK_ROPE = 64
V_HEAD = 128
ROPE_THETA = 10000.0
N_EXPERTS = 8
TOP_K = 2
MOE_BLOCK = 512
MOE_TILE = 2 * MOE_BLOCK
MOE_GROUP = MOE_TILE // 8
EPS = 1e-6
LOG2_E = 1.4426950408889634

LANES = 128
HEAD_PAD = 2 * LANES
BF16_SUBLANES = 16
V_ROWS = V_HEAD + BF16_SUBLANES

ATTN_TQ = 2048
ATTN_TK = 512
ROWS_RESIDENT = 512
ROWS_STREAM = 1024
COLS_STREAM = 1024
FF_TILE = 512
ROWS_COMBINE = 256
V7X_VMEM_BYTES = 64 * 1024 * 1024
VMEM_LIMIT = V7X_VMEM_BYTES - 8 * 1024 * 1024


def _params(*sem):
    return pltpu.CompilerParams(dimension_semantics=sem, vmem_limit_bytes=VMEM_LIMIT)


def _tile(n, want):
    t = min(n, want)
    while n % t:
        t //= 2
    return t


def _rms(x, g):
    return x * lax.rsqrt(jnp.mean(x * x, axis=-1, keepdims=True) + EPS) * g


def _rope_table_kernel(pos_c_ref, pos_r_ref, inv_r_ref, sgn_r_ref, inv_c_ref, sgn_c_ref,
                       cos_ref, sin_ref, cos_t_ref, sin_t_ref):
    ang = pos_c_ref[...].astype(F32) * inv_r_ref[...]
    cos_ref[...] = jnp.cos(ang) * jnp.abs(sgn_r_ref[...])
    sin_ref[...] = jnp.sin(ang) * sgn_r_ref[...]
    ang_t = inv_c_ref[...] * pos_r_ref[...].astype(F32)
    cos_t_ref[...] = jnp.cos(ang_t) * jnp.abs(sgn_c_ref[...])
    sin_t_ref[...] = jnp.sin(ang_t) * sgn_c_ref[...]


def _rope_tables(positions):
    n = positions.size
    tm = _tile(n, ROWS_STREAM)
    half = QK_ROPE // 2
    inv = ROPE_THETA ** (-jnp.arange(0, QK_ROPE, 2, dtype=F32) / QK_ROPE)
    z = jnp.zeros((half,), F32)
    inv_slot = jnp.concatenate([inv, z, inv, z])
    one = jnp.ones((half,), F32)
    sgn_slot = jnp.concatenate([-one, z, one, z])
    row = pl.BlockSpec((1, LANES), lambda i: (0, 0))
    col = pl.BlockSpec((LANES, 1), lambda i: (0, 0))
    tab = pl.BlockSpec((tm, LANES), lambda i: (i, 0))
    tab_t = pl.BlockSpec((LANES, tm), lambda i: (0, i))
    return pl.pallas_call(
        _rope_table_kernel,
        grid=(n // tm,),
        in_specs=[pl.BlockSpec((tm, 1), lambda i: (i, 0)), pl.BlockSpec((1, tm), lambda i: (0, i)),
                  row, row, col, col],
        out_specs=[tab, tab, tab_t, tab_t],
        out_shape=[jax.ShapeDtypeStruct((n, LANES), F32)] * 2 + [jax.ShapeDtypeStruct((LANES, n), F32)] * 2,
        compiler_params=_params("parallel"),
        name="rope_tables",
    )(positions.reshape(n, 1), positions.reshape(1, n), inv_slot.reshape(1, LANES),
      sgn_slot.reshape(1, LANES), inv_slot.reshape(LANES, 1), sgn_slot.reshape(LANES, 1))


def _rope_slot(x, cos_t, sin_t):
    return x * cos_t + pltpu.roll(x, LANES // 2, axis=1) * sin_t


def _norm_kernel(h_ref, g_ref, o_ref):
    o_ref[...] = _rms(h_ref[...], g_ref[...]).astype(o_ref.dtype)


def _norm_bf16(h, g):
    n, d = h.shape
    tm = _tile(n, ROWS_RESIDENT)
    return pl.pallas_call(
        _norm_kernel,
        grid=(n // tm,),
        in_specs=[pl.BlockSpec((tm, d), lambda i: (i, 0)), pl.BlockSpec((1, d), lambda i: (0, 0))],
        out_specs=pl.BlockSpec((tm, d), lambda i: (i, 0)),
        out_shape=jax.ShapeDtypeStruct((n, d), BF16),
        compiler_params=_params("parallel"),
        name="rmsnorm",
    )(h, g.reshape(1, d))


ROUTE_E, ROUTE_RANK = 0, TOP_K


def _router_kernel(h_ref, g_ref, wr_ref, ri_ref, rw_ref, cnt_ref, carry):
    @pl.when(pl.program_id(0) == 0)
    def _():
        carry[...] = jnp.zeros_like(carry)

    y = _rms(h_ref[...], g_ref[...])
    w = wr_ref[...]
    y_hi, w_hi = y.astype(BF16), w.astype(BF16)
    y_lo = (y - y_hi.astype(F32)).astype(BF16)
    w_lo = (w - w_hi.astype(F32)).astype(BF16)
    logits = (jnp.dot(y_hi, w_hi, preferred_element_type=F32)
              + jnp.dot(y_lo, w_hi, preferred_element_type=F32)
              + jnp.dot(y_hi, w_lo, preferred_element_type=F32))
    tm = logits.shape[0]
    lane = lax.broadcasted_iota(jnp.int32, logits.shape, 1)
    lane_f = lane.astype(F32)
    lg = jnp.where(lane < N_EXPERTS, logits, -jnp.inf)
    tops, hots = [], []
    for _ in range(TOP_K):
        mx = jnp.max(lg, axis=1, keepdims=True)
        idx = jnp.min(jnp.where(lg == mx, lane_f, float(LANES)), axis=1, keepdims=True)
        hot = lane_f == idx
        tops.append((mx, idx))
        hots.append(hot)
        lg = jnp.where(hot, -jnp.inf, lg)
    e1 = jnp.exp(tops[1][0] - tops[0][0])
    gates = (1.0 / (1.0 + e1), e1 / (1.0 + e1))
    hot_any = hots[0].astype(F32) + hots[1].astype(F32)
    tri = (lax.broadcasted_iota(jnp.int32, (tm, tm), 0) > lax.broadcasted_iota(jnp.int32, (tm, tm), 1))
    before = jnp.dot(tri.astype(BF16), hot_any.astype(BF16), preferred_element_type=F32) + carry[...]
    ranks = [jnp.sum(jnp.where(hot, before, 0.0), axis=1, keepdims=True) for hot in hots]
    carry[...] += jnp.sum(hot_any, axis=0, keepdims=True)
    cnt_ref[...] = carry[...]
    rec = jnp.zeros(logits.shape, jnp.int32)
    wrec = jnp.zeros(logits.shape, F32)
    for k in range(TOP_K):
        rec = jnp.where(lane == ROUTE_E + k, tops[k][1].astype(jnp.int32), rec)
        rec = jnp.where(lane == ROUTE_RANK + k, ranks[k].astype(jnp.int32), rec)
        wrec = jnp.where(lane == k, gates[k], wrec)
    ri_ref[...] = rec
    rw_ref[...] = wrec


def _router(h, g, w_router):
    n, d = h.shape
    tm = _tile(n, ROWS_RESIDENT)
    wr = jnp.zeros((d, LANES), F32).at[:, :N_EXPERTS].set(w_router)
    rec = pl.BlockSpec((tm, LANES), lambda i: (i, 0))
    return pl.pallas_call(
        _router_kernel,
        grid=(n // tm,),
        in_specs=[pl.BlockSpec((tm, d), lambda i: (i, 0)), pl.BlockSpec((1, d), lambda i: (0, 0)),
                  pl.BlockSpec((d, LANES), lambda i: (0, 0))],
        out_specs=[rec, rec, pl.BlockSpec((1, LANES), lambda i: (0, 0))],
        out_shape=[jax.ShapeDtypeStruct((n, LANES), jnp.int32), jax.ShapeDtypeStruct((n, LANES), F32),
                   jax.ShapeDtypeStruct((1, LANES), F32)],
        scratch_shapes=[pltpu.VMEM((1, LANES), F32)],
        compiler_params=_params("arbitrary"),
        name="moe_router",
    )(h, g.reshape(1, d), wr)


def _pack_bf16_pairs(y):
    half = y.shape[1] // 2
    lo = lax.bitcast_convert_type(y[:, :half].astype(BF16).astype(F32), jnp.uint32)
    hi = lax.bitcast_convert_type(y[:, half:].astype(BF16).astype(F32), jnp.uint32)
    return hi | (lo >> 16)


def _unpack_bf16_pairs(w):
    lo = lax.bitcast_convert_type(w << 16, F32).astype(BF16)
    hi = lax.bitcast_convert_type(w & jnp.uint32(0xFFFF0000), F32).astype(BF16)
    return jnp.concatenate([lo, hi], axis=1)


def _row_copy_wait(src_row, dst_row, sem, count):
    def body(_, c):
        pltpu.make_async_copy(src_row, dst_row, sem).wait()
        return c
    lax.fori_loop(0, count, body, 0, unroll=16)


def _dispatch_kernel(dest_ref, h_ref, g_ref, xs_zero_ref, xs_ref, pbuf, sem):
    del xs_zero_ref
    i = pl.program_id(0)
    last = pl.num_programs(0) - 1
    slot = i % 2
    tm = h_ref.shape[0]

    def wait_slot(s):
        _row_copy_wait(pbuf.at[s, pl.ds(0, 1)], xs_ref.at[pl.ds(0, 1)], sem.at[s], TOP_K * tm)

    @pl.when(i >= 2)
    def _():
        wait_slot(slot)

    pbuf[slot] = _pack_bf16_pairs(_rms(h_ref[...], g_ref[...]))

    def send(r, c):
        for k in range(TOP_K):
            dst = dest_ref[0, 0, TOP_K * r + k]
            pltpu.make_async_copy(pbuf.at[slot, pl.ds(r, 1)], xs_ref.at[pl.ds(dst, 1)], sem.at[slot]).start()
        return c
    lax.fori_loop(0, tm, send, 0, unroll=8)

    @pl.when(i == last)
    def _():
        @pl.when(i >= 1)
        def _():
            wait_slot(1 - slot)
        wait_slot(slot)


def _dispatch(h, g, dest, n_slots):
    n, d = h.shape
    tm = _tile(n, ROWS_RESIDENT)
    nt = n // tm
    xs0 = jnp.zeros((n_slots, d // 2), jnp.uint32)
    return pl.pallas_call(
        _dispatch_kernel,
        grid=(nt,),
        in_specs=[pl.BlockSpec((1, 1, TOP_K * tm), lambda i: (i, 0, 0), memory_space=pltpu.SMEM),
                  pl.BlockSpec((tm, d), lambda i: (i, 0)), pl.BlockSpec((1, d), lambda i: (0, 0)),
                  pl.BlockSpec(memory_space=pl.ANY)],
        out_specs=pl.BlockSpec(memory_space=pl.ANY),
        out_shape=jax.ShapeDtypeStruct((n_slots, d // 2), jnp.uint32),
        scratch_shapes=[pltpu.VMEM((2, tm, d // 2), jnp.uint32), pltpu.SemaphoreType.DMA((2,))],
        input_output_aliases={3: 0},
        compiler_params=_params("arbitrary"),
        name="moe_dispatch",
    )(dest.reshape(nt, 1, TOP_K * tm), h, g.reshape(1, d), xs0)


def _mm_kernel(epi, n_extra, x_ref, w_ref, *rest):
    extras = rest[:n_extra]
    outs = rest[n_extra:]
    acc = jnp.dot(x_ref[...], w_ref[...], preferred_element_type=F32)
    res = epi(acc, *extras)
    if not isinstance(res, (tuple, list)):
        res = (res,)
    for o, r in zip(outs, res):
        o[...] = r.astype(o.dtype)


def _mm(x, w, epi, *, tm, tn, extras=(), outs, name):
    m, k = x.shape
    n = w.shape[1]
    tm = _tile(m, tm)
    tn = _tile(n, tn)

    def spec(ncols, kind):
        if kind == "tile":
            return pl.BlockSpec((tm, tn), lambda j, i: (i, j))
        if kind == "rows":
            return pl.BlockSpec((tm, ncols), lambda j, i: (i, 0))
        assert kind == "vec", kind
        return pl.BlockSpec((1, ncols), lambda j, i: (0, 0))

    in_specs = [pl.BlockSpec((tm, k), lambda j, i: (i, 0)), pl.BlockSpec((k, tn), lambda j, i: (0, j))]
    in_specs += [spec(a.shape[1], kind) for a, kind in extras]
    return pl.pallas_call(
        functools.partial(_mm_kernel, epi, len(extras)),
        grid=(n // tn, m // tm),
        in_specs=in_specs,
        out_specs=[spec(nc, kind) for nc, _, kind in outs],
        out_shape=[jax.ShapeDtypeStruct((m, nc), dt) for nc, dt, _ in outs],
        compiler_params=_params("parallel", "parallel"),
        name=name,
    )(x, w, *[a for a, _ in extras])


def _epi_gelu(acc):
    return jax.nn.gelu(acc)


def _epi_gelu_group_ln(acc, g_ref, b_ref):
    y = jax.nn.gelu(acc)
    g = g_ref[...]
    b = b_ref[...]
    cols = []
    for gi in range(acc.shape[1] // GM_GROUP_W):
        sl = slice(gi * GM_GROUP_W, (gi + 1) * GM_GROUP_W)
        yg = y[:, sl]
        mu = jnp.mean(yg, axis=-1, keepdims=True)
        dlt = yg - mu
        var = jnp.mean(dlt * dlt, axis=-1, keepdims=True)
        cols.append(dlt * lax.rsqrt(var + EPS) * g[:, sl] + b[:, sl])
    return jnp.concatenate(cols, axis=1)


def _epi_sigmoid(acc):
    return jax.nn.sigmoid(acc)


def _epi_latent(acc, qn_ref, kvn_ref, cos_ref, sin_ref):
    c_q = _rms(acc[:, :Q_LORA], qn_ref[...])
    c_kv = _rms(acc[:, Q_LORA:Q_LORA + KV_LORA], kvn_ref[...])
    k_r = _rope_slot(acc[:, Q_LORA + KV_LORA:], cos_ref[...], sin_ref[...])
    return c_q, c_kv, k_r


def _in_proj_kernel(h_ref, gn_ref, w_ref, lg_ref, lb_ref, qn_ref, kvn_ref, cos_ref, sin_ref,
                    hn_ref, u_ref, v_ref, cq_ref, ckv_ref, kr_ref):
    hn = _rms(h_ref[...], gn_ref[...]).astype(hn_ref.dtype)
    hn_ref[...] = hn
    acc = jnp.dot(hn, w_ref[...], preferred_element_type=F32)
    u_ref[...] = _epi_gelu(acc[:, :GM_W]).astype(u_ref.dtype)
    v_ref[...] = _epi_gelu_group_ln(acc[:, GM_W:2 * GM_W], lg_ref, lb_ref).astype(v_ref.dtype)
    c_q, c_kv, k_r = _epi_latent(acc[:, 2 * GM_W:], qn_ref, kvn_ref, cos_ref, sin_ref)
    cq_ref[...] = c_q.astype(cq_ref.dtype)
    ckv_ref[...] = c_kv.astype(ckv_ref.dtype)
    kr_ref[...] = k_r.astype(kr_ref.dtype)


def _in_proj(h, g_norm, w_uvl, ln_g, ln_b, q_norm, kv_norm, cos_t, sin_t):
    n, d = h.shape
    tm = _tile(n, ROWS_RESIDENT)
    vec = lambda a: (a.reshape(1, -1), pl.BlockSpec((1, a.size), lambda i: (0, 0)))
    rows = lambda w: pl.BlockSpec((tm, w), lambda i: (i, 0))
    vecs = [vec(g_norm), vec(ln_g), vec(ln_b), vec(q_norm), vec(kv_norm)]
    widths = [d, GM_W, GM_W, Q_LORA, KV_LORA, LANES]
    return pl.pallas_call(
        _in_proj_kernel,
        grid=(n // tm,),
        in_specs=[rows(d), vecs[0][1],
                  pl.BlockSpec(w_uvl.shape, lambda i: (0, 0), pipeline_mode=pl.Buffered(1)),
                  vecs[1][1], vecs[2][1], vecs[3][1], vecs[4][1], rows(LANES), rows(LANES)],
        out_specs=[rows(w) for w in widths],
        out_shape=[jax.ShapeDtypeStruct((n, w), BF16) for w in widths],
        compiler_params=_params("parallel"),
        name="in_proj_uvl",
    )(h, vecs[0][0], w_uvl, vecs[1][0], vecs[2][0], vecs[3][0], vecs[4][0], cos_t, sin_t)


def _q_proj_kernel(scale, c_ref, w_ref, cos_ref, sin_ref, o_ref):
    acc = lax.dot_general(w_ref[...], c_ref[...], (((1,), (1,)), ((), ())),
                          preferred_element_type=F32)
    cos_t = cos_ref[...] * scale
    sin_t = sin_ref[...] * scale
    half = LANES // 2
    for hd in range(MLA_HEADS):
        base = hd * HEAD_PAD
        o_ref[base:base + QK_NOPE, :] = (acc[base:base + QK_NOPE, :] * scale).astype(o_ref.dtype)
        r = acc[base + QK_NOPE:base + HEAD_PAD, :]
        rot = jnp.concatenate([r[half:, :], r[:half, :]], axis=0)
        o_ref[base + QK_NOPE:base + HEAD_PAD, :] = (r * cos_t + rot * sin_t).astype(o_ref.dtype)


def _q_proj(c_q, w_qt, cos_tt, sin_tt, scale):
    n, r = c_q.shape
    qw = w_qt.shape[0]
    tm = _tile(n, ROWS_RESIDENT)
    tab = pl.BlockSpec((LANES, tm), lambda i: (0, i))
    return pl.pallas_call(
        functools.partial(_q_proj_kernel, scale),
        grid=(n // tm,),
        in_specs=[pl.BlockSpec((tm, r), lambda i: (i, 0)), pl.BlockSpec((qw, r), lambda i: (0, 0)), tab, tab],
        out_specs=pl.BlockSpec((qw, tm), lambda i: (0, i)),
        out_shape=jax.ShapeDtypeStruct((qw, n), BF16),
        compiler_params=_params("parallel"),
        name="mla_q_proj",
    )(c_q, w_qt, cos_tt, sin_tt)


def _kv_kernel(c_ref, wk_ref, wvt_ref, kr_ref, k_ref, vt_ref):
    c = c_ref[...]
    kn = jnp.dot(c, wk_ref[...], preferred_element_type=F32).astype(k_ref.dtype)
    kr = kr_ref[...]
    for hd in range(MLA_HEADS):
        k_ref[:, hd * HEAD_PAD:hd * HEAD_PAD + QK_NOPE] = kn[:, hd * QK_NOPE:(hd + 1) * QK_NOPE]
        k_ref[:, hd * HEAD_PAD + QK_NOPE:(hd + 1) * HEAD_PAD] = kr
    vt = lax.dot_general(wvt_ref[...], c, (((1,), (1,)), ((), ())),
                         preferred_element_type=F32).astype(vt_ref.dtype)
    pad_rows = V_ROWS - V_HEAD
    ones_row = (lax.broadcasted_iota(jnp.int32, (pad_rows, vt.shape[1]), 0) == 0).astype(vt_ref.dtype)
    for hd in range(MLA_HEADS):
        vt_ref[hd * V_ROWS:hd * V_ROWS + V_HEAD, :] = vt[hd * V_HEAD:(hd + 1) * V_HEAD, :]
        vt_ref[hd * V_ROWS + V_HEAD:(hd + 1) * V_ROWS, :] = ones_row


def _kv_proj(c_kv, w_uk, w_uvt, k_rope):
    n, r = c_kv.shape
    tm = _tile(n, ROWS_RESIDENT)
    kw = MLA_HEADS * HEAD_PAD
    vw = MLA_HEADS * V_ROWS
    return pl.pallas_call(
        _kv_kernel,
        grid=(n // tm,),
        in_specs=[pl.BlockSpec((tm, r), lambda i: (i, 0)),
                  pl.BlockSpec(w_uk.shape, lambda i: (0, 0)),
                  pl.BlockSpec(w_uvt.shape, lambda i: (0, 0)),
                  pl.BlockSpec((tm, LANES), lambda i: (i, 0))],
        out_specs=[pl.BlockSpec((tm, kw), lambda i: (i, 0)), pl.BlockSpec((vw, tm), lambda i: (0, i))],
        out_shape=[jax.ShapeDtypeStruct((n, kw), BF16), jax.ShapeDtypeStruct((vw, n), BF16)],
        compiler_params=_params("parallel"),
        name="mla_kv_proj",
    )(c_kv, w_uk, w_uvt, k_rope)


def _attn_kernel(tk, p_dtype, qt_ref, k_ref, vt_ref, o_ref, s_a, s_b, p_a, p_b):
    tq = qt_ref.shape[1]
    n_chunks = k_ref.shape[0] // tk

    def scores(i, s_out):
        off = pl.multiple_of(i * tk, tk)
        s = jnp.dot(k_ref[pl.ds(off, tk), :], qt_ref[...], preferred_element_type=F32)
        s_out[...] = s
        return jnp.max(s, axis=0, keepdims=True)

    def softmax(s_in, p_out, mx, m):
        m_new = jnp.maximum(m, mx)
        x = s_in[...] - m_new
        p_out[...] = jnp.exp2(x.astype(p_dtype)).astype(p_out.dtype)
        return m_new, jnp.exp2(m - m_new)

    def values(i, p_in, alpha, acc):
        off = pl.multiple_of(i * tk, tk)
        return alpha * acc + jnp.dot(vt_ref[:, pl.ds(off, tk)], p_in[...], preferred_element_type=F32)

    def step(i, odd, carry, first=False, last=False):
        s_cur, s_nxt, p_cur, p_prv = (s_b, s_a, p_b, p_a) if odd else (s_a, s_b, p_a, p_b)
        mx, m, alpha, acc = carry
        if not first:
            acc = values(i - 1, p_prv, alpha, acc)
        mx_nxt = mx if last else scores(i + 1, s_nxt)
        m, alpha = softmax(s_cur, p_cur, mx, m)
        return mx_nxt, m, alpha, acc

    carry = (scores(0, s_a), jnp.full((1, tq), -jnp.inf, F32), jnp.ones((1, tq), F32),
             jnp.zeros((vt_ref.shape[0], tq), F32))
    for i in range(n_chunks):
        carry = step(i, i % 2 == 1, carry, first=i == 0, last=i == n_chunks - 1)
    _, _, alpha, acc = carry
    acc = values(n_chunks - 1, p_b if n_chunks % 2 == 0 else p_a, alpha, acc)
    o_ref[...] = jnp.transpose(acc[:V_HEAD, :] / acc[V_HEAD:V_HEAD + 1, :]).astype(o_ref.dtype)


def _attention(qt, k, vt, batch, seq, p_dtype):
    n = k.shape[0]
    tq = _tile(seq, ATTN_TQ)
    tk = _tile(seq // 2, ATTN_TK)
    assert (seq // tk) % 2 == 0
    nq = seq // tq
    return pl.pallas_call(
        functools.partial(_attn_kernel, tk, p_dtype),
        grid=(batch, MLA_HEADS, nq),
        scratch_shapes=[pltpu.VMEM((tk, tq), F32), pltpu.VMEM((tk, tq), F32),
                        pltpu.VMEM((tk, tq), BF16), pltpu.VMEM((tk, tq), BF16)],
        in_specs=[pl.BlockSpec((HEAD_PAD, tq), lambda b, h, i: (h, b * nq + i)),
                  pl.BlockSpec((seq, HEAD_PAD), lambda b, h, i: (b, h)),
                  pl.BlockSpec((V_ROWS, seq), lambda b, h, i: (h, b))],
        out_specs=pl.BlockSpec((tq, V_HEAD), lambda b, h, i: (b * nq + i, h)),
        out_shape=jax.ShapeDtypeStruct((n, MLA_HEADS * V_HEAD), BF16),
        compiler_params=_params("parallel", "parallel", "arbitrary"),
        name="mla_attention",
    )(qt, k, vt)


def _gmlp_merge_kernel(u_ref, v_ref, ws_ref, bs_ref, wa_ref, b_ref, wb_ref, ga_ref, gb_ref, o_ref, gm):
    n_chunks = u_ref.shape[0] // CHUNK
    for c in range(n_chunks):
        rows = slice(c * CHUNK, (c + 1) * CHUNK)
        for g in range(GM_GROUPS):
            cols = slice(g * GM_GROUP_W, (g + 1) * GM_GROUP_W)
            mixed = jnp.dot(ws_ref[g], v_ref[rows, cols], preferred_element_type=F32)
            gm[rows, cols] = (u_ref[rows, cols].astype(F32) * (mixed + bs_ref[:, cols])).astype(gm.dtype)
    ya = jnp.dot(gm[...], wa_ref[...], preferred_element_type=F32)
    yb = jnp.dot(b_ref[...], wb_ref[...], preferred_element_type=F32)
    o_ref[...] = (ga_ref[...].astype(F32) * ya + gb_ref[...].astype(F32) * yb).astype(o_ref.dtype)


def _gmlp_branch_merge(u, v, ws, bs_tile, wa, attn, wb, gates):
    n, w = u.shape
    kb = attn.shape[1]
    d = wa.shape[1]
    tm = _tile(n, 4 * CHUNK)
    rows = lambda width, j=0: pl.BlockSpec((tm, width), lambda i: (i, j))
    resident = lambda a: pl.BlockSpec(a.shape, lambda i: (0,) * a.ndim, pipeline_mode=pl.Buffered(1))
    return pl.pallas_call(
        _gmlp_merge_kernel,
        grid=(n // tm,),
        in_specs=[rows(w), rows(w), resident(ws), resident(bs_tile), resident(wa), rows(kb), resident(wb),
                  rows(d, 0), rows(d, 1)],
        out_specs=rows(d),
        out_shape=jax.ShapeDtypeStruct((n, d), BF16),
        scratch_shapes=[pltpu.VMEM((tm, w), BF16)],
        compiler_params=_params("parallel"),
        name="gmlp_branch_merge",
    )(u, v, ws, bs_tile, wa, attn, wb, gates, gates)


def _epi_norm_residual(acc, h_ref, g_ref):
    return h_ref[...] + _rms(acc, g_ref[...])


def _swiglu_step(x, wg_ref, wu_ref, wd_ref):
    g = jnp.dot(x, wg_ref[...], preferred_element_type=F32)
    u = jnp.dot(x, wu_ref[...], preferred_element_type=F32)
    a = (jax.nn.silu(g) * u).astype(x.dtype)
    return jnp.dot(a, wd_ref[...], preferred_element_type=F32)


def _dense_ffn_kernel(x_ref, wg_ref, wu_ref, wd_ref, o_ref):
    f = pl.program_id(1)
    @pl.when(f == 0)
    def _():
        o_ref[...] = jnp.zeros_like(o_ref)

    o_ref[...] += _swiglu_step(x_ref[...], wg_ref, wu_ref, wd_ref)


def _dense_ffn(x, w_gu, w_down):
    n, d = x.shape
    dff = w_down.shape[0]
    tm = _tile(n, ROWS_STREAM)
    tf = _tile(dff, FF_TILE)
    nf = dff // tf
    return pl.pallas_call(
        _dense_ffn_kernel,
        grid=(n // tm, nf),
        in_specs=[pl.BlockSpec((tm, d), lambda i, f: (i, 0)),
                  pl.BlockSpec((d, tf), lambda i, f: (0, f)),
                  pl.BlockSpec((d, tf), lambda i, f: (0, f + nf)),
                  pl.BlockSpec((tf, d), lambda i, f: (f, 0))],
        out_specs=pl.BlockSpec((tm, d), lambda i, f: (i, 0)),
        out_shape=jax.ShapeDtypeStruct((n, d), F32),
        compiler_params=_params("parallel", "arbitrary"),
        name="dense_swiglu",
    )(x, w_gu, w_gu, w_down)


def _moe_ffn_kernel(te_ref, th_ref, nu_ref, x_ref, wg_ref, wu_ref, wd_ref, o_ref):
    t = pl.program_id(0)
    f = pl.program_id(1)
    n_grp = th_ref[t]
    full_grps = MOE_TILE // MOE_GROUP
    half_grps = full_grps // 2

    @pl.when(f == 0)
    def _():
        o_ref[...] = jnp.zeros_like(o_ref)

    @pl.when(n_grp > 0)
    def _():
        wg = wg_ref[...].astype(BF16)
        wu = wu_ref[...].astype(BF16)
        wd = wd_ref[...].astype(BF16)

        def ffn_rows(start, size):
            rows = pl.ds(start, size)
            x = _unpack_bf16_pairs(x_ref[rows, :])
            g = jnp.dot(x, wg, preferred_element_type=F32)
            u = jnp.dot(x, wu, preferred_element_type=F32)
            y = jnp.dot((jax.nn.silu(g) * u).astype(BF16), wd, preferred_element_type=F32)
            o_ref[rows, :] += y

        @pl.when(n_grp == full_grps)
        def _():
            ffn_rows(0, MOE_TILE)

        @pl.when(n_grp < full_grps)
        def _():
            has_half = n_grp >= half_grps
            rem = n_grp - jnp.where(has_half, half_grps, 0)
            base = jnp.where(has_half, MOE_TILE // 2, 0)

            @pl.when(has_half)
            def _():
                ffn_rows(0, MOE_TILE // 2)

            @pl.when(rem >= 2)
            def _():
                ffn_rows(pl.multiple_of(base, MOE_GROUP), 2 * MOE_GROUP)

            @pl.when(rem % 2 == 1)
            def _():
                ffn_rows(pl.multiple_of(base + jnp.where(rem >= 2, 2 * MOE_GROUP, 0), MOE_GROUP), MOE_GROUP)


def _moe_ffn(xs, tile_e, tile_groups, n_used, w_gu, w_down):
    assert MOE_TILE == 8 * MOE_GROUP
    n_slots = xs.shape[0]
    d = w_down.shape[2]
    dff = w_down.shape[1]
    tm = MOE_TILE
    tf = _tile(dff, FF_TILE)
    nf = dff // tf

    def _tt(t, nu):
        return jnp.minimum(t, jnp.maximum(nu[0] - 1, 0))

    def _e(t, te, nu):
        return te[_tt(t, nu)]

    def _ff(t, f, nu):
        return jnp.where(t < nu[0], f, nf - 1)

    grid_spec = pltpu.PrefetchScalarGridSpec(
        num_scalar_prefetch=3,
        grid=(n_slots // tm, nf),
        in_specs=[
            pl.BlockSpec((tm, d // 2), lambda t, f, te, th, nu: (_tt(t, nu), 0)),
            pl.BlockSpec((None, d, tf), lambda t, f, te, th, nu: (_e(t, te, nu), 0, _ff(t, f, nu))),
            pl.BlockSpec((None, d, tf), lambda t, f, te, th, nu: (_e(t, te, nu), 0, _ff(t, f, nu) + nf)),
            pl.BlockSpec((None, tf, d), lambda t, f, te, th, nu: (_e(t, te, nu), _ff(t, f, nu), 0)),
        ],
        out_specs=pl.BlockSpec((tm, d), lambda t, f, te, th, nu: (t, 0), pipeline_mode=pl.Buffered(1)),
    )
    return pl.pallas_call(
        _moe_ffn_kernel,
        grid_spec=grid_spec,
        out_shape=jax.ShapeDtypeStruct((n_slots, d), F32),
        compiler_params=_params("arbitrary", "arbitrary"),
        name="moe_swiglu",
    )(tile_e, tile_groups, n_used, xs, w_gu, w_gu, w_down)


def _post_ple_math(h, f, gf_ref, p_ref, wp_ref, wg_ref, gp_ref, o_ref):
    h = h + _rms(f, gf_ref[...])
    e = jnp.dot(p_ref[...].astype(BF16), wp_ref[...], preferred_element_type=F32)
    gate = jax.nn.sigmoid(jnp.dot(h.astype(BF16), wg_ref[...], preferred_element_type=F32))
    o_ref[...] = h + _rms(gate * e, gp_ref[...])


def _ple_kernel(h_ref, f_ref, gf_ref, p_ref, wp_ref, wg_ref, gp_ref, o_ref):
    _post_ple_math(h_ref[...], f_ref[...], gf_ref, p_ref, wp_ref, wg_ref, gp_ref, o_ref)


def _ple_specs(n, d, pd, tm):
    blk = pl.BlockSpec((tm, d), lambda i: (i, 0))
    row = pl.BlockSpec((1, d), lambda i: (0, 0))
    tail = [row, pl.BlockSpec((tm, pd), lambda i: (i, 0)), pl.BlockSpec((pd, d), lambda i: (0, 0)),
            pl.BlockSpec((d, d), lambda i: (0, 0)), row]
    return blk, tail


def _ffn_post_ple(h, f, g_post, p, w_ple_in, w_ple_gate, g_ple):
    n, d = h.shape
    tm = _tile(n, ROWS_RESIDENT)
    blk, tail = _ple_specs(n, d, p.shape[1], tm)
    return pl.pallas_call(
        _ple_kernel,
        grid=(n // tm,),
        in_specs=[blk, blk] + tail,
        out_specs=blk,
        out_shape=jax.ShapeDtypeStruct((n, d), F32),
        compiler_params=_params("parallel"),
        name="ffn_post_ple",
    )(h, f, g_post.reshape(1, d), p, w_ple_in, w_ple_gate, g_ple.reshape(1, d))


def _ple_combine_kernel(d_cur_ref, d_nxt_ref, h_ref, rw_ref, gf_ref, p_ref, wp_ref, wg_ref, gp_ref,
                        ys_ref, o_ref, gbuf, sem):
    i = pl.program_id(0)
    last = pl.num_programs(0) - 1
    slot = i % 2
    tm = h_ref.shape[0]

    def fetch(d_ref, s):
        def body(r, c):
            for k in range(TOP_K):
                src = d_ref[0, 0, TOP_K * r + k]
                pltpu.make_async_copy(ys_ref.at[pl.ds(src, 1)], gbuf.at[s, k, pl.ds(r, 1)], sem.at[s]).start()
            return c
        lax.fori_loop(0, tm, body, 0, unroll=8)

    @pl.when(i == 0)
    def _():
        fetch(d_cur_ref, 0)

    @pl.when(i < last)
    def _():
        fetch(d_nxt_ref, 1 - slot)

    _row_copy_wait(ys_ref.at[pl.ds(0, 1)], gbuf.at[slot, 0, pl.ds(0, 1)], sem.at[slot], TOP_K * tm)
    f = rw_ref[:, 0:1] * gbuf[slot, 0]
    for k in range(1, TOP_K):
        f = f + rw_ref[:, k:k + 1] * gbuf[slot, k]
    _post_ple_math(h_ref[...], f, gf_ref, p_ref, wp_ref, wg_ref, gp_ref, o_ref)


def _moe_combine_post_ple(h, ys, dest, route_w, g_post, p, w_ple_in, w_ple_gate, g_ple):
    n, d = h.shape
    tm = _tile(n, ROWS_COMBINE)
    nt = n // tm
    blk, tail = _ple_specs(n, d, p.shape[1], tm)
    dest3 = dest.reshape(nt, 1, TOP_K * tm)
    idx = functools.partial(pl.BlockSpec, (1, 1, TOP_K * tm), memory_space=pltpu.SMEM)
    return pl.pallas_call(
        _ple_combine_kernel,
        grid=(nt,),
        in_specs=[idx(lambda i: (i, 0, 0)), idx(lambda i: (jnp.minimum(i + 1, nt - 1), 0, 0)),
                  blk, pl.BlockSpec((tm, LANES), lambda i: (i, 0))] + tail
                 + [pl.BlockSpec(memory_space=pl.ANY)],
        out_specs=blk,
        out_shape=jax.ShapeDtypeStruct((n, d), F32),
        scratch_shapes=[pltpu.VMEM((2, TOP_K, tm, d), F32), pltpu.SemaphoreType.DMA((2,))],
        compiler_params=_params("arbitrary"),
        name="moe_combine_post_ple",
    )(dest3, dest3, h, route_w, g_post.reshape(1, d), p, w_ple_in, w_ple_gate, g_ple.reshape(1, d), ys)


def _slot_tables(route_i, counts_f, n_tok):
    a = n_tok * TOP_K
    counts = counts_f[0, :N_EXPERTS].astype(jnp.int32)
    padded = (counts + MOE_TILE - 1) // MOE_TILE * MOE_TILE
    pad_end = jnp.cumsum(padded)
    pad_start = pad_end - padded
    experts = route_i[:, ROUTE_E:ROUTE_E + TOP_K]
    dest = pad_start[experts] + route_i[:, ROUTE_RANK:ROUTE_RANK + TOP_K]
    n_tiles = -(-a // MOE_TILE) + N_EXPERTS
    tile_start = jnp.arange(n_tiles, dtype=jnp.int32) * MOE_TILE
    tile_e = jnp.minimum(jnp.sum((pad_end[None, :] <= tile_start[:, None]).astype(jnp.int32), axis=1),
                         N_EXPERTS - 1)
    rows = jnp.clip(pad_start[tile_e] + counts[tile_e] - tile_start, 0, MOE_TILE)
    tile_groups = ((rows + MOE_GROUP - 1) // MOE_GROUP).astype(jnp.int32)
    n_used = (pad_end[-1] // MOE_TILE).astype(jnp.int32).reshape(1)
    return dest.astype(jnp.int32), tile_e, tile_groups, n_used, n_tiles * MOE_TILE


def _rope_slot_cols(w):
    half = QK_ROPE // 2
    z = jnp.zeros(w.shape[:-1] + (half,), w.dtype)
    return jnp.concatenate([w[..., :half], z, w[..., half:], z], axis=-1)


def _prep_layer(i, w_in, gm_ws, gm_bs, w_uq, w_ukv, w_branch_a, w_branch_b, w_out, w_ple_in, w_ple_gate):
    d = w_in.shape[1]
    wi = w_in[i]
    o = 0
    w_u = wi[:, o:o + GM_W]; o += GM_W
    w_v = wi[:, o:o + GM_W]; o += GM_W
    w_cq = wi[:, o:o + Q_LORA]; o += Q_LORA
    w_ckv = wi[:, o:o + KV_LORA]; o += KV_LORA
    w_kr = wi[:, o:o + QK_ROPE]; o += QK_ROPE
    w_g = wi[:, o:o + 2 * d]
    w_uvl = jnp.concatenate([w_u, w_v, w_cq, w_ckv, _rope_slot_cols(w_kr)], axis=1)
    wq = w_uq[i].reshape(Q_LORA, MLA_HEADS, QK_NOPE + QK_ROPE)
    wq = jnp.concatenate([wq[..., :QK_NOPE], _rope_slot_cols(wq[..., QK_NOPE:])], axis=-1)
    wkv = w_ukv[i].reshape(KV_LORA, MLA_HEADS, QK_NOPE + V_HEAD)
    bs_tile = jnp.repeat(gm_bs[i].T, GM_GROUP_W, axis=1)
    return dict(
        w_uvl=w_uvl.astype(BF16), w_g=w_g.astype(BF16),
        w_qt=wq.reshape(Q_LORA, MLA_HEADS * HEAD_PAD).T.astype(BF16),
        w_uk=wkv[..., :QK_NOPE].reshape(KV_LORA, MLA_HEADS * QK_NOPE).astype(BF16),
        w_uvt=wkv[..., QK_NOPE:].reshape(KV_LORA, MLA_HEADS * V_HEAD).T.astype(BF16),
        ws=gm_ws[i].astype(BF16), bs_tile=bs_tile,
        w_a=w_branch_a[i].astype(BF16), w_b=w_branch_b[i].astype(BF16), w_o=w_out[i].astype(BF16),
        w_pi=w_ple_in[i].astype(BF16), w_pg=w_ple_gate[i].astype(BF16),
    )


def kernel(x, p, positions, norm_pre_mix, norm_post_mix, norm_pre_ffn, norm_post_ffn, w_in, gm_ln_g, gm_ln_b, gm_ws, gm_bs, mla_q_norm, w_uq, mla_kv_norm, w_ukv, w_branch_a, w_branch_b, w_out, w_dense_gu, w_dense_down, w_router, w_exp_gu, w_exp_down, w_ple_in, w_ple_gate, ple_norm):
    batch, seq, d = x.shape
    depth = p.shape[0]
    n = batch * seq
    scale = (QK_NOPE + QK_ROPE) ** -0.5 * LOG2_E
    cos_t, sin_t, cos_tt, sin_tt = _rope_tables(positions)
    h = x.reshape(n, d)

    for li in range(depth):
        wts = _prep_layer(li, w_in, gm_ws, gm_bs, w_uq, w_ukv, w_branch_a, w_branch_b, w_out,
                          w_ple_in, w_ple_gate)
        hn, u, v, c_q, c_kv, k_r = _in_proj(h, norm_pre_mix[li], wts["w_uvl"], gm_ln_g[li], gm_ln_b[li],
                                            mla_q_norm[li], mla_kv_norm[li], cos_t, sin_t)
        (gates,) = _mm(hn, wts["w_g"], _epi_sigmoid, tm=ROWS_STREAM, tn=COLS_STREAM, outs=[(2 * d, BF16, "tile")],
                       name="in_proj_gates")
        qt = _q_proj(c_q, wts["w_qt"], cos_tt, sin_tt, scale)
        k, vt = _kv_proj(c_kv, wts["w_uk"], wts["w_uvt"], k_r)
        attn = _attention(qt, k, vt, batch, seq, F32)
        merged = _gmlp_branch_merge(u, v, wts["ws"], wts["bs_tile"], wts["w_a"], attn, wts["w_b"], gates)
        (h,) = _mm(merged, wts["w_o"], _epi_norm_residual, tm=ROWS_RESIDENT, tn=d,
                   extras=[(h, "rows"), (norm_post_mix[li].reshape(1, d), "vec")],
                   outs=[(d, F32, "tile")], name="out_proj")
        p_li = p[li].reshape(n, -1)
        if li % 2 == 0:
            hn = _norm_bf16(h, norm_pre_ffn[li])
            f = _dense_ffn(hn, w_dense_gu[li // 2].astype(BF16), w_dense_down[li // 2].astype(BF16))
            h = _ffn_post_ple(h, f, norm_post_ffn[li], p_li, wts["w_pi"], wts["w_pg"], ple_norm[li])
        else:
            route_i, route_w, counts = _router(h, norm_pre_ffn[li], w_router[li // 2])
            dest, tile_e, tile_groups, n_used, n_slots = _slot_tables(route_i, counts, n)
            xs = _dispatch(h, norm_pre_ffn[li], dest, n_slots)
            ys = _moe_ffn(xs, tile_e, tile_groups, n_used, w_exp_gu[li // 2], w_exp_down[li // 2])
            h = _moe_combine_post_ple(h, ys, dest, route_w, norm_post_ffn[li], p_li, wts["w_pi"],
                                      wts["w_pg"], ple_norm[li])
    return h.reshape(batch, seq, d)
```

```python
import functools

import jax
import jax.numpy as jnp
from jax import lax
from jax.experimental import pallas as pl
from jax.experimental.pallas import tpu as pltpu

F32 = jnp.float32
BF16 = jnp.bfloat16

GM_GROUPS = 8
GM_GROUP_W = 128
GM_W = GM_GROUPS * GM_GROUP_W
CHUNK = 128
MLA_HEADS = 8
Q_LORA = 512
KV_LORA = 256
QK_NOPE = 128
QK_ROPE = 64
V_HEAD = 128
ROPE_THETA = 10000.0
N_EXPERTS = 8
TOP_K = 2
MOE_BLOCK = 512
MOE_TILE = 2 * MOE_BLOCK
MOE_GROUP = MOE_TILE // 8
EPS = 1e-6
LOG2_E = 1.4426950408889634

LANES = 128
HEAD_PAD = 2 * LANES
BF16_SUBLANES = 16
V_ROWS = V_HEAD + BF16_SUBLANES

ATTN_TQ = 2048
ATTN_TK = 512
ROWS_RESIDENT = 512
ROWS_STREAM = 1024
COLS_STREAM = 1024
FF_TILE = 512
ROWS_MIX = 512
ROWS_COMBINE = 256
V7X_VMEM_BYTES = 64 * 1024 * 1024
VMEM_LIMIT = V7X_VMEM_BYTES - 8 * 1024 * 1024


def _params(*sem):
    return pltpu.CompilerParams(dimension_semantics=sem, vmem_limit_bytes=VMEM_LIMIT)


def _tile(n, want):
    t = min(n, want)
    while n % t:
        t //= 2
    return t


def _rms(x, g):
    return x * lax.rsqrt(jnp.mean(x * x, axis=-1, keepdims=True) + EPS) * g


def _rope_table_kernel(pos_c_ref, pos_r_ref, inv_r_ref, sgn_r_ref, inv_c_ref, sgn_c_ref,
                       cos_ref, sin_ref, cos_t_ref, sin_t_ref):
    ang = pos_c_ref[...].astype(F32) * inv_r_ref[...]
    cos_ref[...] = jnp.cos(ang) * jnp.abs(sgn_r_ref[...])
    sin_ref[...] = jnp.sin(ang) * sgn_r_ref[...]
    ang_t = inv_c_ref[...] * pos_r_ref[...].astype(F32)
    cos_t_ref[...] = jnp.cos(ang_t) * jnp.abs(sgn_c_ref[...])
    sin_t_ref[...] = jnp.sin(ang_t) * sgn_c_ref[...]


def _rope_tables(positions):
    n = positions.size
    tm = _tile(n, ROWS_STREAM)
    half = QK_ROPE // 2
    inv = ROPE_THETA ** (-jnp.arange(0, QK_ROPE, 2, dtype=F32) / QK_ROPE)
    z = jnp.zeros((half,), F32)
    inv_slot = jnp.concatenate([inv, z, inv, z])
    one = jnp.ones((half,), F32)
    sgn_slot = jnp.concatenate([-one, z, one, z])
    row = pl.BlockSpec((1, LANES), lambda i: (0, 0))
    col = pl.BlockSpec((LANES, 1), lambda i: (0, 0))
    tab = pl.BlockSpec((tm, LANES), lambda i: (i, 0))
    tab_t = pl.BlockSpec((LANES, tm), lambda i: (0, i))
    return pl.pallas_call(
        _rope_table_kernel,
        grid=(n // tm,),
        in_specs=[pl.BlockSpec((tm, 1), lambda i: (i, 0)), pl.BlockSpec((1, tm), lambda i: (0, i)),
                  row, row, col, col],
        out_specs=[tab, tab, tab_t, tab_t],
        out_shape=[jax.ShapeDtypeStruct((n, LANES), F32)] * 2 + [jax.ShapeDtypeStruct((LANES, n), F32)] * 2,
        compiler_params=_params("parallel"),
        name="rope_tables",
    )(positions.reshape(n, 1), positions.reshape(1, n), inv_slot.reshape(1, LANES),
      sgn_slot.reshape(1, LANES), inv_slot.reshape(LANES, 1), sgn_slot.reshape(LANES, 1))


def _rope_slot(x, cos_t, sin_t):
    return x * cos_t + pltpu.roll(x, LANES // 2, axis=1) * sin_t


def _norm_kernel(h_ref, g_ref, o_ref):
    o_ref[...] = _rms(h_ref[...], g_ref[...]).astype(o_ref.dtype)


def _norm_bf16(h, g):
    n, d = h.shape
    tm = _tile(n, ROWS_RESIDENT)
    return pl.pallas_call(
        _norm_kernel,
        grid=(n // tm,),
        in_specs=[pl.BlockSpec((tm, d), lambda i: (i, 0)), pl.BlockSpec((1, d), lambda i: (0, 0))],
        out_specs=pl.BlockSpec((tm, d), lambda i: (i, 0)),
        out_shape=jax.ShapeDtypeStruct((n, d), BF16),
        compiler_params=_params("parallel"),
        name="rmsnorm",
    )(h, g.reshape(1, d))


ROUTE_E, ROUTE_RANK = 0, TOP_K


def _router_kernel(h_ref, g_ref, wr_ref, ri_ref, rw_ref, cnt_ref, carry):
    @pl.when(pl.program_id(0) == 0)
    def _():
        carry[...] = jnp.zeros_like(carry)

    y = _rms(h_ref[...], g_ref[...])
    w = wr_ref[...]
    y_hi, w_hi = y.astype(BF16), w.astype(BF16)
    y_lo = (y - y_hi.astype(F32)).astype(BF16)
    w_lo = (w - w_hi.astype(F32)).astype(BF16)
    logits = (jnp.dot(y_hi, w_hi, preferred_element_type=F32)
              + jnp.dot(y_lo, w_hi, preferred_element_type=F32)
              + jnp.dot(y_hi, w_lo, preferred_element_type=F32))
    tm = logits.shape[0]
    lane = lax.broadcasted_iota(jnp.int32, logits.shape, 1)
    lane_f = lane.astype(F32)
    lg = jnp.where(lane < N_EXPERTS, logits, -jnp.inf)
    tops, hots = [], []
    for _ in range(TOP_K):
        mx = jnp.max(lg, axis=1, keepdims=True)
        idx = jnp.min(jnp.where(lg == mx, lane_f, float(LANES)), axis=1, keepdims=True)
        hot = lane_f == idx
        tops.append((mx, idx))
        hots.append(hot)
        lg = jnp.where(hot, -jnp.inf, lg)
    e1 = jnp.exp(tops[1][0] - tops[0][0])
    gates = (1.0 / (1.0 + e1), e1 / (1.0 + e1))
    hot_any = hots[0].astype(F32) + hots[1].astype(F32)
    tri = (lax.broadcasted_iota(jnp.int32, (tm, tm), 0) > lax.broadcasted_iota(jnp.int32, (tm, tm), 1))
    before = jnp.dot(tri.astype(BF16), hot_any.astype(BF16), preferred_element_type=F32) + carry[...]
    ranks = [jnp.sum(jnp.where(hot, before, 0.0), axis=1, keepdims=True) for hot in hots]
    carry[...] += jnp.sum(hot_any, axis=0, keepdims=True)
    cnt_ref[...] = carry[...]
    rec = jnp.zeros(logits.shape, jnp.int32)
    wrec = jnp.zeros(logits.shape, F32)
    for k in range(TOP_K):
        rec = jnp.where(lane == ROUTE_E + k, tops[k][1].astype(jnp.int32), rec)
        rec = jnp.where(lane == ROUTE_RANK + k, ranks[k].astype(jnp.int32), rec)
        wrec = jnp.where(lane == k, gates[k], wrec)
    ri_ref[...] = rec
    rw_ref[...] = wrec


def _router(h, g, w_router):
    n, d = h.shape
    tm = _tile(n, ROWS_RESIDENT)
    wr = jnp.zeros((d, LANES), F32).at[:, :N_EXPERTS].set(w_router)
    rec = pl.BlockSpec((tm, LANES), lambda i: (i, 0))
    return pl.pallas_call(
        _router_kernel,
        grid=(n // tm,),
        in_specs=[pl.BlockSpec((tm, d), lambda i: (i, 0)), pl.BlockSpec((1, d), lambda i: (0, 0)),
                  pl.BlockSpec((d, LANES), lambda i: (0, 0))],
        out_specs=[rec, rec, pl.BlockSpec((1, LANES), lambda i: (0, 0))],
        out_shape=[jax.ShapeDtypeStruct((n, LANES), jnp.int32), jax.ShapeDtypeStruct((n, LANES), F32),
                   jax.ShapeDtypeStruct((1, LANES), F32)],
        scratch_shapes=[pltpu.VMEM((1, LANES), F32)],
        compiler_params=_params("arbitrary"),
        name="moe_router",
    )(h, g.reshape(1, d), wr)


def _pack_bf16_pairs(y):
    half = y.shape[1] // 2
    lo = lax.bitcast_convert_type(y[:, :half].astype(BF16).astype(F32), jnp.uint32)
    hi = lax.bitcast_convert_type(y[:, half:].astype(BF16).astype(F32), jnp.uint32)
    return hi | (lo >> 16)


def _unpack_bf16_pairs(w):
    lo = lax.bitcast_convert_type(w << 16, F32).astype(BF16)
    hi = lax.bitcast_convert_type(w & jnp.uint32(0xFFFF0000), F32).astype(BF16)
    return jnp.concatenate([lo, hi], axis=1)


def _row_copy_wait(src_row, dst_row, sem, count):
    def body(_, c):
        pltpu.make_async_copy(src_row, dst_row, sem).wait()
        return c
    lax.fori_loop(0, count, body, 0, unroll=16)


def _dispatch_kernel(dest_ref, h_ref, g_ref, xs_zero_ref, xs_ref, pbuf, sem):
    del xs_zero_ref
    i = pl.program_id(0)
    last = pl.num_programs(0) - 1
    slot = i % 2
    tm = h_ref.shape[0]

    def wait_slot(s):
        _row_copy_wait(pbuf.at[s, pl.ds(0, 1)], xs_ref.at[pl.ds(0, 1)], sem.at[s], TOP_K * tm)

    @pl.when(i >= 2)
    def _():
        wait_slot(slot)

    pbuf[slot] = _pack_bf16_pairs(_rms(h_ref[...], g_ref[...]))

    def send(r, c):
        for k in range(TOP_K):
            dst = dest_ref[0, 0, TOP_K * r + k]
            pltpu.make_async_copy(pbuf.at[slot, pl.ds(r, 1)], xs_ref.at[pl.ds(dst, 1)], sem.at[slot]).start()
        return c
    lax.fori_loop(0, tm, send, 0, unroll=8)

    @pl.when(i == last)
    def _():
        @pl.when(i >= 1)
        def _():
            wait_slot(1 - slot)
        wait_slot(slot)


def _dispatch(h, g, dest, n_slots):
    n, d = h.shape
    tm = _tile(n, ROWS_RESIDENT)
    nt = n // tm
    xs0 = jnp.zeros((n_slots, d // 2), jnp.uint32)
    return pl.pallas_call(
        _dispatch_kernel,
        grid=(nt,),
        in_specs=[pl.BlockSpec((1, 1, TOP_K * tm), lambda i: (i, 0, 0), memory_space=pltpu.SMEM),
                  pl.BlockSpec((tm, d), lambda i: (i, 0)), pl.BlockSpec((1, d), lambda i: (0, 0)),
                  pl.BlockSpec(memory_space=pl.ANY)],
        out_specs=pl.BlockSpec(memory_space=pl.ANY),
        out_shape=jax.ShapeDtypeStruct((n_slots, d // 2), jnp.uint32),
        scratch_shapes=[pltpu.VMEM((2, tm, d // 2), jnp.uint32), pltpu.SemaphoreType.DMA((2,))],
        input_output_aliases={3: 0},
        compiler_params=_params("arbitrary"),
        name="moe_dispatch",
    )(dest.reshape(nt, 1, TOP_K * tm), h, g.reshape(1, d), xs0)


def _mm_kernel(epi, n_extra, x_ref, w_ref, *rest):
    extras = rest[:n_extra]
    outs = rest[n_extra:]
    acc = jnp.dot(x_ref[...], w_ref[...], preferred_element_type=F32)
    res = epi(acc, *extras)
    if not isinstance(res, (tuple, list)):
        res = (res,)
    for o, r in zip(outs, res):
        o[...] = r.astype(o.dtype)


def _mm(x, w, epi, *, tm, tn, extras=(), outs, name):
    m, k = x.shape
    n = w.shape[1]
    tm = _tile(m, tm)
    tn = _tile(n, tn)

    def spec(ncols, kind):
        if kind == "tile":
            return pl.BlockSpec((tm, tn), lambda j, i: (i, j))
        if kind == "rows":
            return pl.BlockSpec((tm, ncols), lambda j, i: (i, 0))
        assert kind == "vec", kind
        return pl.BlockSpec((1, ncols), lambda j, i: (0, 0))

    in_specs = [pl.BlockSpec((tm, k), lambda j, i: (i, 0)), pl.BlockSpec((k, tn), lambda j, i: (0, j))]
    in_specs += [spec(a.shape[1], kind) for a, kind in extras]
    return pl.pallas_call(
        functools.partial(_mm_kernel, epi, len(extras)),
        grid=(n // tn, m // tm),
        in_specs=in_specs,
        out_specs=[spec(nc, kind) for nc, _, kind in outs],
        out_shape=[jax.ShapeDtypeStruct((m, nc), dt) for nc, dt, _ in outs],
        compiler_params=_params("parallel", "parallel"),
        name=name,
    )(x, w, *[a for a, _ in extras])


def _epi_gelu(acc):
    return jax.nn.gelu(acc)


def _epi_gelu_group_ln(acc, g_ref, b_ref):
    y = jax.nn.gelu(acc)
    g = g_ref[...]
    b = b_ref[...]
    cols = []
    for gi in range(acc.shape[1] // GM_GROUP_W):
        sl = slice(gi * GM_GROUP_W, (gi + 1) * GM_GROUP_W)
        yg = y[:, sl]
        mu = jnp.mean(yg, axis=-1, keepdims=True)
        dlt = yg - mu
        var = jnp.mean(dlt * dlt, axis=-1, keepdims=True)
        cols.append(dlt * lax.rsqrt(var + EPS) * g[:, sl] + b[:, sl])
    return jnp.concatenate(cols, axis=1)


def _epi_sigmoid(acc):
    return jax.nn.sigmoid(acc)


def _epi_latent(acc, qn_ref, kvn_ref, cos_ref, sin_ref):
    c_q = _rms(acc[:, :Q_LORA], qn_ref[...])
    c_kv = _rms(acc[:, Q_LORA:Q_LORA + KV_LORA], kvn_ref[...])
    k_r = _rope_slot(acc[:, Q_LORA + KV_LORA:], cos_ref[...], sin_ref[...])
    return c_q, c_kv, k_r


def _in_proj_kernel(h_ref, gn_ref, w_ref, lg_ref, lb_ref, qn_ref, kvn_ref, cos_ref, sin_ref,
                    hn_ref, u_ref, v_ref, cq_ref, ckv_ref, kr_ref):
    hn = _rms(h_ref[...], gn_ref[...]).astype(hn_ref.dtype)
    hn_ref[...] = hn
    acc = jnp.dot(hn, w_ref[...], preferred_element_type=F32)
    u_ref[...] = _epi_gelu(acc[:, :GM_W]).astype(u_ref.dtype)
    v_ref[...] = _epi_gelu_group_ln(acc[:, GM_W:2 * GM_W], lg_ref, lb_ref).astype(v_ref.dtype)
    c_q, c_kv, k_r = _epi_latent(acc[:, 2 * GM_W:], qn_ref, kvn_ref, cos_ref, sin_ref)
    cq_ref[...] = c_q.astype(cq_ref.dtype)
    ckv_ref[...] = c_kv.astype(ckv_ref.dtype)
    kr_ref[...] = k_r.astype(kr_ref.dtype)


def _in_proj(h, g_norm, w_uvl, ln_g, ln_b, q_norm, kv_norm, cos_t, sin_t):
    n, d = h.shape
    tm = _tile(n, ROWS_RESIDENT)
    vec = lambda a: (a.reshape(1, -1), pl.BlockSpec((1, a.size), lambda i: (0, 0)))
    rows = lambda w: pl.BlockSpec((tm, w), lambda i: (i, 0))
    vecs = [vec(g_norm), vec(ln_g), vec(ln_b), vec(q_norm), vec(kv_norm)]
    widths = [d, GM_W, GM_W, Q_LORA, KV_LORA, LANES]
    return pl.pallas_call(
        _in_proj_kernel,
        grid=(n // tm,),
        in_specs=[rows(d), vecs[0][1],
                  pl.BlockSpec(w_uvl.shape, lambda i: (0, 0), pipeline_mode=pl.Buffered(1)),
                  vecs[1][1], vecs[2][1], vecs[3][1], vecs[4][1], rows(LANES), rows(LANES)],
        out_specs=[rows(w) for w in widths],
        out_shape=[jax.ShapeDtypeStruct((n, w), BF16) for w in widths],
        compiler_params=_params("parallel"),
        name="in_proj_uvl",
    )(h, vecs[0][0], w_uvl, vecs[1][0], vecs[2][0], vecs[3][0], vecs[4][0], cos_t, sin_t)


def _q_proj_kernel(scale, c_ref, w_ref, cos_ref, sin_ref, o_ref):
    acc = lax.dot_general(w_ref[...], c_ref[...], (((1,), (1,)), ((), ())),
                          preferred_element_type=F32)
    cos_t = cos_ref[...] * scale
    sin_t = sin_ref[...] * scale
    half = LANES // 2
    for hd in range(MLA_HEADS):
        base = hd * HEAD_PAD
        o_ref[base:base + QK_NOPE, :] = (acc[base:base + QK_NOPE, :] * scale).astype(o_ref.dtype)
        r = acc[base + QK_NOPE:base + HEAD_PAD, :]
        rot = jnp.concatenate([r[half:, :], r[:half, :]], axis=0)
        o_ref[base + QK_NOPE:base + HEAD_PAD, :] = (r * cos_t + rot * sin_t).astype(o_ref.dtype)


def _q_proj(c_q, w_qt, cos_tt, sin_tt, scale):
    n, r = c_q.shape
    qw = w_qt.shape[0]
    tm = _tile(n, ROWS_RESIDENT)
    tab = pl.BlockSpec((LANES, tm), lambda i: (0, i))
    return pl.pallas_call(
        functools.partial(_q_proj_kernel, scale),
        grid=(n // tm,),
        in_specs=[pl.BlockSpec((tm, r), lambda i: (i, 0)), pl.BlockSpec((qw, r), lambda i: (0, 0)), tab, tab],
        out_specs=pl.BlockSpec((qw, tm), lambda i: (0, i)),
        out_shape=jax.ShapeDtypeStruct((qw, n), BF16),
        compiler_params=_params("parallel"),
        name="mla_q_proj",
    )(c_q, w_qt, cos_tt, sin_tt)


def _kv_kernel(c_ref, wk_ref, wvt_ref, kr_ref, k_ref, vt_ref):
    c = c_ref[...]
    kn = jnp.dot(c, wk_ref[...], preferred_element_type=F32).astype(k_ref.dtype)
    kr = kr_ref[...]
    for hd in range(MLA_HEADS):
        k_ref[:, hd * HEAD_PAD:hd * HEAD_PAD + QK_NOPE] = kn[:, hd * QK_NOPE:(hd + 1) * QK_NOPE]
        k_ref[:, hd * HEAD_PAD + QK_NOPE:(hd + 1) * HEAD_PAD] = kr
    vt = lax.dot_general(wvt_ref[...], c, (((1,), (1,)), ((), ())),
                         preferred_element_type=F32).astype(vt_ref.dtype)
    pad_rows = V_ROWS - V_HEAD
    ones_row = (lax.broadcasted_iota(jnp.int32, (pad_rows, vt.shape[1]), 0) == 0).astype(vt_ref.dtype)
    for hd in range(MLA_HEADS):
        vt_ref[hd * V_ROWS:hd * V_ROWS + V_HEAD, :] = vt[hd * V_HEAD:(hd + 1) * V_HEAD, :]
        vt_ref[hd * V_ROWS + V_HEAD:(hd + 1) * V_ROWS, :] = ones_row


def _kv_proj(c_kv, w_uk, w_uvt, k_rope):
    n, r = c_kv.shape
    tm = _tile(n, ROWS_RESIDENT)
    kw = MLA_HEADS * HEAD_PAD
    vw = MLA_HEADS * V_ROWS
    return pl.pallas_call(
        _kv_kernel,
        grid=(n // tm,),
        in_specs=[pl.BlockSpec((tm, r), lambda i: (i, 0)),
                  pl.BlockSpec(w_uk.shape, lambda i: (0, 0)),
                  pl.BlockSpec(w_uvt.shape, lambda i: (0, 0)),
                  pl.BlockSpec((tm, LANES), lambda i: (i, 0))],
        out_specs=[pl.BlockSpec((tm, kw), lambda i: (i, 0)), pl.BlockSpec((vw, tm), lambda i: (0, i))],
        out_shape=[jax.ShapeDtypeStruct((n, kw), BF16), jax.ShapeDtypeStruct((vw, n), BF16)],
        compiler_params=_params("parallel"),
        name="mla_kv_proj",
    )(c_kv, w_uk, w_uvt, k_rope)


def _attn_kernel(tk, p_dtype, qt_ref, k_ref, vt_ref, o_ref, s_a, s_b, p_a, p_b):
    tq = qt_ref.shape[1]
    n_chunks = k_ref.shape[0] // tk

    def scores(i, s_out):
        off = pl.multiple_of(i * tk, tk)
        s = jnp.dot(k_ref[pl.ds(off, tk), :], qt_ref[...], preferred_element_type=F32)
        s_out[...] = s
        return jnp.max(s, axis=0, keepdims=True)

    def softmax(s_in, p_out, mx, m):
        m_new = jnp.maximum(m, mx)
        x = s_in[...] - m_new
        p_out[...] = jnp.exp2(x.astype(p_dtype)).astype(p_out.dtype)
        return m_new, jnp.exp2(m - m_new)

    def values(i, p_in, alpha, acc):
        off = pl.multiple_of(i * tk, tk)
        return alpha * acc + jnp.dot(vt_ref[:, pl.ds(off, tk)], p_in[...], preferred_element_type=F32)

    def step(i, odd, carry, first=False, last=False):
        s_cur, s_nxt, p_cur, p_prv = (s_b, s_a, p_b, p_a) if odd else (s_a, s_b, p_a, p_b)
        mx, m, alpha, acc = carry
        if not first:
            acc = values(i - 1, p_prv, alpha, acc)
        mx_nxt = mx if last else scores(i + 1, s_nxt)
        m, alpha = softmax(s_cur, p_cur, mx, m)
        return mx_nxt, m, alpha, acc

    carry = (scores(0, s_a), jnp.full((1, tq), -jnp.inf, F32), jnp.ones((1, tq), F32),
             jnp.zeros((vt_ref.shape[0], tq), F32))
    for i in range(n_chunks):
        carry = step(i, i % 2 == 1, carry, first=i == 0, last=i == n_chunks - 1)
    _, _, alpha, acc = carry
    acc = values(n_chunks - 1, p_b if n_chunks % 2 == 0 else p_a, alpha, acc)
    o_ref[...] = jnp.transpose(acc[:V_HEAD, :] / acc[V_HEAD:V_HEAD + 1, :]).astype(o_ref.dtype)


def _attention(qt, k, vt, batch, seq, p_dtype):
    n = k.shape[0]
    tq = _tile(seq, ATTN_TQ)
    tk = _tile(seq // 2, ATTN_TK)
    assert (seq // tk) % 2 == 0
    nq = seq // tq
    return pl.pallas_call(
        functools.partial(_attn_kernel, tk, p_dtype),
        grid=(batch, MLA_HEADS, nq),
        scratch_shapes=[pltpu.VMEM((tk, tq), F32), pltpu.VMEM((tk, tq), F32),
                        pltpu.VMEM((tk, tq), BF16), pltpu.VMEM((tk, tq), BF16)],
        in_specs=[pl.BlockSpec((HEAD_PAD, tq), lambda b, h, i: (h, b * nq + i)),
                  pl.BlockSpec((seq, HEAD_PAD), lambda b, h, i: (b, h)),
                  pl.BlockSpec((V_ROWS, seq), lambda b, h, i: (h, b))],
        out_specs=pl.BlockSpec((tq, V_HEAD), lambda b, h, i: (b * nq + i, h)),
        out_shape=jax.ShapeDtypeStruct((n, MLA_HEADS * V_HEAD), BF16),
        compiler_params=_params("parallel", "parallel", "arbitrary"),
        name="mla_attention",
    )(qt, k, vt)


def _gmlp_merge_kernel(u_ref, v_ref, ws_ref, bs_ref, wa_ref, b_ref, wb_ref, ga_ref, gb_ref, wo_ref,
                       h_ref, gpost_ref, o_ref, gm):
    n_chunks = u_ref.shape[0] // CHUNK
    for c in range(n_chunks):
        rows = slice(c * CHUNK, (c + 1) * CHUNK)
        for g in range(GM_GROUPS):
            cols = slice(g * GM_GROUP_W, (g + 1) * GM_GROUP_W)
            mixed = jnp.dot(ws_ref[g], v_ref[rows, cols], preferred_element_type=F32)
            gm[rows, cols] = (u_ref[rows, cols].astype(F32) * (mixed + bs_ref[:, cols])).astype(gm.dtype)
    ya = jnp.dot(gm[...], wa_ref[...], preferred_element_type=F32)
    yb = jnp.dot(b_ref[...], wb_ref[...], preferred_element_type=F32)
    merged = (ga_ref[...].astype(F32) * ya + gb_ref[...].astype(F32) * yb).astype(BF16)
    mixed = jnp.dot(merged, wo_ref[...], preferred_element_type=F32)
    o_ref[...] = h_ref[...] + _rms(mixed, gpost_ref[...])


def _gmlp_branch_merge(u, v, ws, bs_tile, wa, attn, wb, gates, wo, h, g_post):
    n, w = u.shape
    kb = attn.shape[1]
    d = wa.shape[1]
    tm = _tile(n, ROWS_MIX)
    rows = lambda width, j=0: pl.BlockSpec((tm, width), lambda i: (i, j))
    resident = lambda a: pl.BlockSpec(a.shape, lambda i: (0,) * a.ndim, pipeline_mode=pl.Buffered(1))
    return pl.pallas_call(
        _gmlp_merge_kernel,
        grid=(n // tm,),
        in_specs=[rows(w), rows(w), resident(ws), resident(bs_tile), resident(wa), rows(kb), resident(wb),
                  rows(d, 0), rows(d, 1), resident(wo), rows(d), pl.BlockSpec((1, d), lambda i: (0, 0))],
        out_specs=rows(d),
        out_shape=jax.ShapeDtypeStruct((n, d), F32),
        scratch_shapes=[pltpu.VMEM((tm, w), BF16)],
        compiler_params=_params("parallel"),
        name="gmlp_merge_out_proj",
    )(u, v, ws, bs_tile, wa, attn, wb, gates, gates, wo, h, g_post.reshape(1, d))


def _swiglu_step(x, wg_ref, wu_ref, wd_ref):
    g = jnp.dot(x, wg_ref[...], preferred_element_type=F32)
    u = jnp.dot(x, wu_ref[...], preferred_element_type=F32)
    a = (jax.nn.silu(g) * u).astype(x.dtype)
    return jnp.dot(a, wd_ref[...], preferred_element_type=F32)


def _dense_ffn_kernel(x_ref, wg_ref, wu_ref, wd_ref, o_ref):
    f = pl.program_id(1)
    @pl.when(f == 0)
    def _():
        o_ref[...] = jnp.zeros_like(o_ref)

    o_ref[...] += _swiglu_step(x_ref[...], wg_ref, wu_ref, wd_ref)


def _dense_ffn(x, w_gu, w_down):
    n, d = x.shape
    dff = w_down.shape[0]
    tm = _tile(n, ROWS_STREAM)
    tf = _tile(dff, FF_TILE)
    nf = dff // tf
    return pl.pallas_call(
        _dense_ffn_kernel,
        grid=(n // tm, nf),
        in_specs=[pl.BlockSpec((tm, d), lambda i, f: (i, 0)),
                  pl.BlockSpec((d, tf), lambda i, f: (0, f)),
                  pl.BlockSpec((d, tf), lambda i, f: (0, f + nf)),
                  pl.BlockSpec((tf, d), lambda i, f: (f, 0))],
        out_specs=pl.BlockSpec((tm, d), lambda i, f: (i, 0)),
        out_shape=jax.ShapeDtypeStruct((n, d), F32),
        compiler_params=_params("parallel", "arbitrary"),
        name="dense_swiglu",
    )(x, w_gu, w_gu, w_down)


def _moe_ffn_kernel(te_ref, th_ref, nu_ref, x_ref, wg_ref, wu_ref, wd_ref, o_ref):
    t = pl.program_id(0)
    f = pl.program_id(1)
    n_grp = th_ref[t]
    full_grps = MOE_TILE // MOE_GROUP
    half_grps = full_grps // 2

    @pl.when(f == 0)
    def _():
        o_ref[...] = jnp.zeros_like(o_ref)

    @pl.when(n_grp > 0)
    def _():
        wg = wg_ref[...].astype(BF16)
        wu = wu_ref[...].astype(BF16)
        wd = wd_ref[...].astype(BF16)

        def ffn_rows(start, size):
            rows = pl.ds(start, size)
            x = _unpack_bf16_pairs(x_ref[rows, :])
            g = jnp.dot(x, wg, preferred_element_type=F32)
            u = jnp.dot(x, wu, preferred_element_type=F32)
            y = jnp.dot((jax.nn.silu(g) * u).astype(BF16), wd, preferred_element_type=F32)
            o_ref[rows, :] += y

        @pl.when(n_grp == full_grps)
        def _():
            ffn_rows(0, MOE_TILE)

        @pl.when(n_grp < full_grps)
        def _():
            has_half = n_grp >= half_grps
            rem = n_grp - jnp.where(has_half, half_grps, 0)
            base = jnp.where(has_half, MOE_TILE // 2, 0)

            @pl.when(has_half)
            def _():
                ffn_rows(0, MOE_TILE // 2)

            @pl.when(rem >= 2)
            def _():
                ffn_rows(pl.multiple_of(base, MOE_GROUP), 2 * MOE_GROUP)

            @pl.when(rem % 2 == 1)
            def _():
                ffn_rows(pl.multiple_of(base + jnp.where(rem >= 2, 2 * MOE_GROUP, 0), MOE_GROUP), MOE_GROUP)


def _moe_ffn(xs, tile_e, tile_groups, n_used, w_gu, w_down):
    assert MOE_TILE == 8 * MOE_GROUP
    n_slots = xs.shape[0]
    d = w_down.shape[2]
    dff = w_down.shape[1]
    tm = MOE_TILE
    tf = _tile(dff, FF_TILE)
    nf = dff // tf

    def _tt(t, nu):
        return jnp.minimum(t, jnp.maximum(nu[0] - 1, 0))

    def _e(t, te, nu):
        return te[_tt(t, nu)]

    def _ff(t, f, nu):
        return jnp.where(t < nu[0], f, nf - 1)

    grid_spec = pltpu.PrefetchScalarGridSpec(
        num_scalar_prefetch=3,
        grid=(n_slots // tm, nf),
        in_specs=[
            pl.BlockSpec((tm, d // 2), lambda t, f, te, th, nu: (_tt(t, nu), 0)),
            pl.BlockSpec((None, d, tf), lambda t, f, te, th, nu: (_e(t, te, nu), 0, _ff(t, f, nu))),
            pl.BlockSpec((None, d, tf), lambda t, f, te, th, nu: (_e(t, te, nu), 0, _ff(t, f, nu) + nf)),
            pl.BlockSpec((None, tf, d), lambda t, f, te, th, nu: (_e(t, te, nu), _ff(t, f, nu), 0)),
        ],
        out_specs=pl.BlockSpec((tm, d), lambda t, f, te, th, nu: (t, 0), pipeline_mode=pl.Buffered(1)),
    )
    return pl.pallas_call(
        _moe_ffn_kernel,
        grid_spec=grid_spec,
        out_shape=jax.ShapeDtypeStruct((n_slots, d), F32),
        compiler_params=_params("arbitrary", "arbitrary"),
        name="moe_swiglu",
    )(tile_e, tile_groups, n_used, xs, w_gu, w_gu, w_down)


def _post_ple_math(h, f, gf_ref, p_ref, wp_ref, wg_ref, gp_ref, o_ref):
    h = h + _rms(f, gf_ref[...])
    e = jnp.dot(p_ref[...].astype(BF16), wp_ref[...], preferred_element_type=F32)
    gate = jax.nn.sigmoid(jnp.dot(h.astype(BF16), wg_ref[...], preferred_element_type=F32))
    o_ref[...] = h + _rms(gate * e, gp_ref[...])


def _ple_kernel(h_ref, f_ref, gf_ref, p_ref, wp_ref, wg_ref, gp_ref, o_ref):
    _post_ple_math(h_ref[...], f_ref[...], gf_ref, p_ref, wp_ref, wg_ref, gp_ref, o_ref)


def _ple_specs(n, d, pd, tm):
    blk = pl.BlockSpec((tm, d), lambda i: (i, 0))
    row = pl.BlockSpec((1, d), lambda i: (0, 0))
    tail = [row, pl.BlockSpec((tm, pd), lambda i: (i, 0)), pl.BlockSpec((pd, d), lambda i: (0, 0)),
            pl.BlockSpec((d, d), lambda i: (0, 0)), row]
    return blk, tail


def _ffn_post_ple(h, f, g_post, p, w_ple_in, w_ple_gate, g_ple):
    n, d = h.shape
    tm = _tile(n, ROWS_RESIDENT)
    blk, tail = _ple_specs(n, d, p.shape[1], tm)
    return pl.pallas_call(
        _ple_kernel,
        grid=(n // tm,),
        in_specs=[blk, blk] + tail,
        out_specs=blk,
        out_shape=jax.ShapeDtypeStruct((n, d), F32),
        compiler_params=_params("parallel"),
        name="ffn_post_ple",
    )(h, f, g_post.reshape(1, d), p, w_ple_in, w_ple_gate, g_ple.reshape(1, d))


def _ple_combine_kernel(d_cur_ref, d_nxt_ref, h_ref, rw_ref, gf_ref, p_ref, wp_ref, wg_ref, gp_ref,
                        ys_ref, o_ref, gbuf, sem):
    i = pl.program_id(0)
    last = pl.num_programs(0) - 1
    slot = i % 2
    tm = h_ref.shape[0]

    def fetch(d_ref, s):
        def body(r, c):
            for k in range(TOP_K):
                src = d_ref[0, 0, TOP_K * r + k]
                pltpu.make_async_copy(ys_ref.at[pl.ds(src, 1)], gbuf.at[s, k, pl.ds(r, 1)], sem.at[s]).start()
            return c
        lax.fori_loop(0, tm, body, 0, unroll=8)

    @pl.when(i == 0)
    def _():
        fetch(d_cur_ref, 0)

    @pl.when(i < last)
    def _():
        fetch(d_nxt_ref, 1 - slot)

    _row_copy_wait(ys_ref.at[pl.ds(0, 1)], gbuf.at[slot, 0, pl.ds(0, 1)], sem.at[slot], TOP_K * tm)
    f = rw_ref[:, 0:1] * gbuf[slot, 0]
    for k in range(1, TOP_K):
        f = f + rw_ref[:, k:k + 1] * gbuf[slot, k]
    _post_ple_math(h_ref[...], f, gf_ref, p_ref, wp_ref, wg_ref, gp_ref, o_ref)


def _moe_combine_post_ple(h, ys, dest, route_w, g_post, p, w_ple_in, w_ple_gate, g_ple):
    n, d = h.shape
    tm = _tile(n, ROWS_COMBINE)
    nt = n // tm
    blk, tail = _ple_specs(n, d, p.shape[1], tm)
    dest3 = dest.reshape(nt, 1, TOP_K * tm)
    idx = functools.partial(pl.BlockSpec, (1, 1, TOP_K * tm), memory_space=pltpu.SMEM)
    return pl.pallas_call(
        _ple_combine_kernel,
        grid=(nt,),
        in_specs=[idx(lambda i: (i, 0, 0)), idx(lambda i: (jnp.minimum(i + 1, nt - 1), 0, 0)),
                  blk, pl.BlockSpec((tm, LANES), lambda i: (i, 0))] + tail
                 + [pl.BlockSpec(memory_space=pl.ANY)],
        out_specs=blk,
        out_shape=jax.ShapeDtypeStruct((n, d), F32),
        scratch_shapes=[pltpu.VMEM((2, TOP_K, tm, d), F32), pltpu.SemaphoreType.DMA((2,))],
        compiler_params=_params("arbitrary"),
        name="moe_combine_post_ple",
    )(dest3, dest3, h, route_w, g_post.reshape(1, d), p, w_ple_in, w_ple_gate, g_ple.reshape(1, d), ys)


def _slot_tables(route_i, counts_f, n_tok):
    a = n_tok * TOP_K
    counts = counts_f[0, :N_EXPERTS].astype(jnp.int32)
    padded = (counts + MOE_TILE - 1) // MOE_TILE * MOE_TILE
    pad_end = jnp.cumsum(padded)
    pad_start = pad_end - padded
    experts = route_i[:, ROUTE_E:ROUTE_E + TOP_K]
    dest = pad_start[experts] + route_i[:, ROUTE_RANK:ROUTE_RANK + TOP_K]
    n_tiles = -(-a // MOE_TILE) + N_EXPERTS
    tile_start = jnp.arange(n_tiles, dtype=jnp.int32) * MOE_TILE
    tile_e = jnp.minimum(jnp.sum((pad_end[None, :] <= tile_start[:, None]).astype(jnp.int32), axis=1),
                         N_EXPERTS - 1)
    rows = jnp.clip(pad_start[tile_e] + counts[tile_e] - tile_start, 0, MOE_TILE)
    tile_groups = ((rows + MOE_GROUP - 1) // MOE_GROUP).astype(jnp.int32)
    n_used = (pad_end[-1] // MOE_TILE).astype(jnp.int32).reshape(1)
    return dest.astype(jnp.int32), tile_e, tile_groups, n_used, n_tiles * MOE_TILE


def _rope_slot_cols(w):
    half = QK_ROPE // 2
    z = jnp.zeros(w.shape[:-1] + (half,), w.dtype)
    return jnp.concatenate([w[..., :half], z, w[..., half:], z], axis=-1)


def _prep_layer(i, w_in, gm_ws, gm_bs, w_uq, w_ukv, w_branch_a, w_branch_b, w_out, w_ple_in, w_ple_gate):
    d = w_in.shape[1]
    wi = w_in[i]
    o = 0
    w_u = wi[:, o:o + GM_W]; o += GM_W
    w_v = wi[:, o:o + GM_W]; o += GM_W
    w_cq = wi[:, o:o + Q_LORA]; o += Q_LORA
    w_ckv = wi[:, o:o + KV_LORA]; o += KV_LORA
    w_kr = wi[:, o:o + QK_ROPE]; o += QK_ROPE
    w_g = wi[:, o:o + 2 * d]
    w_uvl = jnp.concatenate([w_u, w_v, w_cq, w_ckv, _rope_slot_cols(w_kr)], axis=1)
    wq = w_uq[i].reshape(Q_LORA, MLA_HEADS, QK_NOPE + QK_ROPE)
    wq = jnp.concatenate([wq[..., :QK_NOPE], _rope_slot_cols(wq[..., QK_NOPE:])], axis=-1)
    wkv = w_ukv[i].reshape(KV_LORA, MLA_HEADS, QK_NOPE + V_HEAD)
    bs_tile = jnp.repeat(gm_bs[i].T, GM_GROUP_W, axis=1)
    return dict(
        w_uvl=w_uvl.astype(BF16), w_g=w_g.astype(BF16),
        w_qt=wq.reshape(Q_LORA, MLA_HEADS * HEAD_PAD).T.astype(BF16),
        w_uk=wkv[..., :QK_NOPE].reshape(KV_LORA, MLA_HEADS * QK_NOPE).astype(BF16),
        w_uvt=wkv[..., QK_NOPE:].reshape(KV_LORA, MLA_HEADS * V_HEAD).T.astype(BF16),
        ws=gm_ws[i].astype(BF16), bs_tile=bs_tile,
        w_a=w_branch_a[i].astype(BF16), w_b=w_branch_b[i].astype(BF16), w_o=w_out[i].astype(BF16),
        w_pi=w_ple_in[i].astype(BF16), w_pg=w_ple_gate[i].astype(BF16),
    )


def kernel(x, p, positions, norm_pre_mix, norm_post_mix, norm_pre_ffn, norm_post_ffn, w_in, gm_ln_g, gm_ln_b, gm_ws, gm_bs, mla_q_norm, w_uq, mla_kv_norm, w_ukv, w_branch_a, w_branch_b, w_out, w_dense_gu, w_dense_down, w_router, w_exp_gu, w_exp_down, w_ple_in, w_ple_gate, ple_norm):
    batch, seq, d = x.shape
    depth = p.shape[0]
    n = batch * seq
    scale = (QK_NOPE + QK_ROPE) ** -0.5 * LOG2_E
    cos_t, sin_t, cos_tt, sin_tt = _rope_tables(positions)
    h = x.reshape(n, d)

    for li in range(depth):
        wts = _prep_layer(li, w_in, gm_ws, gm_bs, w_uq, w_ukv, w_branch_a, w_branch_b, w_out,
                          w_ple_in, w_ple_gate)
        hn, u, v, c_q, c_kv, k_r = _in_proj(h, norm_pre_mix[li], wts["w_uvl"], gm_ln_g[li], gm_ln_b[li],
                                            mla_q_norm[li], mla_kv_norm[li], cos_t, sin_t)
        (gates,) = _mm(hn, wts["w_g"], _epi_sigmoid, tm=ROWS_STREAM, tn=COLS_STREAM, outs=[(2 * d, BF16, "tile")],
                       name="in_proj_gates")
        qt = _q_proj(c_q, wts["w_qt"], cos_tt, sin_tt, scale)
        k, vt = _kv_proj(c_kv, wts["w_uk"], wts["w_uvt"], k_r)
        attn = _attention(qt, k, vt, batch, seq, F32)
        h = _gmlp_branch_merge(u, v, wts["ws"], wts["bs_tile"], wts["w_a"], attn, wts["w_b"], gates,
                               wts["w_o"], h, norm_post_mix[li])
        p_li = p[li].reshape(n, -1)
        if li % 2 == 0:
            hn = _norm_bf16(h, norm_pre_ffn[li])
            f = _dense_ffn(hn, w_dense_gu[li // 2].astype(BF16), w_dense_down[li // 2].astype(BF16))
            h = _ffn_post_ple(h, f, norm_post_ffn[li], p_li, wts["w_pi"], wts["w_pg"], ple_norm[li])
        else:
            route_i, route_w, counts = _router(h, norm_pre_ffn[li], w_router[li // 2])
            dest, tile_e, tile_groups, n_used, n_slots = _slot_tables(route_i, counts, n)
            xs = _dispatch(h, norm_pre_ffn[li], dest, n_slots)
            ys = _moe_ffn(xs, tile_e, tile_groups, n_used, w_exp_gu[li // 2], w_exp_down[li // 2])
            h = _moe_combine_post_ple(h, ys, dest, route_w, norm_post_ffn[li], p_li, wts["w_pi"],
                                      wts["w_pg"], ple_norm[li])
    return h.reshape(batch, seq, d)
```

```python
import functools

import jax
import jax.numpy as jnp
from jax import lax
from jax.experimental import pallas as pl
from jax.experimental.pallas import tpu as pltpu

F32 = jnp.float32
BF16 = jnp.bfloat16

GM_GROUPS = 8
GM_GROUP_W = 128
GM_W = GM_GROUPS * GM_GROUP_W
CHUNK = 128
MLA_HEADS = 8
Q_LORA = 512
KV_LORA = 256
QK_NOPE = 128
QK_ROPE = 64
V_HEAD = 128
ROPE_THETA = 10000.0
N_EXPERTS = 8
TOP_K = 2
MOE_BLOCK = 512
MOE_TILE = 2 * MOE_BLOCK
MOE_GROUP = MOE_TILE // 8
EPS = 1e-6
LOG2_E = 1.4426950408889634

LANES = 128
HEAD_PAD = 2 * LANES
BF16_SUBLANES = 16
V_ROWS = V_HEAD + BF16_SUBLANES

ATTN_TQ = 2048
ATTN_TK = 512
ROWS_RESIDENT = 512
ROWS_STREAM = 1024
COLS_STREAM = 1024
FF_TILE = 512
MOE_K_CHUNK = 256
ROWS_MIX = 512
ROWS_COMBINE = 256
V7X_VMEM_BYTES = 64 * 1024 * 1024
VMEM_LIMIT = V7X_VMEM_BYTES - 8 * 1024 * 1024


def _params(*sem):
    return pltpu.CompilerParams(dimension_semantics=sem, vmem_limit_bytes=VMEM_LIMIT)


def _tile(n, want):
    t = min(n, want)
    while n % t:
        t //= 2
    return t


def _rms(x, g):
    return x * lax.rsqrt(jnp.mean(x * x, axis=-1, keepdims=True) + EPS) * g


def _rope_table_kernel(pos_c_ref, pos_r_ref, inv_r_ref, sgn_r_ref, inv_c_ref, sgn_c_ref,
                       cos_ref, sin_ref, cos_t_ref, sin_t_ref):
    ang = pos_c_ref[...].astype(F32) * inv_r_ref[...]
    cos_ref[...] = jnp.cos(ang) * jnp.abs(sgn_r_ref[...])
    sin_ref[...] = jnp.sin(ang) * sgn_r_ref[...]
    ang_t = inv_c_ref[...] * pos_r_ref[...].astype(F32)
    cos_t_ref[...] = jnp.cos(ang_t) * jnp.abs(sgn_c_ref[...])
    sin_t_ref[...] = jnp.sin(ang_t) * sgn_c_ref[...]


def _rope_tables(positions):
    n = positions.size
    tm = _tile(n, ROWS_STREAM)
    half = QK_ROPE // 2
    inv = ROPE_THETA ** (-jnp.arange(0, QK_ROPE, 2, dtype=F32) / QK_ROPE)
    z = jnp.zeros((half,), F32)
    inv_slot = jnp.concatenate([inv, z, inv, z])
    one = jnp.ones((half,), F32)
    sgn_slot = jnp.concatenate([-one, z, one, z])
    row = pl.BlockSpec((1, LANES), lambda i: (0, 0))
    col = pl.BlockSpec((LANES, 1), lambda i: (0, 0))
    tab = pl.BlockSpec((tm, LANES), lambda i: (i, 0))
    tab_t = pl.BlockSpec((LANES, tm), lambda i: (0, i))
    return pl.pallas_call(
        _rope_table_kernel,
        grid=(n // tm,),
        in_specs=[pl.BlockSpec((tm, 1), lambda i: (i, 0)), pl.BlockSpec((1, tm), lambda i: (0, i)),
                  row, row, col, col],
        out_specs=[tab, tab, tab_t, tab_t],
        out_shape=[jax.ShapeDtypeStruct((n, LANES), F32)] * 2 + [jax.ShapeDtypeStruct((LANES, n), F32)] * 2,
        compiler_params=_params("parallel"),
        name="rope_tables",
    )(positions.reshape(n, 1), positions.reshape(1, n), inv_slot.reshape(1, LANES),
      sgn_slot.reshape(1, LANES), inv_slot.reshape(LANES, 1), sgn_slot.reshape(LANES, 1))


def _rope_slot(x, cos_t, sin_t):
    return x * cos_t + pltpu.roll(x, LANES // 2, axis=1) * sin_t


def _norm_kernel(h_ref, g_ref, o_ref):
    o_ref[...] = _rms(h_ref[...], g_ref[...]).astype(o_ref.dtype)


def _norm_bf16(h, g):
    n, d = h.shape
    tm = _tile(n, ROWS_RESIDENT)
    return pl.pallas_call(
        _norm_kernel,
        grid=(n // tm,),
        in_specs=[pl.BlockSpec((tm, d), lambda i: (i, 0)), pl.BlockSpec((1, d), lambda i: (0, 0))],
        out_specs=pl.BlockSpec((tm, d), lambda i: (i, 0)),
        out_shape=jax.ShapeDtypeStruct((n, d), BF16),
        compiler_params=_params("parallel"),
        name="rmsnorm",
    )(h, g.reshape(1, d))


ROUTE_E, ROUTE_RANK = 0, TOP_K


def _router_kernel(h_ref, g_ref, wr_ref, ri_ref, rw_ref, cnt_ref, carry):
    @pl.when(pl.program_id(0) == 0)
    def _():
        carry[...] = jnp.zeros_like(carry)

    y = _rms(h_ref[...], g_ref[...])
    w = wr_ref[...]
    y_hi, w_hi = y.astype(BF16), w.astype(BF16)
    y_lo = (y - y_hi.astype(F32)).astype(BF16)
    w_lo = (w - w_hi.astype(F32)).astype(BF16)
    logits = (jnp.dot(y_hi, w_hi, preferred_element_type=F32)
              + jnp.dot(y_lo, w_hi, preferred_element_type=F32)
              + jnp.dot(y_hi, w_lo, preferred_element_type=F32))
    tm = logits.shape[0]
    lane = lax.broadcasted_iota(jnp.int32, logits.shape, 1)
    lane_f = lane.astype(F32)
    lg = jnp.where(lane < N_EXPERTS, logits, -jnp.inf)
    tops, hots = [], []
    for _ in range(TOP_K):
        mx = jnp.max(lg, axis=1, keepdims=True)
        idx = jnp.min(jnp.where(lg == mx, lane_f, float(LANES)), axis=1, keepdims=True)
        hot = lane_f == idx
        tops.append((mx, idx))
        hots.append(hot)
        lg = jnp.where(hot, -jnp.inf, lg)
    e1 = jnp.exp(tops[1][0] - tops[0][0])
    gates = (1.0 / (1.0 + e1), e1 / (1.0 + e1))
    hot_any = hots[0].astype(F32) + hots[1].astype(F32)
    tri = (lax.broadcasted_iota(jnp.int32, (tm, tm), 0) > lax.broadcasted_iota(jnp.int32, (tm, tm), 1))
    before = jnp.dot(tri.astype(BF16), hot_any.astype(BF16), preferred_element_type=F32) + carry[...]
    ranks = [jnp.sum(jnp.where(hot, before, 0.0), axis=1, keepdims=True) for hot in hots]
    carry[...] += jnp.sum(hot_any, axis=0, keepdims=True)
    cnt_ref[...] = carry[...]
    rec = jnp.zeros(logits.shape, jnp.int32)
    wrec = jnp.zeros(logits.shape, F32)
    for k in range(TOP_K):
        rec = jnp.where(lane == ROUTE_E + k, tops[k][1].astype(jnp.int32), rec)
        rec = jnp.where(lane == ROUTE_RANK + k, ranks[k].astype(jnp.int32), rec)
        wrec = jnp.where(lane == k, gates[k], wrec)
    ri_ref[...] = rec
    rw_ref[...] = wrec


def _router(h, g, w_router):
    n, d = h.shape
    tm = _tile(n, ROWS_RESIDENT)
    wr = jnp.zeros((d, LANES), F32).at[:, :N_EXPERTS].set(w_router)
    rec = pl.BlockSpec((tm, LANES), lambda i: (i, 0))
    return pl.pallas_call(
        _router_kernel,
        grid=(n // tm,),
        in_specs=[pl.BlockSpec((tm, d), lambda i: (i, 0)), pl.BlockSpec((1, d), lambda i: (0, 0)),
                  pl.BlockSpec((d, LANES), lambda i: (0, 0))],
        out_specs=[rec, rec, pl.BlockSpec((1, LANES), lambda i: (0, 0))],
        out_shape=[jax.ShapeDtypeStruct((n, LANES), jnp.int32), jax.ShapeDtypeStruct((n, LANES), F32),
                   jax.ShapeDtypeStruct((1, LANES), F32)],
        scratch_shapes=[pltpu.VMEM((1, LANES), F32)],
        compiler_params=_params("arbitrary"),
        name="moe_router",
    )(h, g.reshape(1, d), wr)


def _pack_bf16_pairs(y):
    half = y.shape[1] // 2
    lo = lax.bitcast_convert_type(y[:, :half].astype(BF16).astype(F32), jnp.uint32)
    hi = lax.bitcast_convert_type(y[:, half:].astype(BF16).astype(F32), jnp.uint32)
    return hi | (lo >> 16)


def _unpack_bf16_pairs(w):
    lo = lax.bitcast_convert_type(w << 16, F32).astype(BF16)
    hi = lax.bitcast_convert_type(w & jnp.uint32(0xFFFF0000), F32).astype(BF16)
    return jnp.concatenate([lo, hi], axis=1)


def _row_copy_wait(src_row, dst_row, sem, count):
    def body(_, c):
        pltpu.make_async_copy(src_row, dst_row, sem).wait()
        return c
    lax.fori_loop(0, count, body, 0, unroll=16)


def _dispatch_kernel(dest_ref, h_ref, g_ref, xs_zero_ref, xs_ref, pbuf, sem):
    del xs_zero_ref
    i = pl.program_id(0)
    last = pl.num_programs(0) - 1
    slot = i % 2
    tm = h_ref.shape[0]

    def wait_slot(s):
        _row_copy_wait(pbuf.at[s, pl.ds(0, 1)], xs_ref.at[pl.ds(0, 1)], sem.at[s], TOP_K * tm)

    @pl.when(i >= 2)
    def _():
        wait_slot(slot)

    pbuf[slot] = _pack_bf16_pairs(_rms(h_ref[...], g_ref[...]))

    def send(r, c):
        for k in range(TOP_K):
            dst = dest_ref[0, 0, TOP_K * r + k]
            pltpu.make_async_copy(pbuf.at[slot, pl.ds(r, 1)], xs_ref.at[pl.ds(dst, 1)], sem.at[slot]).start()
        return c
    lax.fori_loop(0, tm, send, 0, unroll=8)

    @pl.when(i == last)
    def _():
        @pl.when(i >= 1)
        def _():
            wait_slot(1 - slot)
        wait_slot(slot)


def _dispatch(h, g, dest, n_slots):
    n, d = h.shape
    tm = _tile(n, ROWS_RESIDENT)
    nt = n // tm
    xs0 = jnp.zeros((n_slots, d // 2), jnp.uint32)
    return pl.pallas_call(
        _dispatch_kernel,
        grid=(nt,),
        in_specs=[pl.BlockSpec((1, 1, TOP_K * tm), lambda i: (i, 0, 0), memory_space=pltpu.SMEM),
                  pl.BlockSpec((tm, d), lambda i: (i, 0)), pl.BlockSpec((1, d), lambda i: (0, 0)),
                  pl.BlockSpec(memory_space=pl.ANY)],
        out_specs=pl.BlockSpec(memory_space=pl.ANY),
        out_shape=jax.ShapeDtypeStruct((n_slots, d // 2), jnp.uint32),
        scratch_shapes=[pltpu.VMEM((2, tm, d // 2), jnp.uint32), pltpu.SemaphoreType.DMA((2,))],
        input_output_aliases={3: 0},
        compiler_params=_params("arbitrary"),
        name="moe_dispatch",
    )(dest.reshape(nt, 1, TOP_K * tm), h, g.reshape(1, d), xs0)


def _mm_kernel(epi, n_extra, x_ref, w_ref, *rest):
    extras = rest[:n_extra]
    outs = rest[n_extra:]
    acc = jnp.dot(x_ref[...], w_ref[...], preferred_element_type=F32)
    res = epi(acc, *extras)
    if not isinstance(res, (tuple, list)):
        res = (res,)
    for o, r in zip(outs, res):
        o[...] = r.astype(o.dtype)


def _mm(x, w, epi, *, tm, tn, extras=(), outs, name):
    m, k = x.shape
    n = w.shape[1]
    tm = _tile(m, tm)
    tn = _tile(n, tn)

    def spec(ncols, kind):
        if kind == "tile":
            return pl.BlockSpec((tm, tn), lambda j, i: (i, j))
        if kind == "rows":
            return pl.BlockSpec((tm, ncols), lambda j, i: (i, 0))
        assert kind == "vec", kind
        return pl.BlockSpec((1, ncols), lambda j, i: (0, 0))

    in_specs = [pl.BlockSpec((tm, k), lambda j, i: (i, 0)), pl.BlockSpec((k, tn), lambda j, i: (0, j))]
    in_specs += [spec(a.shape[1], kind) for a, kind in extras]
    return pl.pallas_call(
        functools.partial(_mm_kernel, epi, len(extras)),
        grid=(n // tn, m // tm),
        in_specs=in_specs,
        out_specs=[spec(nc, kind) for nc, _, kind in outs],
        out_shape=[jax.ShapeDtypeStruct((m, nc), dt) for nc, dt, _ in outs],
        compiler_params=_params("parallel", "parallel"),
        name=name,
    )(x, w, *[a for a, _ in extras])


def _epi_gelu(acc):
    return jax.nn.gelu(acc)


def _epi_gelu_group_ln(acc, g_ref, b_ref):
    y = jax.nn.gelu(acc)
    g = g_ref[...]
    b = b_ref[...]
    cols = []
    for gi in range(acc.shape[1] // GM_GROUP_W):
        sl = slice(gi * GM_GROUP_W, (gi + 1) * GM_GROUP_W)
        yg = y[:, sl]
        mu = jnp.mean(yg, axis=-1, keepdims=True)
        dlt = yg - mu
        var = jnp.mean(dlt * dlt, axis=-1, keepdims=True)
        cols.append(dlt * lax.rsqrt(var + EPS) * g[:, sl] + b[:, sl])
    return jnp.concatenate(cols, axis=1)


def _epi_sigmoid(acc):
    return jax.nn.sigmoid(acc)


def _epi_latent(acc, qn_ref, kvn_ref, cos_ref, sin_ref):
    c_q = _rms(acc[:, :Q_LORA], qn_ref[...])
    c_kv = _rms(acc[:, Q_LORA:Q_LORA + KV_LORA], kvn_ref[...])
    k_r = _rope_slot(acc[:, Q_LORA + KV_LORA:], cos_ref[...], sin_ref[...])
    return c_q, c_kv, k_r


def _in_proj_kernel(h_ref, gn_ref, w_ref, lg_ref, lb_ref, qn_ref, kvn_ref, cos_ref, sin_ref,
                    hn_ref, u_ref, v_ref, cq_ref, ckv_ref, kr_ref):
    hn = _rms(h_ref[...], gn_ref[...]).astype(hn_ref.dtype)
    hn_ref[...] = hn
    acc = jnp.dot(hn, w_ref[...], preferred_element_type=F32)
    u_ref[...] = _epi_gelu(acc[:, :GM_W]).astype(u_ref.dtype)
    v_ref[...] = _epi_gelu_group_ln(acc[:, GM_W:2 * GM_W], lg_ref, lb_ref).astype(v_ref.dtype)
    c_q, c_kv, k_r = _epi_latent(acc[:, 2 * GM_W:], qn_ref, kvn_ref, cos_ref, sin_ref)
    cq_ref[...] = c_q.astype(cq_ref.dtype)
    ckv_ref[...] = c_kv.astype(ckv_ref.dtype)
    kr_ref[...] = k_r.astype(kr_ref.dtype)


def _in_proj(h, g_norm, w_uvl, ln_g, ln_b, q_norm, kv_norm, cos_t, sin_t):
    n, d = h.shape
    tm = _tile(n, ROWS_RESIDENT)
    vec = lambda a: (a.reshape(1, -1), pl.BlockSpec((1, a.size), lambda i: (0, 0)))
    rows = lambda w: pl.BlockSpec((tm, w), lambda i: (i, 0))
    vecs = [vec(g_norm), vec(ln_g), vec(ln_b), vec(q_norm), vec(kv_norm)]
    widths = [d, GM_W, GM_W, Q_LORA, KV_LORA, LANES]
    return pl.pallas_call(
        _in_proj_kernel,
        grid=(n // tm,),
        in_specs=[rows(d), vecs[0][1],
                  pl.BlockSpec(w_uvl.shape, lambda i: (0, 0), pipeline_mode=pl.Buffered(1)),
                  vecs[1][1], vecs[2][1], vecs[3][1], vecs[4][1], rows(LANES), rows(LANES)],
        out_specs=[rows(w) for w in widths],
        out_shape=[jax.ShapeDtypeStruct((n, w), BF16) for w in widths],
        compiler_params=_params("parallel"),
        name="in_proj_uvl",
    )(h, vecs[0][0], w_uvl, vecs[1][0], vecs[2][0], vecs[3][0], vecs[4][0], cos_t, sin_t)


def _q_proj_kernel(scale, c_ref, w_ref, cos_ref, sin_ref, o_ref):
    acc = lax.dot_general(w_ref[...], c_ref[...], (((1,), (1,)), ((), ())),
                          preferred_element_type=F32)
    cos_t = cos_ref[...] * scale
    sin_t = sin_ref[...] * scale
    half = LANES // 2
    for hd in range(MLA_HEADS):
        base = hd * HEAD_PAD
        o_ref[base:base + QK_NOPE, :] = (acc[base:base + QK_NOPE, :] * scale).astype(o_ref.dtype)
        r = acc[base + QK_NOPE:base + HEAD_PAD, :]
        rot = jnp.concatenate([r[half:, :], r[:half, :]], axis=0)
        o_ref[base + QK_NOPE:base + HEAD_PAD, :] = (r * cos_t + rot * sin_t).astype(o_ref.dtype)


def _q_proj(c_q, w_qt, cos_tt, sin_tt, scale):
    n, r = c_q.shape
    qw = w_qt.shape[0]
    tm = _tile(n, ROWS_RESIDENT)
    tab = pl.BlockSpec((LANES, tm), lambda i: (0, i))
    return pl.pallas_call(
        functools.partial(_q_proj_kernel, scale),
        grid=(n // tm,),
        in_specs=[pl.BlockSpec((tm, r), lambda i: (i, 0)), pl.BlockSpec((qw, r), lambda i: (0, 0)), tab, tab],
        out_specs=pl.BlockSpec((qw, tm), lambda i: (0, i)),
        out_shape=jax.ShapeDtypeStruct((qw, n), BF16),
        compiler_params=_params("parallel"),
        name="mla_q_proj",
    )(c_q, w_qt, cos_tt, sin_tt)


def _kv_kernel(c_ref, wk_ref, wvt_ref, kr_ref, k_ref, vt_ref):
    c = c_ref[...]
    kn = jnp.dot(c, wk_ref[...], preferred_element_type=F32).astype(k_ref.dtype)
    kr = kr_ref[...]
    for hd in range(MLA_HEADS):
        k_ref[:, hd * HEAD_PAD:hd * HEAD_PAD + QK_NOPE] = kn[:, hd * QK_NOPE:(hd + 1) * QK_NOPE]
        k_ref[:, hd * HEAD_PAD + QK_NOPE:(hd + 1) * HEAD_PAD] = kr
    vt = lax.dot_general(wvt_ref[...], c, (((1,), (1,)), ((), ())),
                         preferred_element_type=F32).astype(vt_ref.dtype)
    pad_rows = V_ROWS - V_HEAD
    ones_row = (lax.broadcasted_iota(jnp.int32, (pad_rows, vt.shape[1]), 0) == 0).astype(vt_ref.dtype)
    for hd in range(MLA_HEADS):
        vt_ref[hd * V_ROWS:hd * V_ROWS + V_HEAD, :] = vt[hd * V_HEAD:(hd + 1) * V_HEAD, :]
        vt_ref[hd * V_ROWS + V_HEAD:(hd + 1) * V_ROWS, :] = ones_row


def _kv_proj(c_kv, w_uk, w_uvt, k_rope):
    n, r = c_kv.shape
    tm = _tile(n, ROWS_RESIDENT)
    kw = MLA_HEADS * HEAD_PAD
    vw = MLA_HEADS * V_ROWS
    return pl.pallas_call(
        _kv_kernel,
        grid=(n // tm,),
        in_specs=[pl.BlockSpec((tm, r), lambda i: (i, 0)),
                  pl.BlockSpec(w_uk.shape, lambda i: (0, 0)),
                  pl.BlockSpec(w_uvt.shape, lambda i: (0, 0)),
                  pl.BlockSpec((tm, LANES), lambda i: (i, 0))],
        out_specs=[pl.BlockSpec((tm, kw), lambda i: (i, 0)), pl.BlockSpec((vw, tm), lambda i: (0, i))],
        out_shape=[jax.ShapeDtypeStruct((n, kw), BF16), jax.ShapeDtypeStruct((vw, n), BF16)],
        compiler_params=_params("parallel"),
        name="mla_kv_proj",
    )(c_kv, w_uk, w_uvt, k_rope)


def _attn_kernel(tk, p_dtype, qt_ref, k_ref, vt_ref, o_ref, s_a, s_b, p_a, p_b):
    tq = qt_ref.shape[1]
    n_chunks = k_ref.shape[0] // tk

    def scores(i, s_out):
        off = pl.multiple_of(i * tk, tk)
        s = jnp.dot(k_ref[pl.ds(off, tk), :], qt_ref[...], preferred_element_type=F32)
        s_out[...] = s
        return jnp.max(s, axis=0, keepdims=True)

    def softmax(s_in, p_out, mx, m):
        m_new = jnp.maximum(m, mx)
        x = s_in[...] - m_new
        p_out[...] = jnp.exp2(x.astype(p_dtype)).astype(p_out.dtype)
        return m_new, jnp.exp2(m - m_new)

    def values(i, p_in, alpha, acc):
        off = pl.multiple_of(i * tk, tk)
        return alpha * acc + jnp.dot(vt_ref[:, pl.ds(off, tk)], p_in[...], preferred_element_type=F32)

    def step(i, odd, carry, first=False, last=False):
        s_cur, s_nxt, p_cur, p_prv = (s_b, s_a, p_b, p_a) if odd else (s_a, s_b, p_a, p_b)
        mx, m, alpha, acc = carry
        if not first:
            acc = values(i - 1, p_prv, alpha, acc)
        mx_nxt = mx if last else scores(i + 1, s_nxt)
        m, alpha = softmax(s_cur, p_cur, mx, m)
        return mx_nxt, m, alpha, acc

    carry = (scores(0, s_a), jnp.full((1, tq), -jnp.inf, F32), jnp.ones((1, tq), F32),
             jnp.zeros((vt_ref.shape[0], tq), F32))
    for i in range(n_chunks):
        carry = step(i, i % 2 == 1, carry, first=i == 0, last=i == n_chunks - 1)
    _, _, alpha, acc = carry
    acc = values(n_chunks - 1, p_b if n_chunks % 2 == 0 else p_a, alpha, acc)
    o_ref[...] = jnp.transpose(acc[:V_HEAD, :] / acc[V_HEAD:V_HEAD + 1, :]).astype(o_ref.dtype)


def _attention(qt, k, vt, batch, seq, p_dtype):
    n = k.shape[0]
    tq = _tile(seq, ATTN_TQ)
    tk = _tile(seq // 2, ATTN_TK)
    assert (seq // tk) % 2 == 0
    nq = seq // tq
    return pl.pallas_call(
        functools.partial(_attn_kernel, tk, p_dtype),
        grid=(batch, MLA_HEADS, nq),
        scratch_shapes=[pltpu.VMEM((tk, tq), F32), pltpu.VMEM((tk, tq), F32),
                        pltpu.VMEM((tk, tq), BF16), pltpu.VMEM((tk, tq), BF16)],
        in_specs=[pl.BlockSpec((HEAD_PAD, tq), lambda b, h, i: (h, b * nq + i)),
                  pl.BlockSpec((seq, HEAD_PAD), lambda b, h, i: (b, h)),
                  pl.BlockSpec((V_ROWS, seq), lambda b, h, i: (h, b))],
        out_specs=pl.BlockSpec((tq, V_HEAD), lambda b, h, i: (b * nq + i, h)),
        out_shape=jax.ShapeDtypeStruct((n, MLA_HEADS * V_HEAD), BF16),
        compiler_params=_params("parallel", "parallel", "arbitrary"),
        name="mla_attention",
    )(qt, k, vt)


def _gmlp_merge_kernel(u_ref, v_ref, ws_ref, bs_ref, wa_ref, b_ref, wb_ref, ga_ref, gb_ref, wo_ref,
                       h_ref, gpost_ref, o_ref, gm):
    n_chunks = u_ref.shape[0] // CHUNK
    for c in range(n_chunks):
        rows = slice(c * CHUNK, (c + 1) * CHUNK)
        for g in range(GM_GROUPS):
            cols = slice(g * GM_GROUP_W, (g + 1) * GM_GROUP_W)
            mixed = jnp.dot(ws_ref[g], v_ref[rows, cols], preferred_element_type=F32)
            gm[rows, cols] = (u_ref[rows, cols].astype(F32) * (mixed + bs_ref[:, cols])).astype(gm.dtype)
    ya = jnp.dot(gm[...], wa_ref[...], preferred_element_type=F32)
    yb = jnp.dot(b_ref[...], wb_ref[...], preferred_element_type=F32)
    merged = (ga_ref[...].astype(F32) * ya + gb_ref[...].astype(F32) * yb).astype(BF16)
    mixed = jnp.dot(merged, wo_ref[...], preferred_element_type=F32)
    o_ref[...] = h_ref[...] + _rms(mixed, gpost_ref[...])


def _gmlp_branch_merge(u, v, ws, bs_tile, wa, attn, wb, gates, wo, h, g_post):
    n, w = u.shape
    kb = attn.shape[1]
    d = wa.shape[1]
    tm = _tile(n, ROWS_MIX)
    rows = lambda width, j=0: pl.BlockSpec((tm, width), lambda i: (i, j))
    resident = lambda a: pl.BlockSpec(a.shape, lambda i: (0,) * a.ndim, pipeline_mode=pl.Buffered(1))
    return pl.pallas_call(
        _gmlp_merge_kernel,
        grid=(n // tm,),
        in_specs=[rows(w), rows(w), resident(ws), resident(bs_tile), resident(wa), rows(kb), resident(wb),
                  rows(d, 0), rows(d, 1), resident(wo), rows(d), pl.BlockSpec((1, d), lambda i: (0, 0))],
        out_specs=rows(d),
        out_shape=jax.ShapeDtypeStruct((n, d), F32),
        scratch_shapes=[pltpu.VMEM((tm, w), BF16)],
        compiler_params=_params("parallel"),
        name="gmlp_merge_out_proj",
    )(u, v, ws, bs_tile, wa, attn, wb, gates, gates, wo, h, g_post.reshape(1, d))


def _swiglu_step(x, wg_ref, wu_ref, wd_ref):
    g = jnp.dot(x, wg_ref[...], preferred_element_type=F32)
    u = jnp.dot(x, wu_ref[...], preferred_element_type=F32)
    a = (jax.nn.silu(g) * u).astype(x.dtype)
    return jnp.dot(a, wd_ref[...], preferred_element_type=F32)


def _dense_ffn_kernel(x_ref, wg_ref, wu_ref, wd_ref, o_ref):
    f = pl.program_id(1)
    @pl.when(f == 0)
    def _():
        o_ref[...] = jnp.zeros_like(o_ref)

    o_ref[...] += _swiglu_step(x_ref[...], wg_ref, wu_ref, wd_ref)


def _dense_ffn(x, w_gu, w_down):
    n, d = x.shape
    dff = w_down.shape[0]
    tm = _tile(n, ROWS_STREAM)
    tf = _tile(dff, FF_TILE)
    nf = dff // tf
    return pl.pallas_call(
        _dense_ffn_kernel,
        grid=(n // tm, nf),
        in_specs=[pl.BlockSpec((tm, d), lambda i, f: (i, 0)),
                  pl.BlockSpec((d, tf), lambda i, f: (0, f)),
                  pl.BlockSpec((d, tf), lambda i, f: (0, f + nf)),
                  pl.BlockSpec((tf, d), lambda i, f: (f, 0))],
        out_specs=pl.BlockSpec((tm, d), lambda i, f: (i, 0)),
        out_shape=jax.ShapeDtypeStruct((n, d), F32),
        compiler_params=_params("parallel", "arbitrary"),
        name="dense_swiglu",
    )(x, w_gu, w_gu, w_down)


def _moe_ffn_kernel(te_ref, th_ref, nu_ref, x_ref, wg_ref, wu_ref, wd_ref, o_ref):
    t = pl.program_id(0)
    f = pl.program_id(1)
    n_grp = th_ref[t]
    full_grps = MOE_TILE // MOE_GROUP
    half_grps = full_grps // 2

    @pl.when(f == 0)
    def _():
        o_ref[...] = jnp.zeros_like(o_ref)

    @pl.when(n_grp > 0)
    def _():
        def chunked_dot(lhs, w_ref):
            kc = MOE_K_CHUNK
            acc = jnp.dot(lhs[:, :kc], w_ref[:kc, :].astype(BF16), preferred_element_type=F32)
            for k0 in range(kc, w_ref.shape[0], kc):
                acc = acc + jnp.dot(lhs[:, k0:k0 + kc], w_ref[k0:k0 + kc, :].astype(BF16),
                                    preferred_element_type=F32)
            return acc

        def ffn_rows(start, size):
            rows = pl.ds(start, size)
            x = _unpack_bf16_pairs(x_ref[rows, :])
            g = chunked_dot(x, wg_ref)
            u = chunked_dot(x, wu_ref)
            y = chunked_dot((jax.nn.silu(g) * u).astype(BF16), wd_ref)
            o_ref[rows, :] += y

        @pl.when(n_grp == full_grps)
        def _():
            ffn_rows(0, MOE_TILE)

        @pl.when(n_grp < full_grps)
        def _():
            has_half = n_grp >= half_grps
            rem = n_grp - jnp.where(has_half, half_grps, 0)
            base = jnp.where(has_half, MOE_TILE // 2, 0)

            @pl.when(has_half)
            def _():
                ffn_rows(0, MOE_TILE // 2)

            @pl.when(rem >= 2)
            def _():
                ffn_rows(pl.multiple_of(base, MOE_GROUP), 2 * MOE_GROUP)

            @pl.when(rem % 2 == 1)
            def _():
                ffn_rows(pl.multiple_of(base + jnp.where(rem >= 2, 2 * MOE_GROUP, 0), MOE_GROUP), MOE_GROUP)


def _moe_ffn(xs, tile_e, tile_groups, n_used, w_gu, w_down):
    assert MOE_TILE == 8 * MOE_GROUP
    n_slots = xs.shape[0]
    d = w_down.shape[2]
    dff = w_down.shape[1]
    tm = MOE_TILE
    tf = _tile(dff, FF_TILE)
    nf = dff // tf

    def _tt(t, nu):
        return jnp.minimum(t, jnp.maximum(nu[0] - 1, 0))

    def _e(t, te, nu):
        return te[_tt(t, nu)]

    def _ff(t, f, nu):
        return jnp.where(t < nu[0], f, nf - 1)

    grid_spec = pltpu.PrefetchScalarGridSpec(
        num_scalar_prefetch=3,
        grid=(n_slots // tm, nf),
        in_specs=[
            pl.BlockSpec((tm, d // 2), lambda t, f, te, th, nu: (_tt(t, nu), 0)),
            pl.BlockSpec((None, d, tf), lambda t, f, te, th, nu: (_e(t, te, nu), 0, _ff(t, f, nu))),
            pl.BlockSpec((None, d, tf), lambda t, f, te, th, nu: (_e(t, te, nu), 0, _ff(t, f, nu) + nf)),
            pl.BlockSpec((None, tf, d), lambda t, f, te, th, nu: (_e(t, te, nu), _ff(t, f, nu), 0)),
        ],
        out_specs=pl.BlockSpec((tm, d), lambda t, f, te, th, nu: (t, 0), pipeline_mode=pl.Buffered(1)),
    )
    return pl.pallas_call(
        _moe_ffn_kernel,
        grid_spec=grid_spec,
        out_shape=jax.ShapeDtypeStruct((n_slots, d), F32),
        compiler_params=_params("arbitrary", "arbitrary"),
        name="moe_swiglu",
    )(tile_e, tile_groups, n_used, xs, w_gu, w_gu, w_down)


def _post_ple_math(h, f, gf_ref, p_ref, wp_ref, wg_ref, gp_ref, o_ref):
    h = h + _rms(f, gf_ref[...])
    e = jnp.dot(p_ref[...].astype(BF16), wp_ref[...], preferred_element_type=F32)
    gate = jax.nn.sigmoid(jnp.dot(h.astype(BF16), wg_ref[...], preferred_element_type=F32))
    o_ref[...] = h + _rms(gate * e, gp_ref[...])


def _ple_kernel(h_ref, f_ref, gf_ref, p_ref, wp_ref, wg_ref, gp_ref, o_ref):
    _post_ple_math(h_ref[...], f_ref[...], gf_ref, p_ref, wp_ref, wg_ref, gp_ref, o_ref)


def _ple_specs(n, d, pd, tm):
    blk = pl.BlockSpec((tm, d), lambda i: (i, 0))
    row = pl.BlockSpec((1, d), lambda i: (0, 0))
    tail = [row, pl.BlockSpec((tm, pd), lambda i: (i, 0)), pl.BlockSpec((pd, d), lambda i: (0, 0)),
            pl.BlockSpec((d, d), lambda i: (0, 0)), row]
    return blk, tail


def _ffn_post_ple(h, f, g_post, p, w_ple_in, w_ple_gate, g_ple):
    n, d = h.shape
    tm = _tile(n, ROWS_RESIDENT)
    blk, tail = _ple_specs(n, d, p.shape[1], tm)
    return pl.pallas_call(
        _ple_kernel,
        grid=(n // tm,),
        in_specs=[blk, blk] + tail,
        out_specs=blk,
        out_shape=jax.ShapeDtypeStruct((n, d), F32),
        compiler_params=_params("parallel"),
        name="ffn_post_ple",
    )(h, f, g_post.reshape(1, d), p, w_ple_in, w_ple_gate, g_ple.reshape(1, d))


def _ple_combine_kernel(d_cur_ref, d_nxt_ref, h_ref, rw_ref, gf_ref, p_ref, wp_ref, wg_ref, gp_ref,
                        ys_ref, o_ref, gbuf, sem):
    i = pl.program_id(0)
    last = pl.num_programs(0) - 1
    slot = i % 2
    tm = h_ref.shape[0]

    def fetch(d_ref, s):
        def body(r, c):
            for k in range(TOP_K):
                src = d_ref[0, 0, TOP_K * r + k]
                pltpu.make_async_copy(ys_ref.at[pl.ds(src, 1)], gbuf.at[s, k, pl.ds(r, 1)], sem.at[s]).start()
            return c
        lax.fori_loop(0, tm, body, 0, unroll=8)

    @pl.when(i == 0)
    def _():
        fetch(d_cur_ref, 0)

    @pl.when(i < last)
    def _():
        fetch(d_nxt_ref, 1 - slot)

    _row_copy_wait(ys_ref.at[pl.ds(0, 1)], gbuf.at[slot, 0, pl.ds(0, 1)], sem.at[slot], TOP_K * tm)
    f = rw_ref[:, 0:1] * gbuf[slot, 0]
    for k in range(1, TOP_K):
        f = f + rw_ref[:, k:k + 1] * gbuf[slot, k]
    _post_ple_math(h_ref[...], f, gf_ref, p_ref, wp_ref, wg_ref, gp_ref, o_ref)


def _moe_combine_post_ple(h, ys, dest, route_w, g_post, p, w_ple_in, w_ple_gate, g_ple):
    n, d = h.shape
    tm = _tile(n, ROWS_COMBINE)
    nt = n // tm
    blk, tail = _ple_specs(n, d, p.shape[1], tm)
    dest3 = dest.reshape(nt, 1, TOP_K * tm)
    idx = functools.partial(pl.BlockSpec, (1, 1, TOP_K * tm), memory_space=pltpu.SMEM)
    return pl.pallas_call(
        _ple_combine_kernel,
        grid=(nt,),
        in_specs=[idx(lambda i: (i, 0, 0)), idx(lambda i: (jnp.minimum(i + 1, nt - 1), 0, 0)),
                  blk, pl.BlockSpec((tm, LANES), lambda i: (i, 0))] + tail
                 + [pl.BlockSpec(memory_space=pl.ANY)],
        out_specs=blk,
        out_shape=jax.ShapeDtypeStruct((n, d), F32),
        scratch_shapes=[pltpu.VMEM((2, TOP_K, tm, d), F32), pltpu.SemaphoreType.DMA((2,))],
        compiler_params=_params("arbitrary"),
        name="moe_combine_post_ple",
    )(dest3, dest3, h, route_w, g_post.reshape(1, d), p, w_ple_in, w_ple_gate, g_ple.reshape(1, d), ys)


def _slot_tables(route_i, counts_f, n_tok):
    a = n_tok * TOP_K
    counts = counts_f[0, :N_EXPERTS].astype(jnp.int32)
    padded = (counts + MOE_TILE - 1) // MOE_TILE * MOE_TILE
    pad_end = jnp.cumsum(padded)
    pad_start = pad_end - padded
    experts = route_i[:, ROUTE_E:ROUTE_E + TOP_K]
    dest = pad_start[experts] + route_i[:, ROUTE_RANK:ROUTE_RANK + TOP_K]
    n_tiles = -(-a // MOE_TILE) + N_EXPERTS
    tile_start = jnp.arange(n_tiles, dtype=jnp.int32) * MOE_TILE
    tile_e = jnp.minimum(jnp.sum((pad_end[None, :] <= tile_start[:, None]).astype(jnp.int32), axis=1),
                         N_EXPERTS - 1)
    rows = jnp.clip(pad_start[tile_e] + counts[tile_e] - tile_start, 0, MOE_TILE)
    tile_groups = ((rows + MOE_GROUP - 1) // MOE_GROUP).astype(jnp.int32)
    n_used = (pad_end[-1] // MOE_TILE).astype(jnp.int32).reshape(1)
    return dest.astype(jnp.int32), tile_e, tile_groups, n_used, n_tiles * MOE_TILE


def _rope_slot_cols(w):
    half = QK_ROPE // 2
    z = jnp.zeros(w.shape[:-1] + (half,), w.dtype)
    return jnp.concatenate([w[..., :half], z, w[..., half:], z], axis=-1)


def _prep_layer(i, w_in, gm_ws, gm_bs, w_uq, w_ukv, w_branch_a, w_branch_b, w_out, w_ple_in, w_ple_gate):
    d = w_in.shape[1]
    wi = w_in[i]
    o = 0
    w_u = wi[:, o:o + GM_W]; o += GM_W
    w_v = wi[:, o:o + GM_W]; o += GM_W
    w_cq = wi[:, o:o + Q_LORA]; o += Q_LORA
    w_ckv = wi[:, o:o + KV_LORA]; o += KV_LORA
    w_kr = wi[:, o:o + QK_ROPE]; o += QK_ROPE
    w_g = wi[:, o:o + 2 * d]
    w_uvl = jnp.concatenate([w_u, w_v, w_cq, w_ckv, _rope_slot_cols(w_kr)], axis=1)
    wq = w_uq[i].reshape(Q_LORA, MLA_HEADS, QK_NOPE + QK_ROPE)
    wq = jnp.concatenate([wq[..., :QK_NOPE], _rope_slot_cols(wq[..., QK_NOPE:])], axis=-1)
    wkv = w_ukv[i].reshape(KV_LORA, MLA_HEADS, QK_NOPE + V_HEAD)
    bs_tile = jnp.repeat(gm_bs[i].T, GM_GROUP_W, axis=1)
    return dict(
        w_uvl=w_uvl.astype(BF16), w_g=w_g.astype(BF16),
        w_qt=wq.reshape(Q_LORA, MLA_HEADS * HEAD_PAD).T.astype(BF16),
        w_uk=wkv[..., :QK_NOPE].reshape(KV_LORA, MLA_HEADS * QK_NOPE).astype(BF16),
        w_uvt=wkv[..., QK_NOPE:].reshape(KV_LORA, MLA_HEADS * V_HEAD).T.astype(BF16),
        ws=gm_ws[i].astype(BF16), bs_tile=bs_tile,
        w_a=w_branch_a[i].astype(BF16), w_b=w_branch_b[i].astype(BF16), w_o=w_out[i].astype(BF16),
        w_pi=w_ple_in[i].astype(BF16), w_pg=w_ple_gate[i].astype(BF16),
    )


def kernel(x, p, positions, norm_pre_mix, norm_post_mix, norm_pre_ffn, norm_post_ffn, w_in, gm_ln_g, gm_ln_b, gm_ws, gm_bs, mla_q_norm, w_uq, mla_kv_norm, w_ukv, w_branch_a, w_branch_b, w_out, w_dense_gu, w_dense_down, w_router, w_exp_gu, w_exp_down, w_ple_in, w_ple_gate, ple_norm):
    batch, seq, d = x.shape
    depth = p.shape[0]
    n = batch * seq
    scale = (QK_NOPE + QK_ROPE) ** -0.5 * LOG2_E
    cos_t, sin_t, cos_tt, sin_tt = _rope_tables(positions)
    h = x.reshape(n, d)

    for li in range(depth):
        wts = _prep_layer(li, w_in, gm_ws, gm_bs, w_uq, w_ukv, w_branch_a, w_branch_b, w_out,
                          w_ple_in, w_ple_gate)
        hn, u, v, c_q, c_kv, k_r = _in_proj(h, norm_pre_mix[li], wts["w_uvl"], gm_ln_g[li], gm_ln_b[li],
                                            mla_q_norm[li], mla_kv_norm[li], cos_t, sin_t)
        (gates,) = _mm(hn, wts["w_g"], _epi_sigmoid, tm=ROWS_STREAM, tn=COLS_STREAM, outs=[(2 * d, BF16, "tile")],
                       name="in_proj_gates")
        qt = _q_proj(c_q, wts["w_qt"], cos_tt, sin_tt, scale)
        k, vt = _kv_proj(c_kv, wts["w_uk"], wts["w_uvt"], k_r)
        attn = _attention(qt, k, vt, batch, seq, F32)
        h = _gmlp_branch_merge(u, v, wts["ws"], wts["bs_tile"], wts["w_a"], attn, wts["w_b"], gates,
                               wts["w_o"], h, norm_post_mix[li])
        p_li = p[li].reshape(n, -1)
        if li % 2 == 0:
            hn = _norm_bf16(h, norm_pre_ffn[li])
            f = _dense_ffn(hn, w_dense_gu[li // 2].astype(BF16), w_dense_down[li // 2].astype(BF16))
            h = _ffn_post_ple(h, f, norm_post_ffn[li], p_li, wts["w_pi"], wts["w_pg"], ple_norm[li])
        else:
            route_i, route_w, counts = _router(h, norm_pre_ffn[li], w_router[li // 2])
            dest, tile_e, tile_groups, n_used, n_slots = _slot_tables(route_i, counts, n)
            xs = _dispatch(h, norm_pre_ffn[li], dest, n_slots)
            ys = _moe_ffn(xs, tile_e, tile_groups, n_used, w_exp_gu[li // 2], w_exp_down[li // 2])
            h = _moe_combine_post_ple(h, ys, dest, route_w, norm_post_ffn[li], p_li, wts["w_pi"],
                                      wts["w_pg"], ple_norm[li])
    return h.reshape(batch, seq, d)
```
